```python
import jax, jax.numpy as jnp
from jax import lax
import numpy as np

D_MODEL = 2048
BATCH = 2
SEQ = 4096
DEPTH = 1

MLA_HEADS = 8
MLA_Q_RANK = 512
MLA_KV_RANK = 256
MLA_NOPE = 128
MLA_ROPE = 64
MLA_V = 128
ROPE_THETA = 10000.0
Q_BLOCK = 128
GLA_HEADS = 4
GLA_DK = 128
GLA_DV = 256
GLA_GATE_RANK = 16
GLA_GATE_NORM = 16.0
GLA_CHUNK = 64
N_EXPERTS = 32
TOP_K = 4
D_EXPERT = D_MODEL
SWIGLU_LIMIT = 7.0
SWIGLU_ALPHA = 1.702
EXPERT_BLOCK = 256
EPS = 1e-6
N_MOD = 6

IN_SPLITS = (
    MLA_Q_RANK,
    MLA_KV_RANK + MLA_ROPE,
    GLA_HEADS * GLA_DK,
    GLA_HEADS * GLA_DK,
    GLA_HEADS * GLA_DV,
    GLA_GATE_RANK,
    GLA_HEADS * GLA_DV,
    D_MODEL,
    D_MODEL,
)
D_IN = sum(IN_SPLITS)
IN_OFFSETS = tuple(sum(IN_SPLITS[: i + 1]) for i in range(len(IN_SPLITS) - 1))

kernel_name = "hybrid_mla_gla_moe_adaln"


def rms_norm(x, w):
    xf = x.astype(jnp.float32)
    y = xf * lax.rsqrt(jnp.mean(xf * xf, axis=-1, keepdims=True) + EPS)
    return (y * w.astype(jnp.float32)).astype(x.dtype)


def modulate(h, shift, scale):
    return h * (1 + scale[:, None, :]) + shift[:, None, :]


def apply_rope(x, cos, sin):
    xf = x.astype(jnp.float32).reshape(*x.shape[:-1], MLA_ROPE // 2, 2)
    x_re, x_im = xf[..., 0], xf[..., 1]
    out = jnp.stack([x_re * cos - x_im * sin, x_re * sin + x_im * cos], axis=-1)
    return out.reshape(x.shape).astype(x.dtype)


def mla_attention(q_lat, kv_lat, positions, q_norm_w, w_q_b, kv_norm_w, w_kv_b):
    B, S, _ = q_lat.shape
    H = MLA_HEADS
    q = (rms_norm(q_lat, q_norm_w) @ w_q_b).reshape(B, S, H, MLA_NOPE + MLA_ROPE)
    q_nope, q_pe = q[..., :MLA_NOPE], q[..., MLA_NOPE:]
    c_kv, k_pe = kv_lat[..., :MLA_KV_RANK], kv_lat[..., MLA_KV_RANK:]
    kv = (rms_norm(c_kv, kv_norm_w) @ w_kv_b).reshape(B, S, H, MLA_NOPE + MLA_V)
    k_nope, v = kv[..., :MLA_NOPE], kv[..., MLA_NOPE:]
    inv_freq = ROPE_THETA ** (-(jnp.arange(0, MLA_ROPE, 2, dtype=jnp.float32) / MLA_ROPE))
    ang = positions.astype(jnp.float32)[..., None] * inv_freq
    cos, sin = jnp.cos(ang), jnp.sin(ang)
    q_pe = apply_rope(q_pe, cos[:, :, None, :], sin[:, :, None, :])
    k_pe = apply_rope(k_pe, cos, sin)
    scale = (MLA_NOPE + MLA_ROPE) ** -0.5
    nb = S // Q_BLOCK
    qn_blocks = q_nope.reshape(B, nb, Q_BLOCK, H, MLA_NOPE).transpose(1, 0, 2, 3, 4)
    qp_blocks = q_pe.reshape(B, nb, Q_BLOCK, H, MLA_ROPE).transpose(1, 0, 2, 3, 4)
    key_idx = jnp.arange(S)

    def query_block(args):
        qn, qp, blk = args
        s = jnp.einsum('bqhd,bkhd->bhqk', qn, k_nope, preferred_element_type=jnp.float32)
        s = s + jnp.einsum('bqhr,bkr->bhqk', qp, k_pe, preferred_element_type=jnp.float32)
        q_idx = blk * Q_BLOCK + jnp.arange(Q_BLOCK)
        mask = key_idx[None, :] <= q_idx[:, None]
        s = jnp.where(mask, s * scale, -jnp.inf)
        p = jax.nn.softmax(s, axis=-1).astype(v.dtype)
        return jnp.einsum('bhqk,bkhd->bqhd', p, v)

    o = lax.map(query_block, (qn_blocks, qp_blocks, jnp.arange(nb)))
    return o.transpose(1, 0, 2, 3, 4).reshape(B, S, H * MLA_V)


def gla_attention(q, k, v, gate_lr, g_out, w_gk_up, b_gk_up, norm_w):
    B, S, _ = q.shape
    H, DK, DV, C = GLA_HEADS, GLA_DK, GLA_DV, GLA_CHUNK
    n_chunks = S // C
    gk = jax.nn.log_sigmoid((gate_lr @ w_gk_up + b_gk_up).astype(jnp.float32)) / GLA_GATE_NORM

    def to_chunks(t, d):
        return t.astype(jnp.float32).reshape(B, n_chunks, C, H, d).transpose(1, 0, 3, 2, 4)

    qc = to_chunks(q, DK) * (DK ** -0.5)
    kc, vc, gc = to_chunks(k, DK), to_chunks(v, DV), to_chunks(gk, DK)
    causal = jnp.tril(jnp.ones((C, C), dtype=bool))

    def chunk_step(state, inp):
        qi, ki, vi, gi = inp
        b = jnp.cumsum(gi, axis=2)
        o_inter = jnp.einsum('bhcd,bhdv->bhcv', qi * jnp.exp(b), state)
        diff = b[:, :, :, None, :] - b[:, :, None, :, :]
        decay = jnp.exp(jnp.where(causal[:, :, None], diff, -jnp.inf))
        a = jnp.einsum('bhid,bhjd,bhijd->bhij', qi, ki, decay)
        o_intra = jnp.einsum('bhij,bhjv->bhiv', a, vi)
        b_last = b[:, :, -1:, :]
        state = state * jnp.exp(b_last[:, :, 0, :])[..., None] + jnp.einsum(
            'bhjd,bhjv->bhdv', ki * jnp.exp(b_last - b), vi)
        return state, o_inter + o_intra

    s0 = jnp.zeros((B, H, DK, DV), jnp.float32)
    _, o = lax.scan(chunk_step, s0, (qc, kc, vc, gc))
    o = o.transpose(1, 0, 3, 2, 4).reshape(B, S, H, DV)
    o = rms_norm(o, norm_w) * jax.nn.silu(g_out.astype(jnp.float32).reshape(B, S, H, DV))
    return o.reshape(B, S, H * DV).astype(q.dtype)


def moe_ffn(h, w_router, b_router, w1, b1, w2, b2):
    B, S, D = h.shape
    N = B * S
    E, BLK = N_EXPERTS, EXPERT_BLOCK
    xt = h.reshape(N, D)
    logits = (xt @ w_router + b_router).astype(jnp.float32)
    top_val, top_idx = lax.top_k(logits, TOP_K)
    top_w = jax.nn.softmax(top_val, axis=-1)
    A = N * TOP_K
    e_flat = top_idx.reshape(A)
    tok_flat = jnp.repeat(jnp.arange(N, dtype=jnp.int32), TOP_K)
    w_flat = top_w.reshape(A)
    onehot = jax.nn.one_hot(e_flat, E, dtype=jnp.int32)
    counts = onehot.sum(axis=0)
    rank = (jnp.cumsum(onehot, axis=0) * onehot).sum(axis=-1) - 1
    padded = (counts + BLK - 1) // BLK * BLK
    pad_end = jnp.cumsum(padded)
    pad_start = pad_end - padded
    dest = pad_start[e_flat] + rank
    n_blocks = -(-A // BLK) + E
    P = n_blocks * BLK
    slot_tok = jnp.full((P,), N, dtype=jnp.int32).at[dest].set(tok_flat)
    slot_w = jnp.zeros((P,), jnp.float32).at[dest].set(w_flat)
    block_expert = jnp.minimum(
        jnp.searchsorted(pad_end, jnp.arange(n_blocks) * BLK, side='right'), E - 1)
    x_pad = jnp.concatenate([xt, jnp.zeros((1, D), xt.dtype)], axis=0)
    xs = x_pad[slot_tok].reshape(n_blocks, BLK, D)

    def expert_block(args):
        xb, e = args
        hm = xb @ w1[e] + b1[e]
        x_glu, x_lin = hm[:, ::2], hm[:, 1::2]
        x_glu = jnp.minimum(x_glu, SWIGLU_LIMIT)
        x_lin = jnp.clip(x_lin, -SWIGLU_LIMIT, SWIGLU_LIMIT)
        act = x_glu * jax.nn.sigmoid(SWIGLU_ALPHA * x_glu) * (x_lin + 1)
        return act @ w2[e] + b2[e]

    ys = lax.map(expert_block, (xs, block_expert)).reshape(P, D)
    out = jnp.zeros((N + 1, D), jnp.float32).at[slot_tok].add(
        ys.astype(jnp.float32) * slot_w[:, None])
    return out[:N].reshape(B, S, D).astype(h.dtype)


def setup_inputs(seed: int = 0) -> dict:
    key = jax.random.key(seed)
    ks = jax.random.split(key, 32)
    f32 = jnp.float32
    L, D = DEPTH, D_MODEL

    def nrm(k, shape, fan_in):
        return jax.random.normal(k, shape, f32) * (fan_in ** -0.5)

    def gain(k, shape):
        return 1.0 + 0.02 * jax.random.normal(k, shape, f32)

    def bias(k, shape):
        return 0.01 * jax.random.normal(k, shape, f32)

    x = jax.random.normal(ks[0], (BATCH, SEQ, D), f32)
    c = jax.random.normal(ks[1], (BATCH, D), f32)
    offsets = jax.random.randint(ks[2], (BATCH, 1), 0, 1024, dtype=jnp.int32)
    positions = offsets + jnp.arange(SEQ, dtype=jnp.int32)[None, :]
    return {
        "x": x,
        "c": c,
        "positions": positions,
        "w_ada": 0.5 * nrm(ks[3], (L, D, N_MOD * D), D),
        "b_ada": bias(ks[4], (L, N_MOD * D)),
        "norm_mix_w": gain(ks[5], (L, D)),
        "w_in": nrm(ks[6], (L, D, D_IN), D),
        "mla_q_norm_w": gain(ks[7], (L, MLA_Q_RANK)),
        "mla_w_q_b": nrm(ks[8], (L, MLA_Q_RANK, MLA_HEADS * (MLA_NOPE + MLA_ROPE)), MLA_Q_RANK),
        "mla_kv_norm_w": gain(ks[9], (L, MLA_KV_RANK)),
        "mla_w_kv_b": nrm(ks[10], (L, MLA_KV_RANK, MLA_HEADS * (MLA_NOPE + MLA_V)), MLA_KV_RANK),
        "gla_w_gk_up": nrm(ks[11], (L, GLA_GATE_RANK, GLA_HEADS * GLA_DK), GLA_GATE_RANK),
        "gla_b_gk_up": bias(ks[12], (L, GLA_HEADS * GLA_DK)),
        "gla_norm_w": gain(ks[13], (L, GLA_DV)),
        "w_o_mla": nrm(ks[14], (L, MLA_HEADS * MLA_V, D), MLA_HEADS * MLA_V),
        "w_o_gla": nrm(ks[15], (L, GLA_HEADS * GLA_DV, D), GLA_HEADS * GLA_DV),
        "w_out": nrm(ks[16], (L, D, D), D),
        "norm_ffn_w": gain(ks[17], (L, D)),
        "w_router": nrm(ks[18], (L, D, N_EXPERTS), D),
        "b_router": bias(ks[19], (L, N_EXPERTS)),
        "w1": nrm(ks[20], (L, N_EXPERTS, D, 2 * D_EXPERT), D),
        "b1": bias(ks[21], (L, N_EXPERTS, 2 * D_EXPERT)),
        "w2": nrm(ks[22], (L, N_EXPERTS, D_EXPERT, D), D_EXPERT),
        "b2": bias(ks[23], (L, N_EXPERTS, D)),
        "w_ada_final": 0.5 * nrm(ks[24], (D, 2 * D), D),
        "b_ada_final": bias(ks[25], (2 * D,)),
        "norm_final_w": gain(ks[26], (D,)),
    }


def reference(x, c, positions, w_ada, b_ada, norm_mix_w, w_in, mla_q_norm_w, mla_w_q_b,
              mla_kv_norm_w, mla_w_kv_b, gla_w_gk_up, gla_b_gk_up, gla_norm_w, w_o_mla,
              w_o_gla, w_out, norm_ffn_w, w_router, b_router, w1, b1, w2, b2,
              w_ada_final, b_ada_final, norm_final_w):
    cond = jax.nn.silu(c)
    for l in range(DEPTH):
        mod = cond @ w_ada[l] + b_ada[l]
        sh_m, sc_m, g_m, sh_f, sc_f, g_f = jnp.split(mod, N_MOD, axis=-1)
        h = modulate(rms_norm(x, norm_mix_w[l]), sh_m, sc_m)
        proj = h @ w_in[l]
        q_lat, kv_lat, gq, gk, gv, g_lr, g_out, gate_a, gate_b = jnp.split(proj, IN_OFFSETS, axis=-1)
        y_a = mla_attention(q_lat, kv_lat, positions, mla_q_norm_w[l], mla_w_q_b[l],
                            mla_kv_norm_w[l], mla_w_kv_b[l]) @ w_o_mla[l]
        y_b = gla_attention(gq, gk, gv, g_lr, g_out, gla_w_gk_up[l], gla_b_gk_up[l],
                            gla_norm_w[l]) @ w_o_gla[l]
        merged = jax.nn.sigmoid(gate_a) * y_a + jax.nn.sigmoid(gate_b) * y_b
        x = x + g_m[:, None, :] * (merged @ w_out[l])
        h = modulate(rms_norm(x, norm_ffn_w[l]), sh_f, sc_f)
        x = x + g_f[:, None, :] * moe_ffn(h, w_router[l], b_router[l], w1[l], b1[l], w2[l], b2[l])
    fmod = cond @ w_ada_final + b_ada_final
    sh_o, sc_o = jnp.split(fmod, 2, axis=-1)
    return modulate(rms_norm(x, norm_final_w), sh_o, sc_o)
```

```python
import functools

import jax
import jax.numpy as jnp
import numpy as np
from jax import lax
from jax.experimental import pallas as pl
from jax.experimental.pallas import tpu as pltpu

F32 = jnp.float32
BF16 = jnp.bfloat16

D_MODEL = 2048
BATCH = 2
SEQ = 4096
N_TOK = BATCH * SEQ

MLA_HEADS = 8
MLA_Q_RANK = 512
MLA_KV_RANK = 256
MLA_NOPE = 128
MLA_ROPE = 64
MLA_V = 128
ROPE_THETA = 10000.0
MLA_HD = 256

GLA_HEADS = 4
GLA_DK = 128
GLA_DV = 256
GLA_GATE_RANK = 16
GLA_GATE_NORM = 16.0
GLA_CHUNK = 64

N_EXPERTS = 32
TOP_K = 4
D_EXPERT = D_MODEL
SWIGLU_LIMIT = 7.0
SWIGLU_ALPHA = 1.702
EPS = 1e-6
N_MOD = 6

LANES = 128
VMEM_LIMIT = 56 * 1024 * 1024

P_QLAT = 0
P_CKV = 512
P_KPE = 768
P_GLR = 896
P_GQ = 1024
P_GK = 1536
P_GV = 2048
P_GOUT = 3072
P_GATEA = 4096
P_GATEB = 6144
P_TOTAL = 8192

MOE_BLK = 256
MOE_ITEM_BLKS = 4
MOE_ITEM_ROWS = MOE_BLK * MOE_ITEM_BLKS
MOE_TH = 256
MOE_NCHUNK = D_EXPERT // MOE_TH
N_ASSIGN = N_TOK * TOP_K
MOE_NBLK_MAX = N_ASSIGN // MOE_BLK + N_EXPERTS
MOE_SLOTS = MOE_NBLK_MAX * MOE_BLK
MOE_NITEMS = N_ASSIGN // MOE_ITEM_ROWS + N_EXPERTS


def _cparams(sem, vmem=VMEM_LIMIT):
    return pltpu.CompilerParams(dimension_semantics=sem, vmem_limit_bytes=vmem)


def _rms(x, w):
    return x * lax.rsqrt(jnp.mean(x * x, axis=-1, keepdims=True) + EPS) * w


def _adaln_kernel(ct_ref, w_ref, b_ref, o_ref):
    ct = ct_ref[...]
    cond = ct * jax.nn.sigmoid(ct)
    w = w_ref[...]
    for b in range(BATCH):
        o_ref[b:b + 1, :] = jnp.sum(w * cond[:, b:b + 1], axis=0, keepdims=True) + b_ref[...]


def _adaln(c_t, w, b, tn=1024):
    d, n = w.shape
    return pl.pallas_call(
        _adaln_kernel,
        grid=(n // tn,),
        in_specs=[pl.BlockSpec((d, BATCH), lambda j: (0, 0)),
                  pl.BlockSpec((d, tn), lambda j: (0, j)),
                  pl.BlockSpec((1, tn), lambda j: (0, j))],
        out_specs=pl.BlockSpec((BATCH, tn), lambda j: (0, j)),
        out_shape=jax.ShapeDtypeStruct((BATCH, n), F32),
        compiler_params=_cparams(("arbitrary",)),
        name="adaln",
    )(c_t, w, b.reshape(1, n))


IN_TM = 1024
IN_TN = 1024


def _inproj_kernel(x_ref, nw_ref, sh_ref, sc_ref, w_ref, o_ref, h_ref):
    @pl.when(pl.program_id(1) == 0)
    def _():
        for r in range(0, IN_TM, 256):
            x = x_ref[r:r + 256, :]
            h = _rms(x, nw_ref[...]) * (1.0 + sc_ref[0]) + sh_ref[0]
            h_ref[r:r + 256, :] = h.astype(BF16)

    o_ref[...] = jnp.dot(h_ref[...], w_ref[...], preferred_element_type=F32).astype(BF16)


def _inproj(x2d, nw, sh, sc, w_packed):
    per_b = SEQ // IN_TM
    return pl.pallas_call(
        _inproj_kernel,
        grid=(N_TOK // IN_TM, P_TOTAL // IN_TN),
        in_specs=[pl.BlockSpec((IN_TM, D_MODEL), lambda i, j: (i, 0)),
                  pl.BlockSpec((1, D_MODEL), lambda i, j: (0, 0)),
                  pl.BlockSpec((1, 1, D_MODEL), lambda i, j: (i // per_b, 0, 0)),
                  pl.BlockSpec((1, 1, D_MODEL), lambda i, j: (i // per_b, 0, 0)),
                  pl.BlockSpec((D_MODEL, IN_TN), lambda i, j: (0, j))],
        out_specs=pl.BlockSpec((IN_TM, IN_TN), lambda i, j: (i, j)),
        out_shape=jax.ShapeDtypeStruct((N_TOK, P_TOTAL), BF16),
        scratch_shapes=[pltpu.VMEM((IN_TM, D_MODEL), BF16)],
        compiler_params=_cparams(("arbitrary", "arbitrary")),
        name="inproj",
    )(x2d, nw, sh, sc, w_packed)


PREP_TM = 512


def _mla_prep_kernel(ql_ref, ckv_ref, kpe_ref, pos_ref, rc_ref, qnw_ref, kvnw_ref, wq_ref, wkv_ref,
                     q_ref, k_ref, v_ref):
    scale = (MLA_NOPE + MLA_ROPE) ** -0.5
    qn = _rms(ql_ref[...].astype(F32), qnw_ref[...]).astype(BF16)
    q = jnp.dot(qn, wq_ref[...], preferred_element_type=F32)
    cn = _rms(ckv_ref[...].astype(F32), kvnw_ref[...]).astype(BF16)
    kv = jnp.dot(cn, wkv_ref[...], preferred_element_type=F32)

    ang = pos_ref[...].astype(F32) * rc_ref[0:1, :]
    cos_t = jnp.cos(ang)
    sin_t = jnp.sin(ang)
    sin_a = sin_t * rc_ref[1:2, :]
    sin_b = sin_t * rc_ref[2:3, :]

    def rope(t):
        return (t * cos_t + pltpu.roll(t, LANES - MLA_ROPE // 2, 1) * sin_a
                + pltpu.roll(t, MLA_ROPE // 2, 1) * sin_b)

    kpe = rope(kpe_ref[...].astype(F32)).astype(BF16)
    for h in range(MLA_HEADS):
        c0 = h * MLA_HD
        q_ref[0, h, :, 0:LANES] = (q[:, c0:c0 + LANES] * scale).astype(BF16)
        q_ref[0, h, :, LANES:MLA_HD] = (rope(q[:, c0 + LANES:c0 + MLA_HD]) * scale).astype(BF16)
        k_ref[0, h, :, 0:LANES] = kv[:, c0:c0 + LANES].astype(BF16)
        k_ref[0, h, :, LANES:MLA_HD] = kpe
        v_ref[0, h, :, :] = kv[:, c0 + LANES:c0 + MLA_HD].astype(BF16)


def _mla_prep(proj, pos_col, rope_consts, qnw, kvnw, wq, wkv):
    tm = PREP_TM
    per_b = SEQ // tm
    row = lambda i: (i, 0)
    const = lambda i: (0, 0)
    out_map = lambda i: (i // per_b, 0, i % per_b, 0)
    return pl.pallas_call(
        _mla_prep_kernel,
        grid=(N_TOK // tm,),
        in_specs=[pl.BlockSpec((tm, MLA_Q_RANK), lambda i: (i, P_QLAT // MLA_Q_RANK)),
                  pl.BlockSpec((tm, MLA_KV_RANK), lambda i: (i, P_CKV // MLA_KV_RANK)),
                  pl.BlockSpec((tm, LANES), lambda i: (i, P_KPE // LANES)),
                  pl.BlockSpec((tm, 1), row),
                  pl.BlockSpec((8, LANES), const),
                  pl.BlockSpec((1, MLA_Q_RANK), const),
                  pl.BlockSpec((1, MLA_KV_RANK), const),
                  pl.BlockSpec((MLA_Q_RANK, MLA_HEADS * MLA_HD), const),
                  pl.BlockSpec((MLA_KV_RANK, MLA_HEADS * MLA_HD), const)],
        out_specs=[pl.BlockSpec((1, MLA_HEADS, tm, MLA_HD), out_map),
                   pl.BlockSpec((1, MLA_HEADS, tm, MLA_HD), out_map),
                   pl.BlockSpec((1, MLA_HEADS, tm, MLA_V), out_map)],
        out_shape=[jax.ShapeDtypeStruct((BATCH, MLA_HEADS, SEQ, MLA_HD), BF16),
                   jax.ShapeDtypeStruct((BATCH, MLA_HEADS, SEQ, MLA_HD), BF16),
                   jax.ShapeDtypeStruct((BATCH, MLA_HEADS, SEQ, MLA_V), BF16)],
        compiler_params=_cparams(("arbitrary",)),
        name="mla_prep",
    )(proj, proj, proj, pos_col, rope_consts, qnw, kvnw, wq, wkv)


ATT_T = 512


def _attn_kernel(q_ref, k_ref, v_ref, o_ref, m_ref, l_ref, acc_ref):
    qi = pl.program_id(2)
    q = q_ref[0, 0]
    m_ref[...] = jnp.full(m_ref.shape, -jnp.inf, F32)
    l_ref[...] = jnp.zeros(l_ref.shape, F32)
    acc_ref[...] = jnp.zeros(acc_ref.shape, F32)

    def step(j, masked):
        r0 = pl.multiple_of(j * ATT_T, ATT_T)
        k = k_ref[0, 0, pl.ds(r0, ATT_T), :]
        v = v_ref[0, 0, pl.ds(r0, ATT_T), :]
        s = lax.dot_general(q, k, (((1,), (1,)), ((), ())), preferred_element_type=F32)
        if masked:
            ri = lax.broadcasted_iota(jnp.int32, s.shape, 0)
            ci = lax.broadcasted_iota(jnp.int32, s.shape, 1)
            s = jnp.where(ci <= ri, s, -jnp.inf)
        m_old = m_ref[...]
        m_new = jnp.maximum(m_old, jnp.max(s, axis=-1, keepdims=True))
        p = jnp.exp(s - m_new)
        alpha = jnp.exp(m_old - m_new)
        l_ref[...] = alpha * l_ref[...] + jnp.sum(p, axis=-1, keepdims=True)
        acc_ref[...] = alpha * acc_ref[...] + jnp.dot(p.astype(BF16), v, preferred_element_type=F32)
        m_ref[...] = m_new

    def body(j, carry):
        step(j, False)
        return carry

    lax.fori_loop(0, qi, body, 0)
    step(qi, True)
    o_ref[...] = (acc_ref[...] / l_ref[...]).astype(BF16)


def _attention(q, k, v):
    t = ATT_T
    nq = SEQ // t
    return pl.pallas_call(
        _attn_kernel,
        grid=(BATCH, MLA_HEADS, nq),
        in_specs=[pl.BlockSpec((1, 1, t, MLA_HD), lambda b, h, i: (b, h, i, 0)),
                  pl.BlockSpec((1, 1, SEQ, MLA_HD), lambda b, h, i: (b, h, 0, 0)),
                  pl.BlockSpec((1, 1, SEQ, MLA_V), lambda b, h, i: (b, h, 0, 0))],
        out_specs=pl.BlockSpec((t, MLA_V), lambda b, h, i: (b * nq + i, h)),
        out_shape=jax.ShapeDtypeStruct((N_TOK, MLA_HEADS * MLA_V), BF16),
        scratch_shapes=[pltpu.VMEM((t, 1), F32), pltpu.VMEM((t, 1), F32), pltpu.VMEM((t, MLA_V), F32)],
        compiler_params=_cparams(("arbitrary", "arbitrary", "arbitrary")),
        name="mla_attn",
    )(q, k, v)


GLA_TG = 512
GLA_DIAG = 8


def _gla_kernel(q_ref, k_ref, v_ref, glr_ref, gout_ref, wgk_ref, bgk_ref, nw_ref, o_ref, st_ref, g_ref):
    C = GLA_CHUNK

    @pl.when(pl.program_id(2) == 0)
    def _():
        st_ref[...] = jnp.zeros(st_ref.shape, F32)

    z = jnp.dot(glr_ref[...].astype(F32), wgk_ref[...], preferred_element_type=F32,
                precision=lax.Precision.HIGHEST) + bgk_ref[...]
    g_ref[...] = jax.nn.log_sigmoid(z) / GLA_GATE_NORM

    row = lax.broadcasted_iota(jnp.int32, (C, GLA_DK), 0)
    ii = lax.broadcasted_iota(jnp.int32, (C, C), 0)
    jj = lax.broadcasted_iota(jnp.int32, (C, C), 1)
    nt = (((1,), (1,)), ((), ()))

    def chunk(ci, carry):
        r0 = pl.multiple_of(ci * C, C)
        b = g_ref[pl.ds(r0, C), :]
        s = 1
        while s < C:
            b = b + jnp.where(row >= s, pltpu.roll(b, s, 0), 0.0)
            s *= 2
        q = q_ref[pl.ds(r0, C), :].astype(F32) * (GLA_DK ** -0.5)
        k = k_ref[pl.ds(r0, C), :].astype(F32)
        v = v_ref[pl.ds(r0, C), :]
        st = st_ref[...]

        o = lax.dot_general((q * jnp.exp(b)).astype(BF16), st.astype(BF16), nt,
                            preferred_element_type=F32)

        a = jnp.zeros((C, C), F32)
        lvl = C // 2
        while lvl >= GLA_DIAG:
            pieces = [jnp.broadcast_to(b[t + lvl:t + lvl + 1, :], (2 * lvl, GLA_DK))
                      for t in range(0, C, 2 * lvl)]
            m = pieces[0] if len(pieces) == 1 else jnp.concatenate(pieces, axis=0)
            odd = (row // lvl) % 2 == 1
            ql = jnp.where(odd, q * jnp.exp(jnp.minimum(b - m, 0.0)), 0.0).astype(BF16)
            kl = jnp.where(odd, 0.0, k * jnp.exp(jnp.minimum(m - b, 0.0))).astype(BF16)
            al = lax.dot_general(ql, kl, nt, preferred_element_type=F32)
            a = a + jnp.where(ii // (2 * lvl) == jj // (2 * lvl), al, 0.0)
            lvl //= 2
        for d in range(GLA_DIAG):
            if d == 0:
                t = q * k
            else:
                t = q * pltpu.roll(k, d, 0) * jnp.exp(jnp.minimum(b - pltpu.roll(b, d, 0), 0.0))
            col = jnp.sum(t, axis=-1, keepdims=True)
            a = a + jnp.where((ii - jj == d) & (ii % GLA_DIAG >= d), col, 0.0)
        o = o + jnp.dot(a.astype(BF16), v, preferred_element_type=F32)

        b_last = b[C - 1:C, :]
        kdec = (k * jnp.exp(b_last - b)).astype(BF16)
        st_ref[...] = st * jnp.exp(b_last) + lax.dot_general(
            v, kdec, (((0,), (0,)), ((), ())), preferred_element_type=F32)

        gate = gout_ref[pl.ds(r0, C), :].astype(F32)
        y = _rms(o, nw_ref[...]) * (gate * jax.nn.sigmoid(gate))
        o_ref[pl.ds(r0, C), :] = y.astype(BF16)
        return carry

    lax.fori_loop(0, GLA_TG // C, chunk, 0)


def _gla(proj, wgk, bgk, nw):
    tg = GLA_TG
    per_b = SEQ // tg
    rb = lambda b, h, t: b * per_b + t
    return pl.pallas_call(
        _gla_kernel,
        grid=(BATCH, GLA_HEADS, per_b),
        in_specs=[pl.BlockSpec((tg, GLA_DK), lambda b, h, t: (rb(b, h, t), P_GQ // GLA_DK + h)),
                  pl.BlockSpec((tg, GLA_DK), lambda b, h, t: (rb(b, h, t), P_GK // GLA_DK + h)),
                  pl.BlockSpec((tg, GLA_DV), lambda b, h, t: (rb(b, h, t), P_GV // GLA_DV + h)),
                  pl.BlockSpec((tg, LANES), lambda b, h, t: (rb(b, h, t), P_GLR // LANES)),
                  pl.BlockSpec((tg, GLA_DV), lambda b, h, t: (rb(b, h, t), P_GOUT // GLA_DV + h)),
                  pl.BlockSpec((LANES, GLA_DK), lambda b, h, t: (0, h)),
                  pl.BlockSpec((1, GLA_DK), lambda b, h, t: (0, h)),
                  pl.BlockSpec((1, GLA_DV), lambda b, h, t: (0, 0))],
        out_specs=pl.BlockSpec((tg, GLA_DV), lambda b, h, t: (rb(b, h, t), h)),
        out_shape=jax.ShapeDtypeStruct((N_TOK, GLA_HEADS * GLA_DV), BF16),
        scratch_shapes=[pltpu.VMEM((GLA_DV, GLA_DK), F32), pltpu.VMEM((tg, GLA_DK), F32)],
        compiler_params=_cparams(("arbitrary", "arbitrary", "arbitrary")),
        name="gla",
    )(proj, proj, proj, proj, proj, wgk, bgk, nw)


MRG_TM = 256


def _merge_kernel(x_ref, oa_ref, ob_ref, ga_ref, gb_ref, gm_ref, shf_ref, scf_ref, nfw_ref,
                  woa_ref, wob_ref, wout_ref, wrh_ref, wrl_ref, br_ref,
                  x1_ref, h2_ref, idx_ref, tw_ref):
    ya = jnp.dot(oa_ref[...], woa_ref[...], preferred_element_type=F32)
    yb = jnp.dot(ob_ref[...], wob_ref[...], preferred_element_type=F32)
    merged = (jax.nn.sigmoid(ga_ref[...].astype(F32)) * ya
              + jax.nn.sigmoid(gb_ref[...].astype(F32)) * yb).astype(BF16)
    x1 = x_ref[...] + gm_ref[0] * jnp.dot(merged, wout_ref[...], preferred_element_type=F32)
    x1_ref[...] = x1
    h2 = _rms(x1, nfw_ref[...]) * (1.0 + scf_ref[0]) + shf_ref[0]
    h2_ref[...] = h2

    hi = h2.astype(BF16)
    lo = (h2 - hi.astype(F32)).astype(BF16)
    logits = (jnp.dot(hi, wrh_ref[...], preferred_element_type=F32)
              + jnp.dot(hi, wrl_ref[...], preferred_element_type=F32)
              + jnp.dot(lo, wrh_ref[...], preferred_element_type=F32)) + br_ref[...]
    lane = lax.broadcasted_iota(jnp.int32, logits.shape, 1)
    vals = jnp.where(lane < N_EXPERTS, logits, -jnp.inf)
    idx_out = jnp.zeros(logits.shape, jnp.int32)
    w_out = jnp.zeros(logits.shape, F32)
    top = None
    denom = None
    for kk in range(TOP_K):
        m = jnp.max(vals, axis=-1, keepdims=True)
        sel = jnp.min(jnp.where(vals == m, lane, LANES), axis=-1, keepdims=True)
        if kk == 0:
            top = m
        e = jnp.exp(m - top)
        denom = e if kk == 0 else denom + e
        idx_out = jnp.where(lane == kk, sel, idx_out)
        w_out = jnp.where(lane == kk, e, w_out)
        vals = jnp.where(lane == sel, -jnp.inf, vals)
    idx_ref[...] = idx_out
    tw_ref[...] = w_out / denom


def _merge(x2d, o_mla, o_gla, proj, gm, shf, scf, nfw, woa, wob, wout, wrh, wrl, br):
    tm = MRG_TM
    per_b = SEQ // tm
    row = lambda i: (i, 0)
    const = lambda i: (0, 0)
    bvec = lambda i: (i // per_b, 0, 0)
    single = dict(pipeline_mode=pl.Buffered(1))
    return pl.pallas_call(
        _merge_kernel,
        grid=(N_TOK // tm,),
        in_specs=[pl.BlockSpec((tm, D_MODEL), row),
                  pl.BlockSpec((tm, MLA_HEADS * MLA_V), row),
                  pl.BlockSpec((tm, GLA_HEADS * GLA_DV), row),
                  pl.BlockSpec((tm, D_MODEL), lambda i: (i, P_GATEA // D_MODEL)),
                  pl.BlockSpec((tm, D_MODEL), lambda i: (i, P_GATEB // D_MODEL)),
                  pl.BlockSpec((1, 1, D_MODEL), bvec),
                  pl.BlockSpec((1, 1, D_MODEL), bvec),
                  pl.BlockSpec((1, 1, D_MODEL), bvec),
                  pl.BlockSpec((1, D_MODEL), const),
                  pl.BlockSpec((MLA_HEADS * MLA_V, D_MODEL), const, **single),
                  pl.BlockSpec((GLA_HEADS * GLA_DV, D_MODEL), const, **single),
                  pl.BlockSpec((D_MODEL, D_MODEL), const, **single),
                  pl.BlockSpec((D_MODEL, LANES), const, **single),
                  pl.BlockSpec((D_MODEL, LANES), const, **single),
                  pl.BlockSpec((1, LANES), const)],
        out_specs=[pl.BlockSpec((tm, D_MODEL), row),
                   pl.BlockSpec((tm, D_MODEL), row),
                   pl.BlockSpec((tm, LANES), row),
                   pl.BlockSpec((tm, LANES), row)],
        out_shape=[jax.ShapeDtypeStruct((N_TOK, D_MODEL), F32),
                   jax.ShapeDtypeStruct((N_TOK, D_MODEL), F32),
                   jax.ShapeDtypeStruct((N_TOK, LANES), jnp.int32),
                   jax.ShapeDtypeStruct((N_TOK, LANES), F32)],
        compiler_params=_cparams(("arbitrary",)),
        name="merge_router",
    )(x2d, o_mla, o_gla, proj, proj, gm, shf, scf, nfw, woa, wob, wout, wrh, wrl, br)


def _moe_kernel(it_e_ref, it_start_ref, it_nblk_ref, slot_ref,
                h2_hbm, w1_ref, b1_ref, w2_ref, b2_ref, sel_ref, y4_hbm,
                stage_ref, xb_ref, acc_ref, gsem, ssem):
    it = pl.program_id(0)
    c = pl.program_id(1)
    nblk = it_nblk_ref[it]
    base = it_start_ref[it]

    @pl.when(jnp.logical_and(it == 0, c == 0))
    def _():
        xb_ref[...] = jnp.zeros(xb_ref.shape, BF16)

    def gather_copy(tok, r):
        return pltpu.make_async_copy(h2_hbm.at[pl.ds(tok, 1), :], stage_ref.at[pl.ds(r, 1), :], gsem)

    def scatter_copy(r, dst):
        return pltpu.make_async_copy(acc_ref.at[pl.ds(r, 1), :], y4_hbm.at[pl.ds(dst, 1), :], ssem)

    @pl.when(jnp.logical_and(c == 0, nblk > 0))
    def _():
        for s in range(MOE_ITEM_BLKS):
            @pl.when(s < nblk)
            def _():
                def issue(r, carry):
                    a = slot_ref[base + s * MOE_BLK + r]
                    gather_copy(jnp.maximum(a, 0) // TOP_K, r).start()
                    return carry
                lax.fori_loop(0, MOE_BLK, issue, 0)

                def drain(r, carry):
                    gather_copy(0, r).wait()
                    return carry
                lax.fori_loop(0, MOE_BLK, drain, 0)
                xb_ref[s * MOE_BLK:(s + 1) * MOE_BLK, :] = stage_ref[...].astype(BF16)
        acc_ref[...] = jnp.broadcast_to(b2_ref[0], acc_ref.shape)

    @pl.when(nblk > 0)
    def _():
        w1 = w1_ref[0].astype(BF16)
        w2 = w2_ref[0].astype(BF16)
        hm = jnp.dot(xb_ref[...], w1, preferred_element_type=F32) + b1_ref[0]
        lin = pltpu.roll(hm, 2 * MOE_TH - 1, 1)
        glu = jnp.minimum(hm, SWIGLU_LIMIT)
        lin = jnp.clip(lin, -SWIGLU_LIMIT, SWIGLU_LIMIT)
        act = glu * jax.nn.sigmoid(SWIGLU_ALPHA * glu) * (lin + 1.0)
        lane = lax.broadcasted_iota(jnp.int32, act.shape, 1)
        act = jnp.where(lane % 2 == 0, act, 0.0).astype(BF16)
        act = jnp.dot(act, sel_ref[...], preferred_element_type=F32).astype(BF16)
        acc_ref[...] += jnp.dot(act, w2, preferred_element_type=F32)

    @pl.when(jnp.logical_and(c == MOE_NCHUNK - 1, nblk > 0))
    def _():
        nrows = nblk * MOE_BLK

        def issue(r, carry):
            a = slot_ref[base + r]

            @pl.when(a >= 0)
            def _():
                scatter_copy(r, a).start()
            return carry
        lax.fori_loop(0, nrows, issue, 0)

        def drain(r, carry):
            @pl.when(slot_ref[base + r] >= 0)
            def _():
                scatter_copy(r, 0).wait()
            return carry
        lax.fori_loop(0, nrows, drain, 0)


def _moe(it_e, it_start, it_nblk, slot_a, h2, w1, b1, w2, b2):
    def w1_map(it, c, e_ref, s_ref, n_ref, a_ref):
        return (e_ref[it], 0, jnp.where(n_ref[it] > 0, c, MOE_NCHUNK - 1))

    def w2_map(it, c, e_ref, s_ref, n_ref, a_ref):
        return (e_ref[it], jnp.where(n_ref[it] > 0, c, MOE_NCHUNK - 1), 0)

    def b1_map(it, c, e_ref, s_ref, n_ref, a_ref):
        return (e_ref[it], 0, jnp.where(n_ref[it] > 0, c, MOE_NCHUNK - 1))

    def b2_map(it, c, e_ref, s_ref, n_ref, a_ref):
        return (e_ref[it], 0, 0)

    sel = (jnp.arange(2 * MOE_TH, dtype=jnp.int32)[:, None]
           == 2 * jnp.arange(MOE_TH, dtype=jnp.int32)[None, :]).astype(BF16)

    grid_spec = pltpu.PrefetchScalarGridSpec(
        num_scalar_prefetch=4,
        grid=(MOE_NITEMS, MOE_NCHUNK),
        in_specs=[pl.BlockSpec(memory_space=pl.ANY),
                  pl.BlockSpec((1, D_MODEL, 2 * MOE_TH), w1_map),
                  pl.BlockSpec((1, 1, 2 * MOE_TH), b1_map),
                  pl.BlockSpec((1, MOE_TH, D_MODEL), w2_map),
                  pl.BlockSpec((1, 1, D_MODEL), b2_map),
                  pl.BlockSpec((2 * MOE_TH, MOE_TH), lambda it, c, *_: (0, 0))],
        out_specs=pl.BlockSpec(memory_space=pl.ANY),
        scratch_shapes=[pltpu.VMEM((MOE_BLK, D_MODEL), F32),
                        pltpu.VMEM((MOE_ITEM_ROWS, D_MODEL), BF16),
                        pltpu.VMEM((MOE_ITEM_ROWS, D_MODEL), F32),
                        pltpu.SemaphoreType.DMA(()),
                        pltpu.SemaphoreType.DMA(())],
    )
    return pl.pallas_call(
        _moe_kernel,
        grid_spec=grid_spec,
        out_shape=jax.ShapeDtypeStruct((N_ASSIGN, D_MODEL), F32),
        compiler_params=_cparams(("arbitrary", "arbitrary")),
        name="moe_experts",
    )(it_e, it_start, it_nblk, slot_a, h2, w1,
      b1.reshape(N_EXPERTS, 1, 2 * D_EXPERT), w2, b2.reshape(N_EXPERTS, 1, D_MODEL), sel)


def _route_tables(top_idx):
    e_flat = top_idx.reshape(N_ASSIGN)
    onehot = (e_flat[:, None] == jnp.arange(N_EXPERTS, dtype=jnp.int32)[None, :]).astype(jnp.int32)
    csum = jnp.cumsum(onehot, axis=0)
    rank = jnp.sum(csum * onehot, axis=-1) - 1
    counts = csum[-1]
    nblk = (counts + MOE_BLK - 1) // MOE_BLK
    blk_end = jnp.cumsum(nblk)
    blk_start = blk_end - nblk
    dest = blk_start[e_flat] * MOE_BLK + rank
    slot_a = jnp.full((MOE_SLOTS,), -1, jnp.int32).at[dest].set(jnp.arange(N_ASSIGN, dtype=jnp.int32))

    nit = (nblk + MOE_ITEM_BLKS - 1) // MOE_ITEM_BLKS
    it_end = jnp.cumsum(nit)
    it_begin = it_end - nit
    n_used = it_end[-1]
    ids = jnp.arange(MOE_NITEMS, dtype=jnp.int32)
    valid = ids < n_used
    last = jnp.minimum(ids, n_used - 1)
    e_of = jnp.minimum(jnp.sum((it_end[None, :] <= last[:, None]).astype(jnp.int32), axis=1), N_EXPERTS - 1)
    local = ids - it_begin[e_of]
    it_start = jnp.where(valid, (blk_start[e_of] + local * MOE_ITEM_BLKS) * MOE_BLK, 0).astype(jnp.int32)
    it_nblk = jnp.where(valid, jnp.clip(nblk[e_of] - local * MOE_ITEM_BLKS, 0, MOE_ITEM_BLKS), 0)
    return e_of, it_start, it_nblk.astype(jnp.int32), slot_a


FIN_TM = 256


def _final_kernel(x1_ref, y4_ref, tw_ref, gf_ref, sho_ref, sco_ref, nw_ref, o_ref):
    tw = tw_ref[...]
    moe = tw[:, 0:1] * y4_ref[:, 0:D_MODEL]
    for kk in range(1, TOP_K):
        moe = moe + tw[:, kk:kk + 1] * y4_ref[:, kk * D_MODEL:(kk + 1) * D_MODEL]
    x2 = x1_ref[...] + gf_ref[0] * moe
    o_ref[...] = _rms(x2, nw_ref[...]) * (1.0 + sco_ref[0]) + sho_ref[0]


def _final(x1, y4, tw, gf, sho, sco, nw):
    tm = FIN_TM
    per_b = SEQ // tm
    row = lambda i: (i, 0)
    bvec = lambda i: (i // per_b, 0, 0)
    y4v = y4.reshape(N_TOK, TOP_K * D_MODEL)
    return pl.pallas_call(
        _final_kernel,
        grid=(N_TOK // tm,),
        in_specs=[pl.BlockSpec((tm, D_MODEL), row),
                  pl.BlockSpec((tm, TOP_K * D_MODEL), row),
                  pl.BlockSpec((tm, LANES), row),
                  pl.BlockSpec((1, 1, D_MODEL), bvec),
                  pl.BlockSpec((1, 1, D_MODEL), bvec),
                  pl.BlockSpec((1, 1, D_MODEL), bvec),
                  pl.BlockSpec((1, D_MODEL), lambda i: (0, 0))],
        out_specs=pl.BlockSpec((tm, D_MODEL), row),
        out_shape=jax.ShapeDtypeStruct((N_TOK, D_MODEL), F32),
        compiler_params=_cparams(("arbitrary",)),
        name="combine_final",
    )(x1, y4v, tw, gf, sho, sco, nw)


def _deinterleave(n):
    return np.concatenate([np.arange(0, n, 2), np.arange(1, n, 2)])


def _pack_w_in(w_in):
    o = np.cumsum([0, MLA_Q_RANK, MLA_KV_RANK + MLA_ROPE, GLA_HEADS * GLA_DK, GLA_HEADS * GLA_DK,
                   GLA_HEADS * GLA_DV, GLA_GATE_RANK, GLA_HEADS * GLA_DV, D_MODEL, D_MODEL])
    seg = lambda i: w_in[:, o[i]:o[i + 1]]
    kv = seg(1)
    zeros = lambda n: jnp.zeros((D_MODEL, n), w_in.dtype)
    kpe = kv[:, MLA_KV_RANK:][:, _deinterleave(MLA_ROPE)]
    parts = [seg(0), kv[:, :MLA_KV_RANK], kpe, zeros(LANES - MLA_ROPE),
             seg(5), zeros(LANES - GLA_GATE_RANK), seg(2), seg(3), seg(4), seg(6), seg(7), seg(8)]
    return jnp.concatenate(parts, axis=1).astype(BF16)


def _pack_w_q_b(w):
    w = w.reshape(MLA_Q_RANK, MLA_HEADS, MLA_NOPE + MLA_ROPE)
    nope = w[:, :, :MLA_NOPE]
    pe = w[:, :, MLA_NOPE:][:, :, _deinterleave(MLA_ROPE)]
    pad = jnp.zeros((MLA_Q_RANK, MLA_HEADS, MLA_HD - MLA_NOPE - MLA_ROPE), w.dtype)
    return jnp.concatenate([nope, pe, pad], axis=-1).reshape(MLA_Q_RANK, MLA_HEADS * MLA_HD).astype(BF16)


def _rope_consts():
    inv_freq = ROPE_THETA ** (-(jnp.arange(0, MLA_ROPE, 2, dtype=F32) / MLA_ROPE))
    half = MLA_ROPE // 2
    z = jnp.zeros((LANES - MLA_ROPE,), F32)
    rows = [jnp.concatenate([inv_freq, inv_freq, z]),
            jnp.concatenate([-jnp.ones((half,), F32), jnp.zeros((half,), F32), z]),
            jnp.concatenate([jnp.zeros((half,), F32), jnp.ones((half,), F32), z])]
    return jnp.concatenate([jnp.stack(rows), jnp.zeros((5, LANES), F32)], axis=0)


def kernel(x, c, positions, w_ada, b_ada, norm_mix_w, w_in, mla_q_norm_w, mla_w_q_b, mla_kv_norm_w,
           mla_w_kv_b, gla_w_gk_up, gla_b_gk_up, gla_norm_w, w_o_mla, w_o_gla, w_out, norm_ffn_w,
           w_router, b_router, w1, b1, w2, b2, w_ada_final, b_ada_final, norm_final_w):
    assert w_ada.shape[0] == 1, "single-layer stack"
    x2d = x.reshape(N_TOK, D_MODEL)
    c_t = c.T

    mod = _adaln(c_t, w_ada[0], b_ada[0])
    fmod = _adaln(c_t, w_ada_final, b_ada_final)
    sh_m, sc_m, g_m, sh_f, sc_f, g_f = [m.reshape(BATCH, 1, D_MODEL) for m in jnp.split(mod, N_MOD, axis=-1)]
    sh_o, sc_o = [m.reshape(BATCH, 1, D_MODEL) for m in jnp.split(fmod, 2, axis=-1)]

    proj = _inproj(x2d, norm_mix_w[0].reshape(1, D_MODEL), sh_m, sc_m, _pack_w_in(w_in[0]))

    q, k, v = _mla_prep(proj, positions.reshape(N_TOK, 1), _rope_consts(),
                        mla_q_norm_w[0].reshape(1, MLA_Q_RANK), mla_kv_norm_w[0].reshape(1, MLA_KV_RANK),
                        _pack_w_q_b(mla_w_q_b[0]), mla_w_kv_b[0].astype(BF16))
    o_mla = _attention(q, k, v)

    wgk = jnp.concatenate([gla_w_gk_up[0], jnp.zeros((LANES - GLA_GATE_RANK, GLA_HEADS * GLA_DK), F32)], axis=0)
    o_gla = _gla(proj, wgk, gla_b_gk_up[0].reshape(1, GLA_HEADS * GLA_DK), gla_norm_w[0].reshape(1, GLA_DV))

    wr = jnp.concatenate([w_router[0], jnp.zeros((D_MODEL, LANES - N_EXPERTS), F32)], axis=1)
    wr_hi = wr.astype(BF16)
    wr_lo = (wr - wr_hi.astype(F32)).astype(BF16)
    br = jnp.concatenate([b_router[0], jnp.zeros((LANES - N_EXPERTS,), F32)]).reshape(1, LANES)
    x1, h2, top_idx, top_w = _merge(
        x2d, o_mla, o_gla, proj, g_m, sh_f, sc_f, norm_ffn_w[0].reshape(1, D_MODEL),
        w_o_mla[0].astype(BF16), w_o_gla[0].astype(BF16), w_out[0].astype(BF16), wr_hi, wr_lo, br)

    it_e, it_start, it_nblk, slot_a = _route_tables(top_idx[:, :TOP_K])
    y4 = _moe(it_e, it_start, it_nblk, slot_a, h2, w1[0], b1[0], w2[0], b2[0])

    out = _final(x1, y4, top_w, g_f, sh_o, sc_o, norm_final_w.reshape(1, D_MODEL))
    return out.reshape(BATCH, SEQ, D_MODEL)
```

```python
import functools

import jax
import jax.numpy as jnp
import numpy as np
from jax import lax
from jax.experimental import pallas as pl
from jax.experimental.pallas import tpu as pltpu

F32 = jnp.float32
BF16 = jnp.bfloat16

D_MODEL = 2048
BATCH = 2
SEQ = 4096
N_TOK = BATCH * SEQ

MLA_HEADS = 8
MLA_Q_RANK = 512
MLA_KV_RANK = 256
MLA_NOPE = 128
MLA_ROPE = 64
MLA_V = 128
ROPE_THETA = 10000.0
MLA_HD = 256

GLA_HEADS = 4
GLA_DK = 128
GLA_DV = 256
GLA_GATE_RANK = 16
GLA_GATE_NORM = 16.0
GLA_CHUNK = 64

N_EXPERTS = 32
TOP_K = 4
D_EXPERT = D_MODEL
SWIGLU_LIMIT = 7.0
SWIGLU_ALPHA = 1.702
EPS = 1e-6
N_MOD = 6

LANES = 128
VMEM_LIMIT = 56 * 1024 * 1024

P_QLAT = 0
P_CKV = 512
P_KPE = 768
P_GLR = 896
P_GQ = 1024
P_GK = 1536
P_GV = 2048
P_GOUT = 3072
P_GATEA = 4096
P_GATEB = 6144
P_TOTAL = 8192

MOE_BLK = 256
MOE_ITEM_BLKS = 4
MOE_ITEM_ROWS = MOE_BLK * MOE_ITEM_BLKS
MOE_TH = 256
MOE_NCHUNK = D_EXPERT // MOE_TH
N_ASSIGN = N_TOK * TOP_K
MOE_NBLK_MAX = N_ASSIGN // MOE_BLK + N_EXPERTS
MOE_SLOTS = MOE_NBLK_MAX * MOE_BLK
MOE_NITEMS = N_ASSIGN // MOE_ITEM_ROWS + N_EXPERTS


def _cparams(sem, vmem=VMEM_LIMIT):
    return pltpu.CompilerParams(dimension_semantics=sem, vmem_limit_bytes=vmem)


def _rms(x, w):
    return x * lax.rsqrt(jnp.mean(x * x, axis=-1, keepdims=True) + EPS) * w


def _adaln_kernel(ct_ref, w_ref, b_ref, o_ref):
    ct = ct_ref[...]
    cond = ct * jax.nn.sigmoid(ct)
    w = w_ref[...]
    for b in range(BATCH):
        o_ref[b:b + 1, :] = jnp.sum(w * cond[:, b:b + 1], axis=0, keepdims=True) + b_ref[...]


def _adaln(c_t, w, b, tn=1024):
    d, n = w.shape
    return pl.pallas_call(
        _adaln_kernel,
        grid=(n // tn,),
        in_specs=[pl.BlockSpec((d, BATCH), lambda j: (0, 0)),
                  pl.BlockSpec((d, tn), lambda j: (0, j)),
                  pl.BlockSpec((1, tn), lambda j: (0, j))],
        out_specs=pl.BlockSpec((BATCH, tn), lambda j: (0, j)),
        out_shape=jax.ShapeDtypeStruct((BATCH, n), F32),
        compiler_params=_cparams(("arbitrary",)),
        name="adaln",
    )(c_t, w, b.reshape(1, n))


IN_TM = 1024
IN_TN = 1024


def _inproj_kernel(x_ref, nw_ref, sh_ref, sc_ref, w_ref, o_ref, h_ref):
    @pl.when(pl.program_id(1) == 0)
    def _():
        for r in range(0, IN_TM, 256):
            x = x_ref[r:r + 256, :]
            h = _rms(x, nw_ref[...]) * (1.0 + sc_ref[0]) + sh_ref[0]
            h_ref[r:r + 256, :] = h.astype(BF16)

    o_ref[...] = jnp.dot(h_ref[...], w_ref[...], preferred_element_type=F32).astype(BF16)


def _inproj(x2d, nw, sh, sc, w_packed):
    per_b = SEQ // IN_TM
    return pl.pallas_call(
        _inproj_kernel,
        grid=(N_TOK // IN_TM, P_TOTAL // IN_TN),
        in_specs=[pl.BlockSpec((IN_TM, D_MODEL), lambda i, j: (i, 0)),
                  pl.BlockSpec((1, D_MODEL), lambda i, j: (0, 0)),
                  pl.BlockSpec((1, 1, D_MODEL), lambda i, j: (i // per_b, 0, 0)),
                  pl.BlockSpec((1, 1, D_MODEL), lambda i, j: (i // per_b, 0, 0)),
                  pl.BlockSpec((D_MODEL, IN_TN), lambda i, j: (0, j))],
        out_specs=pl.BlockSpec((IN_TM, IN_TN), lambda i, j: (i, j)),
        out_shape=jax.ShapeDtypeStruct((N_TOK, P_TOTAL), BF16),
        scratch_shapes=[pltpu.VMEM((IN_TM, D_MODEL), BF16)],
        compiler_params=_cparams(("arbitrary", "arbitrary")),
        name="inproj",
    )(x2d, nw, sh, sc, w_packed)


PREP_TM = 512


def _mla_prep_kernel(ql_ref, ckv_ref, kpe_ref, pos_ref, rc_ref, qnw_ref, kvnw_ref, wq_ref, wkv_ref,
                     q_ref, k_ref, v_ref):
    scale = (MLA_NOPE + MLA_ROPE) ** -0.5
    qn = _rms(ql_ref[...].astype(F32), qnw_ref[...]).astype(BF16)
    q = jnp.dot(qn, wq_ref[...], preferred_element_type=F32)
    cn = _rms(ckv_ref[...].astype(F32), kvnw_ref[...]).astype(BF16)
    kv = jnp.dot(cn, wkv_ref[...], preferred_element_type=F32)

    ang = pos_ref[...].astype(F32) * rc_ref[0:1, :]
    cos_t = jnp.cos(ang)
    sin_t = jnp.sin(ang)
    sin_a = sin_t * rc_ref[1:2, :]
    sin_b = sin_t * rc_ref[2:3, :]

    def rope(t):
        return (t * cos_t + pltpu.roll(t, LANES - MLA_ROPE // 2, 1) * sin_a
                + pltpu.roll(t, MLA_ROPE // 2, 1) * sin_b)

    kpe = rope(kpe_ref[...].astype(F32)).astype(BF16)
    for h in range(MLA_HEADS):
        c0 = h * MLA_HD
        q_ref[0, h, :, 0:LANES] = (q[:, c0:c0 + LANES] * scale).astype(BF16)
        q_ref[0, h, :, LANES:MLA_HD] = (rope(q[:, c0 + LANES:c0 + MLA_HD]) * scale).astype(BF16)
        k_ref[0, h, :, 0:LANES] = kv[:, c0:c0 + LANES].astype(BF16)
        k_ref[0, h, :, LANES:MLA_HD] = kpe
        v_ref[0, h, :, :] = kv[:, c0 + LANES:c0 + MLA_HD].astype(BF16)


def _mla_prep(proj, pos_col, rope_consts, qnw, kvnw, wq, wkv):
    tm = PREP_TM
    per_b = SEQ // tm
    row = lambda i: (i, 0)
    const = lambda i: (0, 0)
    out_map = lambda i: (i // per_b, 0, i % per_b, 0)
    return pl.pallas_call(
        _mla_prep_kernel,
        grid=(N_TOK // tm,),
        in_specs=[pl.BlockSpec((tm, MLA_Q_RANK), lambda i: (i, P_QLAT // MLA_Q_RANK)),
                  pl.BlockSpec((tm, MLA_KV_RANK), lambda i: (i, P_CKV // MLA_KV_RANK)),
                  pl.BlockSpec((tm, LANES), lambda i: (i, P_KPE // LANES)),
                  pl.BlockSpec((tm, 1), row),
                  pl.BlockSpec((8, LANES), const),
                  pl.BlockSpec((1, MLA_Q_RANK), const),
                  pl.BlockSpec((1, MLA_KV_RANK), const),
                  pl.BlockSpec((MLA_Q_RANK, MLA_HEADS * MLA_HD), const),
                  pl.BlockSpec((MLA_KV_RANK, MLA_HEADS * MLA_HD), const)],
        out_specs=[pl.BlockSpec((1, MLA_HEADS, tm, MLA_HD), out_map),
                   pl.BlockSpec((1, MLA_HEADS, tm, MLA_HD), out_map),
                   pl.BlockSpec((1, MLA_HEADS, tm, MLA_V), out_map)],
        out_shape=[jax.ShapeDtypeStruct((BATCH, MLA_HEADS, SEQ, MLA_HD), BF16),
                   jax.ShapeDtypeStruct((BATCH, MLA_HEADS, SEQ, MLA_HD), BF16),
                   jax.ShapeDtypeStruct((BATCH, MLA_HEADS, SEQ, MLA_V), BF16)],
        compiler_params=_cparams(("arbitrary",)),
        name="mla_prep",
    )(proj, proj, proj, pos_col, rope_consts, qnw, kvnw, wq, wkv)


ATT_T = 512


def _attn_kernel(q_ref, k_ref, v_ref, o_ref, m_ref, l_ref, acc_ref):
    qi = pl.program_id(2)
    q = q_ref[0, 0]
    m_ref[...] = jnp.full(m_ref.shape, -jnp.inf, F32)
    l_ref[...] = jnp.zeros(l_ref.shape, F32)
    acc_ref[...] = jnp.zeros(acc_ref.shape, F32)

    def step(j, masked):
        r0 = pl.multiple_of(j * ATT_T, ATT_T)
        k = k_ref[0, 0, pl.ds(r0, ATT_T), :]
        v = v_ref[0, 0, pl.ds(r0, ATT_T), :]
        s = lax.dot_general(k, q, (((1,), (1,)), ((), ())), preferred_element_type=F32)
        if masked:
            key = lax.broadcasted_iota(jnp.int32, s.shape, 0)
            qry = lax.broadcasted_iota(jnp.int32, s.shape, 1)
            s = jnp.where(key <= qry, s, -jnp.inf)
        m_old = m_ref[...]
        m_new = jnp.maximum(m_old, jnp.max(s, axis=0, keepdims=True))
        p = jnp.exp(s - m_new)
        alpha = jnp.exp(m_old - m_new)
        l_ref[...] = alpha * l_ref[...] + jnp.sum(p, axis=0, keepdims=True)
        pv = lax.dot_general(v, p.astype(BF16), (((0,), (0,)), ((), ())), preferred_element_type=F32)
        acc_ref[...] = alpha * acc_ref[...] + pv
        m_ref[...] = m_new

    def body(j, carry):
        step(j, False)
        return carry

    lax.fori_loop(0, qi, body, 0)
    step(qi, True)
    o_ref[...] = jnp.transpose(acc_ref[...] / l_ref[...]).astype(BF16)


def _attention(q, k, v):
    t = ATT_T
    nq = SEQ // t
    return pl.pallas_call(
        _attn_kernel,
        grid=(BATCH, MLA_HEADS, nq),
        in_specs=[pl.BlockSpec((1, 1, t, MLA_HD), lambda b, h, i: (b, h, i, 0)),
                  pl.BlockSpec((1, 1, SEQ, MLA_HD), lambda b, h, i: (b, h, 0, 0)),
                  pl.BlockSpec((1, 1, SEQ, MLA_V), lambda b, h, i: (b, h, 0, 0))],
        out_specs=pl.BlockSpec((t, MLA_V), lambda b, h, i: (b * nq + i, h)),
        out_shape=jax.ShapeDtypeStruct((N_TOK, MLA_HEADS * MLA_V), BF16),
        scratch_shapes=[pltpu.VMEM((1, t), F32), pltpu.VMEM((1, t), F32), pltpu.VMEM((MLA_V, t), F32)],
        compiler_params=_cparams(("arbitrary", "arbitrary", "arbitrary")),
        name="mla_attn",
    )(q, k, v)


GLA_TG = 512
GLA_DIAG = 8


def _gla_kernel(q_ref, k_ref, v_ref, glr_ref, gout_ref, wgk_ref, bgk_ref, nw_ref, o_ref, st_ref, g_ref):
    C = GLA_CHUNK

    @pl.when(pl.program_id(2) == 0)
    def _():
        st_ref[...] = jnp.zeros(st_ref.shape, F32)

    z = jnp.dot(glr_ref[...].astype(F32), wgk_ref[...], preferred_element_type=F32,
                precision=lax.Precision.HIGHEST) + bgk_ref[...]
    g_ref[...] = jax.nn.log_sigmoid(z) / GLA_GATE_NORM

    row = lax.broadcasted_iota(jnp.int32, (C, GLA_DK), 0)
    ii = lax.broadcasted_iota(jnp.int32, (C, C), 0)
    jj = lax.broadcasted_iota(jnp.int32, (C, C), 1)
    nt = (((1,), (1,)), ((), ()))

    def chunk(ci, carry):
        r0 = pl.multiple_of(ci * C, C)
        b = g_ref[pl.ds(r0, C), :]
        s = 1
        while s < C:
            b = b + jnp.where(row >= s, pltpu.roll(b, s, 0), 0.0)
            s *= 2
        q = q_ref[pl.ds(r0, C), :].astype(F32) * (GLA_DK ** -0.5)
        k = k_ref[pl.ds(r0, C), :].astype(F32)
        v = v_ref[pl.ds(r0, C), :]
        st = st_ref[...]

        o = lax.dot_general((q * jnp.exp(b)).astype(BF16), st.astype(BF16), nt,
                            preferred_element_type=F32)

        a = jnp.zeros((C, C), F32)
        lvl = C // 2
        while lvl >= GLA_DIAG:
            pieces = [jnp.broadcast_to(b[t + lvl:t + lvl + 1, :], (2 * lvl, GLA_DK))
                      for t in range(0, C, 2 * lvl)]
            m = pieces[0] if len(pieces) == 1 else jnp.concatenate(pieces, axis=0)
            odd = (row // lvl) % 2 == 1
            ql = jnp.where(odd, q * jnp.exp(jnp.minimum(b - m, 0.0)), 0.0).astype(BF16)
            kl = jnp.where(odd, 0.0, k * jnp.exp(jnp.minimum(m - b, 0.0))).astype(BF16)
            al = lax.dot_general(ql, kl, nt, preferred_element_type=F32)
            a = a + jnp.where(ii // (2 * lvl) == jj // (2 * lvl), al, 0.0)
            lvl //= 2
        for d in range(GLA_DIAG):
            if d == 0:
                t = q * k
            else:
                t = q * pltpu.roll(k, d, 0) * jnp.exp(jnp.minimum(b - pltpu.roll(b, d, 0), 0.0))
            col = jnp.sum(t, axis=-1, keepdims=True)
            a = a + jnp.where((ii - jj == d) & (ii % GLA_DIAG >= d), col, 0.0)
        o = o + jnp.dot(a.astype(BF16), v, preferred_element_type=F32)

        b_last = b[C - 1:C, :]
        kdec = (k * jnp.exp(b_last - b)).astype(BF16)
        st_ref[...] = st * jnp.exp(b_last) + lax.dot_general(
            v, kdec, (((0,), (0,)), ((), ())), preferred_element_type=F32)

        gate = gout_ref[pl.ds(r0, C), :].astype(F32)
        y = _rms(o, nw_ref[...]) * (gate * jax.nn.sigmoid(gate))
        o_ref[pl.ds(r0, C), :] = y.astype(BF16)
        return carry

    lax.fori_loop(0, GLA_TG // C, chunk, 0)


def _gla(proj, wgk, bgk, nw):
    tg = GLA_TG
    per_b = SEQ // tg
    rb = lambda b, h, t: b * per_b + t
    return pl.pallas_call(
        _gla_kernel,
        grid=(BATCH, GLA_HEADS, per_b),
        in_specs=[pl.BlockSpec((tg, GLA_DK), lambda b, h, t: (rb(b, h, t), P_GQ // GLA_DK + h)),
                  pl.BlockSpec((tg, GLA_DK), lambda b, h, t: (rb(b, h, t), P_GK // GLA_DK + h)),
                  pl.BlockSpec((tg, GLA_DV), lambda b, h, t: (rb(b, h, t), P_GV // GLA_DV + h)),
                  pl.BlockSpec((tg, LANES), lambda b, h, t: (rb(b, h, t), P_GLR // LANES)),
                  pl.BlockSpec((tg, GLA_DV), lambda b, h, t: (rb(b, h, t), P_GOUT // GLA_DV + h)),
                  pl.BlockSpec((LANES, GLA_DK), lambda b, h, t: (0, h)),
                  pl.BlockSpec((1, GLA_DK), lambda b, h, t: (0, h)),
                  pl.BlockSpec((1, GLA_DV), lambda b, h, t: (0, 0))],
        out_specs=pl.BlockSpec((tg, GLA_DV), lambda b, h, t: (rb(b, h, t), h)),
        out_shape=jax.ShapeDtypeStruct((N_TOK, GLA_HEADS * GLA_DV), BF16),
        scratch_shapes=[pltpu.VMEM((GLA_DV, GLA_DK), F32), pltpu.VMEM((tg, GLA_DK), F32)],
        compiler_params=_cparams(("arbitrary", "arbitrary", "arbitrary")),
        name="gla",
    )(proj, proj, proj, proj, proj, wgk, bgk, nw)


MRG_TM = 256


def _merge_kernel(x_ref, oa_ref, ob_ref, ga_ref, gb_ref, gm_ref, shf_ref, scf_ref, nfw_ref,
                  woa_ref, wob_ref, wout_ref, wrh_ref, wrl_ref, br_ref,
                  x1_ref, h2_ref, idx_ref, tw_ref):
    ya = jnp.dot(oa_ref[...], woa_ref[...], preferred_element_type=F32)
    yb = jnp.dot(ob_ref[...], wob_ref[...], preferred_element_type=F32)
    merged = (jax.nn.sigmoid(ga_ref[...].astype(F32)) * ya
              + jax.nn.sigmoid(gb_ref[...].astype(F32)) * yb).astype(BF16)
    x1 = x_ref[...] + gm_ref[0] * jnp.dot(merged, wout_ref[...], preferred_element_type=F32)
    x1_ref[...] = x1
    h2 = _rms(x1, nfw_ref[...]) * (1.0 + scf_ref[0]) + shf_ref[0]
    h2_ref[...] = h2

    hi = h2.astype(BF16)
    lo = (h2 - hi.astype(F32)).astype(BF16)
    logits = (jnp.dot(hi, wrh_ref[...], preferred_element_type=F32)
              + jnp.dot(hi, wrl_ref[...], preferred_element_type=F32)
              + jnp.dot(lo, wrh_ref[...], preferred_element_type=F32)) + br_ref[...]
    lane = lax.broadcasted_iota(jnp.int32, logits.shape, 1)
    vals = jnp.where(lane < N_EXPERTS, logits, -jnp.inf)
    idx_out = jnp.zeros(logits.shape, jnp.int32)
    w_out = jnp.zeros(logits.shape, F32)
    top = None
    denom = None
    for kk in range(TOP_K):
        m = jnp.max(vals, axis=-1, keepdims=True)
        sel = jnp.min(jnp.where(vals == m, lane, LANES), axis=-1, keepdims=True)
        if kk == 0:
            top = m
        e = jnp.exp(m - top)
        denom = e if kk == 0 else denom + e
        idx_out = jnp.where(lane == kk, sel, idx_out)
        w_out = jnp.where(lane == kk, e, w_out)
        vals = jnp.where(lane == sel, -jnp.inf, vals)
    idx_ref[...] = idx_out
    tw_ref[...] = w_out / denom


def _merge(x2d, o_mla, o_gla, proj, gm, shf, scf, nfw, woa, wob, wout, wrh, wrl, br):
    tm = MRG_TM
    per_b = SEQ // tm
    row = lambda i: (i, 0)
    const = lambda i: (0, 0)
    bvec = lambda i: (i // per_b, 0, 0)
    single = dict(pipeline_mode=pl.Buffered(1))
    return pl.pallas_call(
        _merge_kernel,
        grid=(N_TOK // tm,),
        in_specs=[pl.BlockSpec((tm, D_MODEL), row),
                  pl.BlockSpec((tm, MLA_HEADS * MLA_V), row),
                  pl.BlockSpec((tm, GLA_HEADS * GLA_DV), row),
                  pl.BlockSpec((tm, D_MODEL), lambda i: (i, P_GATEA // D_MODEL)),
                  pl.BlockSpec((tm, D_MODEL), lambda i: (i, P_GATEB // D_MODEL)),
                  pl.BlockSpec((1, 1, D_MODEL), bvec),
                  pl.BlockSpec((1, 1, D_MODEL), bvec),
                  pl.BlockSpec((1, 1, D_MODEL), bvec),
                  pl.BlockSpec((1, D_MODEL), const),
                  pl.BlockSpec((MLA_HEADS * MLA_V, D_MODEL), const, **single),
                  pl.BlockSpec((GLA_HEADS * GLA_DV, D_MODEL), const, **single),
                  pl.BlockSpec((D_MODEL, D_MODEL), const, **single),
                  pl.BlockSpec((D_MODEL, LANES), const, **single),
                  pl.BlockSpec((D_MODEL, LANES), const, **single),
                  pl.BlockSpec((1, LANES), const)],
        out_specs=[pl.BlockSpec((tm, D_MODEL), row),
                   pl.BlockSpec((tm, D_MODEL), row),
                   pl.BlockSpec((tm, LANES), row),
                   pl.BlockSpec((tm, LANES), row)],
        out_shape=[jax.ShapeDtypeStruct((N_TOK, D_MODEL), F32),
                   jax.ShapeDtypeStruct((N_TOK, D_MODEL), F32),
                   jax.ShapeDtypeStruct((N_TOK, LANES), jnp.int32),
                   jax.ShapeDtypeStruct((N_TOK, LANES), F32)],
        compiler_params=_cparams(("arbitrary",)),
        name="merge_router",
    )(x2d, o_mla, o_gla, proj, proj, gm, shf, scf, nfw, woa, wob, wout, wrh, wrl, br)


MOE_UNROLL = 8


def _moe_kernel(it_e_ref, it_start_ref, it_nblk_ref, src_ref, dst_ref,
                h2_hbm, w1_ref, b1_ref, w2_ref, b2_ref, sel_ref, y4_hbm,
                stage_ref, xb_ref, acc_ref, w1b_ref, w2b_ref, gsem, ssem):
    it = pl.program_id(0)
    c = pl.program_id(1)
    nblk = it_nblk_ref[it]
    buf = it % 2

    def issue_gather(item):
        base = it_start_ref[item]

        def body(g, carry):
            r0 = g * MOE_UNROLL
            for u in range(MOE_UNROLL):
                tok = src_ref[base + r0 + u]
                pltpu.make_async_copy(h2_hbm.at[pl.ds(tok, 1), :],
                                      stage_ref.at[pl.ds(r0 + u, 1), :], gsem).start()
            return carry
        lax.fori_loop(0, it_nblk_ref[item] * (MOE_BLK // MOE_UNROLL), body, 0)

    def issue_scatter(item, b):
        base = it_start_ref[item]

        def body(g, carry):
            r0 = g * MOE_UNROLL
            for u in range(MOE_UNROLL):
                d = dst_ref[base + r0 + u]
                pltpu.make_async_copy(acc_ref.at[b, pl.ds(r0 + u, 1), :],
                                      y4_hbm.at[pl.ds(d, 1), :], ssem).start()
            return carry
        lax.fori_loop(0, it_nblk_ref[item] * (MOE_BLK // MOE_UNROLL), body, 0)

    def wait_gather_block(s):
        rows = pl.ds(s * MOE_BLK, MOE_BLK)
        pltpu.make_async_copy(h2_hbm.at[pl.ds(0, MOE_BLK), :], stage_ref.at[rows, :], gsem).wait()

    def wait_scatter_block(s, b):
        rows = pl.ds(s * MOE_BLK, MOE_BLK)
        pltpu.make_async_copy(acc_ref.at[b, rows, :], y4_hbm.at[pl.ds(0, MOE_BLK), :], ssem).wait()

    @pl.when(c == 0)
    def _():
        @pl.when(it == 0)
        def _():
            acc_ref[1, 0:MOE_BLK, :] = jnp.zeros((MOE_BLK, D_MODEL), F32)
            init = pltpu.make_async_copy(acc_ref.at[1, pl.ds(0, MOE_BLK), :],
                                         y4_hbm.at[pl.ds(N_ASSIGN, MOE_BLK), :], ssem)
            init.start()
            init.wait()
            issue_gather(0)

        for s in range(MOE_ITEM_BLKS):
            @pl.when(s < nblk)
            def _():
                wait_gather_block(s)

        for s in range(MOE_ITEM_BLKS):
            @pl.when(s < nblk)
            def _():
                rows = slice(s * MOE_BLK, (s + 1) * MOE_BLK)
                xb_ref[rows, :] = stage_ref[rows, :].astype(BF16)
                acc_ref[buf, rows, :] = jnp.broadcast_to(b2_ref[0], (MOE_BLK, D_MODEL))

        issue_gather(it + 1)

        @pl.when(it > 0)
        def _():
            issue_scatter(it - 1, 1 - buf)

    @pl.when(nblk > 0)
    def _():
        w1b_ref[...] = w1_ref[0].astype(BF16)
        w2b_ref[...] = w2_ref[0].astype(BF16)

        def block(s, carry):
            rows = pl.ds(pl.multiple_of(s * MOE_BLK, MOE_BLK), MOE_BLK)
            hm = jnp.dot(xb_ref[rows, :], w1b_ref[...], preferred_element_type=F32) + b1_ref[0]
            lin = pltpu.roll(hm, 2 * MOE_TH - 1, 1)
            glu = jnp.minimum(hm, SWIGLU_LIMIT)
            lin = jnp.clip(lin, -SWIGLU_LIMIT, SWIGLU_LIMIT)
            act = glu * jax.nn.sigmoid(SWIGLU_ALPHA * glu) * (lin + 1.0)
            lane = lax.broadcasted_iota(jnp.int32, act.shape, 1)
            act = jnp.where(lane % 2 == 0, act, 0.0).astype(BF16)
            act = jnp.dot(act, sel_ref[...], preferred_element_type=F32).astype(BF16)
            acc_ref[buf, rows, :] += jnp.dot(act, w2b_ref[...], preferred_element_type=F32)
            return carry
        lax.fori_loop(0, nblk, block, 0)

    @pl.when(jnp.logical_and(c == MOE_NCHUNK - 1, it > 0))
    def _():
        for s in range(MOE_ITEM_BLKS):
            @pl.when(s < it_nblk_ref[it - 1])
            def _():
                wait_scatter_block(s, 1 - buf)


def _moe(it_e, it_start, it_nblk, slot_src, slot_dst, h2, w1, b1, w2, b2):
    def chunk_of(it, c, n_ref):
        return jnp.where(n_ref[it] > 0, c, MOE_NCHUNK - 1)

    def w1_map(it, c, e_ref, s_ref, n_ref, *_):
        return (e_ref[it], 0, chunk_of(it, c, n_ref))

    def w2_map(it, c, e_ref, s_ref, n_ref, *_):
        return (e_ref[it], chunk_of(it, c, n_ref), 0)

    def b2_map(it, c, e_ref, *_):
        return (e_ref[it], 0, 0)

    sel = (jnp.arange(2 * MOE_TH, dtype=jnp.int32)[:, None]
           == 2 * jnp.arange(MOE_TH, dtype=jnp.int32)[None, :]).astype(BF16)

    grid_spec = pltpu.PrefetchScalarGridSpec(
        num_scalar_prefetch=5,
        grid=(MOE_NITEMS + 1, MOE_NCHUNK),
        in_specs=[pl.BlockSpec(memory_space=pl.ANY),
                  pl.BlockSpec((1, D_MODEL, 2 * MOE_TH), w1_map),
                  pl.BlockSpec((1, 1, 2 * MOE_TH), w1_map),
                  pl.BlockSpec((1, MOE_TH, D_MODEL), w2_map),
                  pl.BlockSpec((1, 1, D_MODEL), b2_map),
                  pl.BlockSpec((2 * MOE_TH, MOE_TH), lambda it, c, *_: (0, 0))],
        out_specs=pl.BlockSpec(memory_space=pl.ANY),
        scratch_shapes=[pltpu.VMEM((MOE_ITEM_ROWS, D_MODEL), F32),
                        pltpu.VMEM((MOE_ITEM_ROWS, D_MODEL), BF16),
                        pltpu.VMEM((2, MOE_ITEM_ROWS, D_MODEL), F32),
                        pltpu.VMEM((D_MODEL, 2 * MOE_TH), BF16),
                        pltpu.VMEM((MOE_TH, D_MODEL), BF16),
                        pltpu.SemaphoreType.DMA(()),
                        pltpu.SemaphoreType.DMA(())],
    )
    return pl.pallas_call(
        _moe_kernel,
        grid_spec=grid_spec,
        out_shape=jax.ShapeDtypeStruct((N_ASSIGN + MOE_BLK, D_MODEL), F32),
        compiler_params=_cparams(("arbitrary", "arbitrary")),
        name="moe_experts",
    )(it_e, it_start, it_nblk, slot_src, slot_dst, h2, w1,
      b1.reshape(N_EXPERTS, 1, 2 * D_EXPERT), w2, b2.reshape(N_EXPERTS, 1, D_MODEL), sel)


def _route_tables(top_idx):
    e_flat = top_idx.reshape(N_ASSIGN)
    onehot = (e_flat[:, None] == jnp.arange(N_EXPERTS, dtype=jnp.int32)[None, :]).astype(jnp.int32)
    csum = jnp.cumsum(onehot, axis=0)
    rank = jnp.sum(csum * onehot, axis=-1) - 1
    counts = csum[-1]
    nblk = (counts + MOE_BLK - 1) // MOE_BLK
    blk_end = jnp.cumsum(nblk)
    blk_start = blk_end - nblk
    dest = blk_start[e_flat] * MOE_BLK + rank
    slot_a = jnp.full((MOE_SLOTS,), -1, jnp.int32).at[dest].set(jnp.arange(N_ASSIGN, dtype=jnp.int32))
    slot_src = jnp.where(slot_a >= 0, slot_a // TOP_K, 0)
    slot_dst = jnp.where(slot_a >= 0, (slot_a % TOP_K) * N_TOK + slot_a // TOP_K,
                         N_ASSIGN + jnp.arange(MOE_SLOTS, dtype=jnp.int32) % MOE_BLK)

    nit = (nblk + MOE_ITEM_BLKS - 1) // MOE_ITEM_BLKS
    it_end = jnp.cumsum(nit)
    it_begin = it_end - nit
    n_used = it_end[-1]
    ids = jnp.arange(MOE_NITEMS + 2, dtype=jnp.int32)
    valid = ids < n_used
    last = jnp.minimum(ids, n_used - 1)
    e_of = jnp.minimum(jnp.sum((it_end[None, :] <= last[:, None]).astype(jnp.int32), axis=1), N_EXPERTS - 1)
    local = ids - it_begin[e_of]
    it_start = jnp.where(valid, (blk_start[e_of] + local * MOE_ITEM_BLKS) * MOE_BLK, 0).astype(jnp.int32)
    it_nblk = jnp.where(valid, jnp.clip(nblk[e_of] - local * MOE_ITEM_BLKS, 0, MOE_ITEM_BLKS), 0)
    return e_of, it_start, it_nblk.astype(jnp.int32), slot_src, slot_dst


FIN_TM = 256


def _final_kernel(x1_ref, ya_ref, yb_ref, yc_ref, yd_ref, tw_ref, gf_ref, sho_ref, sco_ref, nw_ref, o_ref):
    tw = tw_ref[...]
    moe = tw[:, 0:1] * ya_ref[...]
    for kk, y_ref in enumerate((yb_ref, yc_ref, yd_ref), start=1):
        moe = moe + tw[:, kk:kk + 1] * y_ref[...]
    x2 = x1_ref[...] + gf_ref[0] * moe
    o_ref[...] = _rms(x2, nw_ref[...]) * (1.0 + sco_ref[0]) + sho_ref[0]


def _final(x1, y4, tw, gf, sho, sco, nw):
    tm = FIN_TM
    per_b = SEQ // tm
    row = lambda i: (i, 0)
    bvec = lambda i: (i // per_b, 0, 0)
    nb = N_TOK // tm
    kth = lambda kk: pl.BlockSpec((tm, D_MODEL), lambda i: (kk * nb + i, 0))
    return pl.pallas_call(
        _final_kernel,
        grid=(nb,),
        in_specs=[pl.BlockSpec((tm, D_MODEL), row),
                  kth(0), kth(1), kth(2), kth(3),
                  pl.BlockSpec((tm, LANES), row),
                  pl.BlockSpec((1, 1, D_MODEL), bvec),
                  pl.BlockSpec((1, 1, D_MODEL), bvec),
                  pl.BlockSpec((1, 1, D_MODEL), bvec),
                  pl.BlockSpec((1, D_MODEL), lambda i: (0, 0))],
        out_specs=pl.BlockSpec((tm, D_MODEL), row),
        out_shape=jax.ShapeDtypeStruct((N_TOK, D_MODEL), F32),
        compiler_params=_cparams(("arbitrary",)),
        name="combine_final",
    )(x1, y4, y4, y4, y4, tw, gf, sho, sco, nw)


def _deinterleave(n):
    return np.concatenate([np.arange(0, n, 2), np.arange(1, n, 2)])


def _pack_w_in(w_in):
    o = np.cumsum([0, MLA_Q_RANK, MLA_KV_RANK + MLA_ROPE, GLA_HEADS * GLA_DK, GLA_HEADS * GLA_DK,
                   GLA_HEADS * GLA_DV, GLA_GATE_RANK, GLA_HEADS * GLA_DV, D_MODEL, D_MODEL])
    seg = lambda i: w_in[:, o[i]:o[i + 1]]
    kv = seg(1)
    zeros = lambda n: jnp.zeros((D_MODEL, n), w_in.dtype)
    kpe = kv[:, MLA_KV_RANK:][:, _deinterleave(MLA_ROPE)]
    parts = [seg(0), kv[:, :MLA_KV_RANK], kpe, zeros(LANES - MLA_ROPE),
             seg(5), zeros(LANES - GLA_GATE_RANK), seg(2), seg(3), seg(4), seg(6), seg(7), seg(8)]
    return jnp.concatenate(parts, axis=1).astype(BF16)


def _pack_w_q_b(w):
    w = w.reshape(MLA_Q_RANK, MLA_HEADS, MLA_NOPE + MLA_ROPE)
    nope = w[:, :, :MLA_NOPE]
    pe = w[:, :, MLA_NOPE:][:, :, _deinterleave(MLA_ROPE)]
    pad = jnp.zeros((MLA_Q_RANK, MLA_HEADS, MLA_HD - MLA_NOPE - MLA_ROPE), w.dtype)
    return jnp.concatenate([nope, pe, pad], axis=-1).reshape(MLA_Q_RANK, MLA_HEADS * MLA_HD).astype(BF16)


def _rope_consts():
    inv_freq = ROPE_THETA ** (-(jnp.arange(0, MLA_ROPE, 2, dtype=F32) / MLA_ROPE))
    half = MLA_ROPE // 2
    z = jnp.zeros((LANES - MLA_ROPE,), F32)
    rows = [jnp.concatenate([inv_freq, inv_freq, z]),
            jnp.concatenate([-jnp.ones((half,), F32), jnp.zeros((half,), F32), z]),
            jnp.concatenate([jnp.zeros((half,), F32), jnp.ones((half,), F32), z])]
    return jnp.concatenate([jnp.stack(rows), jnp.zeros((5, LANES), F32)], axis=0)


def kernel(x, c, positions, w_ada, b_ada, norm_mix_w, w_in, mla_q_norm_w, mla_w_q_b, mla_kv_norm_w,
           mla_w_kv_b, gla_w_gk_up, gla_b_gk_up, gla_norm_w, w_o_mla, w_o_gla, w_out, norm_ffn_w,
           w_router, b_router, w1, b1, w2, b2, w_ada_final, b_ada_final, norm_final_w):
    assert w_ada.shape[0] == 1, "single-layer stack"
    x2d = x.reshape(N_TOK, D_MODEL)
    c_t = c.T

    mod = _adaln(c_t, w_ada[0], b_ada[0])
    fmod = _adaln(c_t, w_ada_final, b_ada_final)
    sh_m, sc_m, g_m, sh_f, sc_f, g_f = [m.reshape(BATCH, 1, D_MODEL) for m in jnp.split(mod, N_MOD, axis=-1)]
    sh_o, sc_o = [m.reshape(BATCH, 1, D_MODEL) for m in jnp.split(fmod, 2, axis=-1)]

    proj = _inproj(x2d, norm_mix_w[0].reshape(1, D_MODEL), sh_m, sc_m, _pack_w_in(w_in[0]))

    q, k, v = _mla_prep(proj, positions.reshape(N_TOK, 1), _rope_consts(),
                        mla_q_norm_w[0].reshape(1, MLA_Q_RANK), mla_kv_norm_w[0].reshape(1, MLA_KV_RANK),
                        _pack_w_q_b(mla_w_q_b[0]), mla_w_kv_b[0].astype(BF16))
    o_mla = _attention(q, k, v)

    wgk = jnp.concatenate([gla_w_gk_up[0], jnp.zeros((LANES - GLA_GATE_RANK, GLA_HEADS * GLA_DK), F32)], axis=0)
    o_gla = _gla(proj, wgk, gla_b_gk_up[0].reshape(1, GLA_HEADS * GLA_DK), gla_norm_w[0].reshape(1, GLA_DV))

    wr = jnp.concatenate([w_router[0], jnp.zeros((D_MODEL, LANES - N_EXPERTS), F32)], axis=1)
    wr_hi = wr.astype(BF16)
    wr_lo = (wr - wr_hi.astype(F32)).astype(BF16)
    br = jnp.concatenate([b_router[0], jnp.zeros((LANES - N_EXPERTS,), F32)]).reshape(1, LANES)
    x1, h2, top_idx, top_w = _merge(
        x2d, o_mla, o_gla, proj, g_m, sh_f, sc_f, norm_ffn_w[0].reshape(1, D_MODEL),
        w_o_mla[0].astype(BF16), w_o_gla[0].astype(BF16), w_out[0].astype(BF16), wr_hi, wr_lo, br)

    it_e, it_start, it_nblk, slot_src, slot_dst = _route_tables(top_idx[:, :TOP_K])
    y4 = _moe(it_e, it_start, it_nblk, slot_src, slot_dst, h2, w1[0], b1[0], w2[0], b2[0])

    out = _final(x1, y4, top_w, g_f, sh_o, sc_o, norm_final_w.reshape(1, D_MODEL))
    return out.reshape(BATCH, SEQ, D_MODEL)
```

```python
import functools

import jax
import jax.numpy as jnp
import numpy as np
from jax import lax
from jax.experimental import pallas as pl
from jax.experimental.pallas import tpu as pltpu

F32 = jnp.float32
BF16 = jnp.bfloat16

D_MODEL = 2048
BATCH = 2
SEQ = 4096
N_TOK = BATCH * SEQ

MLA_HEADS = 8
MLA_Q_RANK = 512
MLA_KV_RANK = 256
MLA_NOPE = 128
MLA_ROPE = 64
MLA_V = 128
ROPE_THETA = 10000.0
MLA_HD = 256

GLA_HEADS = 4
GLA_DK = 128
GLA_DV = 256
GLA_GATE_RANK = 16
GLA_GATE_NORM = 16.0
GLA_CHUNK = 64

N_EXPERTS = 32
TOP_K = 4
D_EXPERT = D_MODEL
SWIGLU_LIMIT = 7.0
SWIGLU_ALPHA = 1.702
EPS = 1e-6
N_MOD = 6

LANES = 128
VMEM_LIMIT = 56 * 1024 * 1024

P_QLAT = 0
P_CKV = 512
P_KPE = 768
P_GLR = 896
P_GQ = 1024
P_GK = 1536
P_GV = 2048
P_GOUT = 3072
P_GATEA = 4096
P_GATEB = 6144
P_TOTAL = 8192

MOE_BLK = 256
MOE_ITEM_BLKS = 4
MOE_ITEM_ROWS = MOE_BLK * MOE_ITEM_BLKS
MOE_TH = 256
MOE_NCHUNK = D_EXPERT // MOE_TH
N_ASSIGN = N_TOK * TOP_K
MOE_NBLK_MAX = N_ASSIGN // MOE_BLK + N_EXPERTS
MOE_SLOTS = MOE_NBLK_MAX * MOE_BLK
MOE_NITEMS = N_ASSIGN // MOE_ITEM_ROWS + N_EXPERTS


def _cparams(sem, vmem=VMEM_LIMIT):
    return pltpu.CompilerParams(dimension_semantics=sem, vmem_limit_bytes=vmem)


def _rms(x, w):
    return x * lax.rsqrt(jnp.mean(x * x, axis=-1, keepdims=True) + EPS) * w


def _adaln_kernel(ct_ref, w_ref, b_ref, o_ref):
    ct = ct_ref[...]
    cond = ct * jax.nn.sigmoid(ct)
    w = w_ref[...]
    for b in range(BATCH):
        o_ref[b:b + 1, :] = jnp.sum(w * cond[:, b:b + 1], axis=0, keepdims=True) + b_ref[...]


def _adaln(c_t, w, b, tn=1024):
    d, n = w.shape
    return pl.pallas_call(
        _adaln_kernel,
        grid=(n // tn,),
        in_specs=[pl.BlockSpec((d, BATCH), lambda j: (0, 0)),
                  pl.BlockSpec((d, tn), lambda j: (0, j)),
                  pl.BlockSpec((1, tn), lambda j: (0, j))],
        out_specs=pl.BlockSpec((BATCH, tn), lambda j: (0, j)),
        out_shape=jax.ShapeDtypeStruct((BATCH, n), F32),
        compiler_params=_cparams(("arbitrary",)),
        name="adaln",
    )(c_t, w, b.reshape(1, n))


IN_TM = 1024
IN_TN = 1024


def _inproj_kernel(x_ref, nw_ref, sh_ref, sc_ref, w_ref, o_ref, h_ref):
    @pl.when(pl.program_id(1) == 0)
    def _():
        for r in range(0, IN_TM, 256):
            x = x_ref[r:r + 256, :]
            h = _rms(x, nw_ref[...]) * (1.0 + sc_ref[0]) + sh_ref[0]
            h_ref[r:r + 256, :] = h.astype(BF16)

    o_ref[...] = jnp.dot(h_ref[...], w_ref[...], preferred_element_type=F32).astype(BF16)


def _inproj(x2d, nw, sh, sc, w_packed):
    per_b = SEQ // IN_TM
    return pl.pallas_call(
        _inproj_kernel,
        grid=(N_TOK // IN_TM, P_TOTAL // IN_TN),
        in_specs=[pl.BlockSpec((IN_TM, D_MODEL), lambda i, j: (i, 0)),
                  pl.BlockSpec((1, D_MODEL), lambda i, j: (0, 0)),
                  pl.BlockSpec((1, 1, D_MODEL), lambda i, j: (i // per_b, 0, 0)),
                  pl.BlockSpec((1, 1, D_MODEL), lambda i, j: (i // per_b, 0, 0)),
                  pl.BlockSpec((D_MODEL, IN_TN), lambda i, j: (0, j))],
        out_specs=pl.BlockSpec((IN_TM, IN_TN), lambda i, j: (i, j)),
        out_shape=jax.ShapeDtypeStruct((N_TOK, P_TOTAL), BF16),
        scratch_shapes=[pltpu.VMEM((IN_TM, D_MODEL), BF16)],
        compiler_params=_cparams(("arbitrary", "arbitrary")),
        name="inproj",
    )(x2d, nw, sh, sc, w_packed)


PREP_TM = 512


def _mla_prep_kernel(ql_ref, ckv_ref, kpe_ref, pos_ref, rc_ref, qnw_ref, kvnw_ref, wq_ref, wkv_ref,
                     q_ref, k_ref, v_ref):
    scale = (MLA_NOPE + MLA_ROPE) ** -0.5
    qn = _rms(ql_ref[...].astype(F32), qnw_ref[...]).astype(BF16)
    q = jnp.dot(qn, wq_ref[...], preferred_element_type=F32)
    cn = _rms(ckv_ref[...].astype(F32), kvnw_ref[...]).astype(BF16)
    kv = jnp.dot(cn, wkv_ref[...], preferred_element_type=F32)

    ang = pos_ref[...].astype(F32) * rc_ref[0:1, :]
    cos_t = jnp.cos(ang)
    sin_t = jnp.sin(ang)
    sin_a = sin_t * rc_ref[1:2, :]
    sin_b = sin_t * rc_ref[2:3, :]

    def rope(t):
        return (t * cos_t + pltpu.roll(t, LANES - MLA_ROPE // 2, 1) * sin_a
                + pltpu.roll(t, MLA_ROPE // 2, 1) * sin_b)

    kpe = rope(kpe_ref[...].astype(F32)).astype(BF16)
    for h in range(MLA_HEADS):
        c0 = h * MLA_HD
        q_ref[0, h, :, 0:LANES] = (q[:, c0:c0 + LANES] * scale).astype(BF16)
        q_ref[0, h, :, LANES:MLA_HD] = (rope(q[:, c0 + LANES:c0 + MLA_HD]) * scale).astype(BF16)
        k_ref[0, h, :, 0:LANES] = kv[:, c0:c0 + LANES].astype(BF16)
        k_ref[0, h, :, LANES:MLA_HD] = kpe
        v_ref[0, h, :, :] = kv[:, c0 + LANES:c0 + MLA_HD].astype(BF16)


def _mla_prep(proj, pos_col, rope_consts, qnw, kvnw, wq, wkv):
    tm = PREP_TM
    per_b = SEQ // tm
    row = lambda i: (i, 0)
    const = lambda i: (0, 0)
    out_map = lambda i: (i // per_b, 0, i % per_b, 0)
    return pl.pallas_call(
        _mla_prep_kernel,
        grid=(N_TOK // tm,),
        in_specs=[pl.BlockSpec((tm, MLA_Q_RANK), lambda i: (i, P_QLAT // MLA_Q_RANK)),
                  pl.BlockSpec((tm, MLA_KV_RANK), lambda i: (i, P_CKV // MLA_KV_RANK)),
                  pl.BlockSpec((tm, LANES), lambda i: (i, P_KPE // LANES)),
                  pl.BlockSpec((tm, 1), row),
                  pl.BlockSpec((8, LANES), const),
                  pl.BlockSpec((1, MLA_Q_RANK), const),
                  pl.BlockSpec((1, MLA_KV_RANK), const),
                  pl.BlockSpec((MLA_Q_RANK, MLA_HEADS * MLA_HD), const),
                  pl.BlockSpec((MLA_KV_RANK, MLA_HEADS * MLA_HD), const)],
        out_specs=[pl.BlockSpec((1, MLA_HEADS, tm, MLA_HD), out_map),
                   pl.BlockSpec((1, MLA_HEADS, tm, MLA_HD), out_map),
                   pl.BlockSpec((1, MLA_HEADS, tm, MLA_V), out_map)],
        out_shape=[jax.ShapeDtypeStruct((BATCH, MLA_HEADS, SEQ, MLA_HD), BF16),
                   jax.ShapeDtypeStruct((BATCH, MLA_HEADS, SEQ, MLA_HD), BF16),
                   jax.ShapeDtypeStruct((BATCH, MLA_HEADS, SEQ, MLA_V), BF16)],
        compiler_params=_cparams(("arbitrary",)),
        name="mla_prep",
    )(proj, proj, proj, pos_col, rope_consts, qnw, kvnw, wq, wkv)


ATT_T = 512


def _attn_kernel(q_ref, k_ref, v_ref, o_ref, m_ref, l_ref, acc_ref):
    qi = pl.program_id(2)
    q = q_ref[0, 0]
    m_ref[...] = jnp.full(m_ref.shape, -jnp.inf, F32)
    l_ref[...] = jnp.zeros(l_ref.shape, F32)
    acc_ref[...] = jnp.zeros(acc_ref.shape, F32)

    def step(j, masked):
        r0 = pl.multiple_of(j * ATT_T, ATT_T)
        k = k_ref[0, 0, pl.ds(r0, ATT_T), :]
        v = v_ref[0, 0, pl.ds(r0, ATT_T), :]
        s = lax.dot_general(k, q, (((1,), (1,)), ((), ())), preferred_element_type=F32)
        if masked:
            key = lax.broadcasted_iota(jnp.int32, s.shape, 0)
            qry = lax.broadcasted_iota(jnp.int32, s.shape, 1)
            s = jnp.where(key <= qry, s, -jnp.inf)
        m_old = m_ref[...]
        m_new = jnp.maximum(m_old, jnp.max(s, axis=0, keepdims=True))
        p = jnp.exp(s - m_new)
        alpha = jnp.exp(m_old - m_new)
        l_ref[...] = alpha * l_ref[...] + jnp.sum(p, axis=0, keepdims=True)
        pv = lax.dot_general(v, p.astype(BF16), (((0,), (0,)), ((), ())), preferred_element_type=F32)
        acc_ref[...] = alpha * acc_ref[...] + pv
        m_ref[...] = m_new

    def body(j, carry):
        step(j, False)
        return carry

    lax.fori_loop(0, qi, body, 0)
    step(qi, True)
    o_ref[...] = jnp.transpose(acc_ref[...] / l_ref[...]).astype(BF16)


def _attention(q, k, v):
    t = ATT_T
    nq = SEQ // t
    return pl.pallas_call(
        _attn_kernel,
        grid=(BATCH, MLA_HEADS, nq),
        in_specs=[pl.BlockSpec((1, 1, t, MLA_HD), lambda b, h, i: (b, h, i, 0)),
                  pl.BlockSpec((1, 1, SEQ, MLA_HD), lambda b, h, i: (b, h, 0, 0)),
                  pl.BlockSpec((1, 1, SEQ, MLA_V), lambda b, h, i: (b, h, 0, 0))],
        out_specs=pl.BlockSpec((t, MLA_V), lambda b, h, i: (b * nq + i, h)),
        out_shape=jax.ShapeDtypeStruct((N_TOK, MLA_HEADS * MLA_V), BF16),
        scratch_shapes=[pltpu.VMEM((1, t), F32), pltpu.VMEM((1, t), F32), pltpu.VMEM((MLA_V, t), F32)],
        compiler_params=_cparams(("arbitrary", "arbitrary", "arbitrary")),
        name="mla_attn",
    )(q, k, v)


GLA_TG = 512
GLA_DIAG = 8


def _gla_kernel(q_ref, k_ref, v_ref, glr_ref, gout_ref, wgk_ref, bgk_ref, nw_ref, o_ref, st_ref, g_ref):
    C = GLA_CHUNK

    @pl.when(pl.program_id(2) == 0)
    def _():
        st_ref[...] = jnp.zeros(st_ref.shape, F32)

    z = jnp.dot(glr_ref[...].astype(F32), wgk_ref[...], preferred_element_type=F32,
                precision=lax.Precision.HIGHEST) + bgk_ref[...]
    g_ref[...] = jax.nn.log_sigmoid(z) / GLA_GATE_NORM

    row = lax.broadcasted_iota(jnp.int32, (C, GLA_DK), 0)
    ii = lax.broadcasted_iota(jnp.int32, (C, C), 0)
    jj = lax.broadcasted_iota(jnp.int32, (C, C), 1)
    nt = (((1,), (1,)), ((), ()))

    def chunk(ci, carry):
        r0 = pl.multiple_of(ci * C, C)
        b = g_ref[pl.ds(r0, C), :]
        s = 1
        while s < C:
            b = b + jnp.where(row >= s, pltpu.roll(b, s, 0), 0.0)
            s *= 2
        q = q_ref[pl.ds(r0, C), :].astype(F32) * (GLA_DK ** -0.5)
        k = k_ref[pl.ds(r0, C), :].astype(F32)
        v = v_ref[pl.ds(r0, C), :]
        st = st_ref[...]

        o = lax.dot_general((q * jnp.exp(b)).astype(BF16), st.astype(BF16), nt,
                            preferred_element_type=F32)

        a = jnp.zeros((C, C), F32)
        lvl = C // 2
        while lvl >= GLA_DIAG:
            pieces = [jnp.broadcast_to(b[t + lvl:t + lvl + 1, :], (2 * lvl, GLA_DK))
                      for t in range(0, C, 2 * lvl)]
            m = pieces[0] if len(pieces) == 1 else jnp.concatenate(pieces, axis=0)
            odd = (row // lvl) % 2 == 1
            ql = jnp.where(odd, q * jnp.exp(jnp.minimum(b - m, 0.0)), 0.0).astype(BF16)
            kl = jnp.where(odd, 0.0, k * jnp.exp(jnp.minimum(m - b, 0.0))).astype(BF16)
            al = lax.dot_general(ql, kl, nt, preferred_element_type=F32)
            a = a + jnp.where(ii // (2 * lvl) == jj // (2 * lvl), al, 0.0)
            lvl //= 2
        for d in range(GLA_DIAG):
            if d == 0:
                t = q * k
            else:
                t = q * pltpu.roll(k, d, 0) * jnp.exp(jnp.minimum(b - pltpu.roll(b, d, 0), 0.0))
            col = jnp.sum(t, axis=-1, keepdims=True)
            a = a + jnp.where((ii - jj == d) & (ii % GLA_DIAG >= d), col, 0.0)
        o = o + jnp.dot(a.astype(BF16), v, preferred_element_type=F32)

        b_last = b[C - 1:C, :]
        kdec = (k * jnp.exp(b_last - b)).astype(BF16)
        st_ref[...] = st * jnp.exp(b_last) + lax.dot_general(
            v, kdec, (((0,), (0,)), ((), ())), preferred_element_type=F32)

        gate = gout_ref[pl.ds(r0, C), :].astype(F32)
        y = _rms(o, nw_ref[...]) * (gate * jax.nn.sigmoid(gate))
        o_ref[pl.ds(r0, C), :] = y.astype(BF16)
        return carry

    lax.fori_loop(0, GLA_TG // C, chunk, 0)


def _gla(proj, wgk, bgk, nw):
    tg = GLA_TG
    per_b = SEQ // tg
    rb = lambda b, h, t: b * per_b + t
    return pl.pallas_call(
        _gla_kernel,
        grid=(BATCH, GLA_HEADS, per_b),
        in_specs=[pl.BlockSpec((tg, GLA_DK), lambda b, h, t: (rb(b, h, t), P_GQ // GLA_DK + h)),
                  pl.BlockSpec((tg, GLA_DK), lambda b, h, t: (rb(b, h, t), P_GK // GLA_DK + h)),
                  pl.BlockSpec((tg, GLA_DV), lambda b, h, t: (rb(b, h, t), P_GV // GLA_DV + h)),
                  pl.BlockSpec((tg, LANES), lambda b, h, t: (rb(b, h, t), P_GLR // LANES)),
                  pl.BlockSpec((tg, GLA_DV), lambda b, h, t: (rb(b, h, t), P_GOUT // GLA_DV + h)),
                  pl.BlockSpec((LANES, GLA_DK), lambda b, h, t: (0, h)),
                  pl.BlockSpec((1, GLA_DK), lambda b, h, t: (0, h)),
                  pl.BlockSpec((1, GLA_DV), lambda b, h, t: (0, 0))],
        out_specs=pl.BlockSpec((tg, GLA_DV), lambda b, h, t: (rb(b, h, t), h)),
        out_shape=jax.ShapeDtypeStruct((N_TOK, GLA_HEADS * GLA_DV), BF16),
        scratch_shapes=[pltpu.VMEM((GLA_DV, GLA_DK), F32), pltpu.VMEM((tg, GLA_DK), F32)],
        compiler_params=_cparams(("arbitrary", "arbitrary", "arbitrary")),
        name="gla",
    )(proj, proj, proj, proj, proj, wgk, bgk, nw)


MRG_TM = 256


def _merge_kernel(x_ref, oa_ref, ob_ref, ga_ref, gb_ref, gm_ref, shf_ref, scf_ref, nfw_ref,
                  woa_ref, wob_ref, wout_ref, wrh_ref, wrl_ref, br_ref,
                  x1_ref, h2_ref, idx_ref, tw_ref):
    ya = jnp.dot(oa_ref[...], woa_ref[...], preferred_element_type=F32)
    yb = jnp.dot(ob_ref[...], wob_ref[...], preferred_element_type=F32)
    merged = (jax.nn.sigmoid(ga_ref[...].astype(F32)) * ya
              + jax.nn.sigmoid(gb_ref[...].astype(F32)) * yb).astype(BF16)
    x1 = x_ref[...] + gm_ref[0] * jnp.dot(merged, wout_ref[...], preferred_element_type=F32)
    x1_ref[...] = x1
    h2 = _rms(x1, nfw_ref[...]) * (1.0 + scf_ref[0]) + shf_ref[0]
    h2_ref[...] = h2

    hi = h2.astype(BF16)
    lo = (h2 - hi.astype(F32)).astype(BF16)
    logits = (jnp.dot(hi, wrh_ref[...], preferred_element_type=F32)
              + jnp.dot(hi, wrl_ref[...], preferred_element_type=F32)
              + jnp.dot(lo, wrh_ref[...], preferred_element_type=F32)) + br_ref[...]
    lane = lax.broadcasted_iota(jnp.int32, logits.shape, 1)
    vals = jnp.where(lane < N_EXPERTS, logits, -jnp.inf)
    idx_out = jnp.zeros(logits.shape, jnp.int32)
    w_out = jnp.zeros(logits.shape, F32)
    top = None
    denom = None
    for kk in range(TOP_K):
        m = jnp.max(vals, axis=-1, keepdims=True)
        sel = jnp.min(jnp.where(vals == m, lane, LANES), axis=-1, keepdims=True)
        if kk == 0:
            top = m
        e = jnp.exp(m - top)
        denom = e if kk == 0 else denom + e
        idx_out = jnp.where(lane == kk, sel, idx_out)
        w_out = jnp.where(lane == kk, e, w_out)
        vals = jnp.where(lane == sel, -jnp.inf, vals)
    idx_ref[...] = idx_out
    tw_ref[...] = w_out / denom


def _merge(x2d, o_mla, o_gla, proj, gm, shf, scf, nfw, woa, wob, wout, wrh, wrl, br):
    tm = MRG_TM
    per_b = SEQ // tm
    row = lambda i: (i, 0)
    const = lambda i: (0, 0)
    bvec = lambda i: (i // per_b, 0, 0)
    single = dict(pipeline_mode=pl.Buffered(1))
    return pl.pallas_call(
        _merge_kernel,
        grid=(N_TOK // tm,),
        in_specs=[pl.BlockSpec((tm, D_MODEL), row),
                  pl.BlockSpec((tm, MLA_HEADS * MLA_V), row),
                  pl.BlockSpec((tm, GLA_HEADS * GLA_DV), row),
                  pl.BlockSpec((tm, D_MODEL), lambda i: (i, P_GATEA // D_MODEL)),
                  pl.BlockSpec((tm, D_MODEL), lambda i: (i, P_GATEB // D_MODEL)),
                  pl.BlockSpec((1, 1, D_MODEL), bvec),
                  pl.BlockSpec((1, 1, D_MODEL), bvec),
                  pl.BlockSpec((1, 1, D_MODEL), bvec),
                  pl.BlockSpec((1, D_MODEL), const),
                  pl.BlockSpec((MLA_HEADS * MLA_V, D_MODEL), const, **single),
                  pl.BlockSpec((GLA_HEADS * GLA_DV, D_MODEL), const, **single),
                  pl.BlockSpec((D_MODEL, D_MODEL), const, **single),
                  pl.BlockSpec((D_MODEL, LANES), const, **single),
                  pl.BlockSpec((D_MODEL, LANES), const, **single),
                  pl.BlockSpec((1, LANES), const)],
        out_specs=[pl.BlockSpec((tm, D_MODEL), row),
                   pl.BlockSpec((tm, D_MODEL), row),
                   pl.BlockSpec((tm, LANES), row),
                   pl.BlockSpec((tm, LANES), row)],
        out_shape=[jax.ShapeDtypeStruct((N_TOK, D_MODEL), F32),
                   jax.ShapeDtypeStruct((N_TOK, D_MODEL), F32),
                   jax.ShapeDtypeStruct((N_TOK, LANES), jnp.int32),
                   jax.ShapeDtypeStruct((N_TOK, LANES), F32)],
        compiler_params=_cparams(("arbitrary",)),
        name="merge_router",
    )(x2d, o_mla, o_gla, proj, proj, gm, shf, scf, nfw, woa, wob, wout, wrh, wrl, br)


MOE_PAIRS = MOE_ITEM_BLKS // 2
MOE_QUOTA = MOE_ITEM_ROWS // (MOE_NCHUNK * MOE_PAIRS)


def _moe_kernel(it_e_ref, it_start_ref, it_nblk_ref, src_ref, dst_ref,
                h2_hbm, w1_ref, b1_ref, w2_ref, b2_ref, sel_ref, y4_hbm,
                stage_ref, xb_ref, acc_ref, yout_ref, w1b_ref, w2b_ref, gsem, ssem):
    it = pl.program_id(0)
    c = pl.program_id(1)
    nblk = it_nblk_ref[it]
    prev = jnp.maximum(it - 1, 0)
    nblk_prev = jnp.where(it > 0, it_nblk_ref[prev], 0)
    live = nblk > 0
    prev_live = nblk_prev > 0
    rows_prev = nblk_prev * MOE_BLK
    rows_next = it_nblk_ref[it + 1] * MOE_BLK
    base_prev = it_start_ref[prev]
    base_next = it_start_ref[it + 1]

    def gather_row(r, base, nrows):
        tok = jnp.where(r < nrows, src_ref[jnp.minimum(base + r, MOE_SLOTS - 1)], 0)
        pltpu.make_async_copy(h2_hbm.at[pl.ds(tok, 1), :], stage_ref.at[pl.ds(r, 1), :], gsem).start()

    def scatter_row(r):
        d = dst_ref[jnp.minimum(base_prev + r, MOE_SLOTS - 1)]
        d = jnp.where(jnp.logical_and(r < rows_prev, d >= 0), d, N_ASSIGN + r)
        pltpu.make_async_copy(yout_ref.at[pl.ds(r, 1), :], y4_hbm.at[pl.ds(d, 1), :], ssem).start()

    def issue_quota(pair, gather=True):
        r0 = (c * MOE_PAIRS + pair) * MOE_QUOTA
        for u in range(MOE_QUOTA):
            if gather:
                gather_row(r0 + u, base_next, rows_next)
            scatter_row(r0 + u)

    @pl.when(jnp.logical_and(it == 0, c == 0))
    def _():
        yout_ref[...] = jnp.zeros(yout_ref.shape, F32)

        def first(r, carry):
            gather_row(r, it_start_ref[0], nblk * MOE_BLK)
            return carry
        lax.fori_loop(0, MOE_ITEM_ROWS, first, 0)

    @pl.when(jnp.logical_and(c == 0, jnp.logical_or(it == 0, prev_live)))
    def _():
        pltpu.make_async_copy(h2_hbm.at[pl.ds(0, MOE_ITEM_ROWS), :], stage_ref, gsem).wait()

    @pl.when(jnp.logical_and(c == 0, live))
    def _():
        for s in range(MOE_ITEM_BLKS):
            @pl.when(s < nblk)
            def _():
                rows = slice(s * MOE_BLK, (s + 1) * MOE_BLK)
                xb_ref[rows, :] = stage_ref[rows, :].astype(BF16)
                acc_ref[rows, :] = jnp.broadcast_to(b2_ref[0], (MOE_BLK, D_MODEL))

    def compute(row0, nrows):
        rows = slice(row0, row0 + nrows)
        hm = jnp.dot(xb_ref[rows, :], w1b_ref[...], preferred_element_type=F32) + b1_ref[0]
        lin = pltpu.roll(hm, 2 * MOE_TH - 1, 1)
        glu = jnp.minimum(hm, SWIGLU_LIMIT)
        lin = jnp.clip(lin, -SWIGLU_LIMIT, SWIGLU_LIMIT)
        act = glu * jax.nn.sigmoid(SWIGLU_ALPHA * glu) * (lin + 1.0)
        lane = lax.broadcasted_iota(jnp.int32, act.shape, 1)
        act = jnp.where(lane % 2 == 0, act, 0.0).astype(BF16)
        act = jnp.dot(act, sel_ref[...], preferred_element_type=F32).astype(BF16)
        acc_ref[rows, :] += jnp.dot(act, w2b_ref[...], preferred_element_type=F32)

    @pl.when(live)
    def _():
        w1b_ref[...] = w1_ref[0].astype(BF16)
        w2b_ref[...] = w2_ref[0].astype(BF16)

    for pair in range(MOE_PAIRS):
        in_pair = jnp.clip(nblk - 2 * pair, 0, 2)

        @pl.when(in_pair == 2)
        def _():
            compute(2 * pair * MOE_BLK, 2 * MOE_BLK)
            issue_quota(pair)

        @pl.when(in_pair == 1)
        def _():
            compute(2 * pair * MOE_BLK, MOE_BLK)
            issue_quota(pair)

        @pl.when(jnp.logical_and(in_pair == 0, live))
        def _():
            issue_quota(pair)

        @pl.when(jnp.logical_and(jnp.logical_not(live), prev_live))
        def _():
            issue_quota(pair, gather=False)

    @pl.when(jnp.logical_and(c == MOE_NCHUNK - 1, jnp.logical_or(live, prev_live)))
    def _():
        pltpu.make_async_copy(yout_ref, y4_hbm.at[pl.ds(0, MOE_ITEM_ROWS), :], ssem).wait()

    @pl.when(jnp.logical_and(c == MOE_NCHUNK - 1, live))
    def _():
        for s in range(MOE_ITEM_BLKS):
            @pl.when(s < nblk)
            def _():
                rows = slice(s * MOE_BLK, (s + 1) * MOE_BLK)
                yout_ref[rows, :] = acc_ref[rows, :]


def _moe(it_e, it_start, it_nblk, slot_src, slot_dst, h2, w1, b1, w2, b2):
    def chunk_of(it, c, n_ref):
        return jnp.where(n_ref[it] > 0, c, MOE_NCHUNK - 1)

    def w1_map(it, c, e_ref, s_ref, n_ref, *_):
        return (e_ref[it], 0, chunk_of(it, c, n_ref))

    def w2_map(it, c, e_ref, s_ref, n_ref, *_):
        return (e_ref[it], chunk_of(it, c, n_ref), 0)

    def b2_map(it, c, e_ref, *_):
        return (e_ref[it], 0, 0)

    sel = (jnp.arange(2 * MOE_TH, dtype=jnp.int32)[:, None]
           == 2 * jnp.arange(MOE_TH, dtype=jnp.int32)[None, :]).astype(BF16)

    grid_spec = pltpu.PrefetchScalarGridSpec(
        num_scalar_prefetch=5,
        grid=(MOE_NITEMS + 1, MOE_NCHUNK),
        in_specs=[pl.BlockSpec(memory_space=pl.ANY),
                  pl.BlockSpec((1, D_MODEL, 2 * MOE_TH), w1_map),
                  pl.BlockSpec((1, 1, 2 * MOE_TH), w1_map),
                  pl.BlockSpec((1, MOE_TH, D_MODEL), w2_map),
                  pl.BlockSpec((1, 1, D_MODEL), b2_map),
                  pl.BlockSpec((2 * MOE_TH, MOE_TH), lambda it, c, *_: (0, 0))],
        out_specs=pl.BlockSpec(memory_space=pl.ANY),
        scratch_shapes=[pltpu.VMEM((MOE_ITEM_ROWS, D_MODEL), F32),
                        pltpu.VMEM((MOE_ITEM_ROWS, D_MODEL), BF16),
                        pltpu.VMEM((MOE_ITEM_ROWS, D_MODEL), F32),
                        pltpu.VMEM((MOE_ITEM_ROWS, D_MODEL), F32),
                        pltpu.VMEM((D_MODEL, 2 * MOE_TH), BF16),
                        pltpu.VMEM((MOE_TH, D_MODEL), BF16),
                        pltpu.SemaphoreType.DMA(()),
                        pltpu.SemaphoreType.DMA(())],
    )
    return pl.pallas_call(
        _moe_kernel,
        grid_spec=grid_spec,
        out_shape=jax.ShapeDtypeStruct((N_ASSIGN + MOE_ITEM_ROWS, D_MODEL), F32),
        compiler_params=_cparams(("arbitrary", "arbitrary")),
        name="moe_experts",
    )(it_e, it_start, it_nblk, slot_src, slot_dst, h2, w1,
      b1.reshape(N_EXPERTS, 1, 2 * D_EXPERT), w2, b2.reshape(N_EXPERTS, 1, D_MODEL), sel)


def _route_tables(top_idx):
    e_flat = top_idx.reshape(N_ASSIGN)
    onehot = (e_flat[:, None] == jnp.arange(N_EXPERTS, dtype=jnp.int32)[None, :]).astype(jnp.int32)
    csum = jnp.cumsum(onehot, axis=0)
    rank = jnp.sum(csum * onehot, axis=-1) - 1
    counts = csum[-1]
    nblk = (counts + MOE_BLK - 1) // MOE_BLK
    blk_end = jnp.cumsum(nblk)
    blk_start = blk_end - nblk
    dest = blk_start[e_flat] * MOE_BLK + rank
    slot_a = jnp.full((MOE_SLOTS,), -1, jnp.int32).at[dest].set(jnp.arange(N_ASSIGN, dtype=jnp.int32))
    slot_src = jnp.where(slot_a >= 0, slot_a // TOP_K, 0)
    slot_dst = jnp.where(slot_a >= 0, (slot_a % TOP_K) * N_TOK + slot_a // TOP_K, -1)

    nit = (nblk + MOE_ITEM_BLKS - 1) // MOE_ITEM_BLKS
    it_end = jnp.cumsum(nit)
    it_begin = it_end - nit
    n_used = it_end[-1]
    ids = jnp.arange(MOE_NITEMS + 2, dtype=jnp.int32)
    valid = ids < n_used
    last = jnp.minimum(ids, n_used - 1)
    e_of = jnp.minimum(jnp.sum((it_end[None, :] <= last[:, None]).astype(jnp.int32), axis=1), N_EXPERTS - 1)
    local = ids - it_begin[e_of]
    it_start = jnp.where(valid, (blk_start[e_of] + local * MOE_ITEM_BLKS) * MOE_BLK, 0).astype(jnp.int32)
    it_nblk = jnp.where(valid, jnp.clip(nblk[e_of] - local * MOE_ITEM_BLKS, 0, MOE_ITEM_BLKS), 0)
    return e_of, it_start, it_nblk.astype(jnp.int32), slot_src, slot_dst


FIN_TM = 256


def _final_kernel(x1_ref, ya_ref, yb_ref, yc_ref, yd_ref, tw_ref, gf_ref, sho_ref, sco_ref, nw_ref, o_ref):
    tw = tw_ref[...]
    moe = tw[:, 0:1] * ya_ref[...]
    for kk, y_ref in enumerate((yb_ref, yc_ref, yd_ref), start=1):
        moe = moe + tw[:, kk:kk + 1] * y_ref[...]
    x2 = x1_ref[...] + gf_ref[0] * moe
    o_ref[...] = _rms(x2, nw_ref[...]) * (1.0 + sco_ref[0]) + sho_ref[0]


def _final(x1, y4, tw, gf, sho, sco, nw):
    tm = FIN_TM
    per_b = SEQ // tm
    row = lambda i: (i, 0)
    bvec = lambda i: (i // per_b, 0, 0)
    nb = N_TOK // tm
    kth = lambda kk: pl.BlockSpec((tm, D_MODEL), lambda i: (kk * nb + i, 0))
    return pl.pallas_call(
        _final_kernel,
        grid=(nb,),
        in_specs=[pl.BlockSpec((tm, D_MODEL), row),
                  kth(0), kth(1), kth(2), kth(3),
                  pl.BlockSpec((tm, LANES), row),
                  pl.BlockSpec((1, 1, D_MODEL), bvec),
                  pl.BlockSpec((1, 1, D_MODEL), bvec),
                  pl.BlockSpec((1, 1, D_MODEL), bvec),
                  pl.BlockSpec((1, D_MODEL), lambda i: (0, 0))],
        out_specs=pl.BlockSpec((tm, D_MODEL), row),
        out_shape=jax.ShapeDtypeStruct((N_TOK, D_MODEL), F32),
        compiler_params=_cparams(("arbitrary",)),
        name="combine_final",
    )(x1, y4, y4, y4, y4, tw, gf, sho, sco, nw)


def _deinterleave(n):
    return np.concatenate([np.arange(0, n, 2), np.arange(1, n, 2)])


def _pack_w_in(w_in):
    o = np.cumsum([0, MLA_Q_RANK, MLA_KV_RANK + MLA_ROPE, GLA_HEADS * GLA_DK, GLA_HEADS * GLA_DK,
                   GLA_HEADS * GLA_DV, GLA_GATE_RANK, GLA_HEADS * GLA_DV, D_MODEL, D_MODEL])
    seg = lambda i: w_in[:, o[i]:o[i + 1]]
    kv = seg(1)
    zeros = lambda n: jnp.zeros((D_MODEL, n), w_in.dtype)
    kpe = kv[:, MLA_KV_RANK:][:, _deinterleave(MLA_ROPE)]
    parts = [seg(0), kv[:, :MLA_KV_RANK], kpe, zeros(LANES - MLA_ROPE),
             seg(5), zeros(LANES - GLA_GATE_RANK), seg(2), seg(3), seg(4), seg(6), seg(7), seg(8)]
    return jnp.concatenate(parts, axis=1).astype(BF16)


def _pack_w_q_b(w):
    w = w.reshape(MLA_Q_RANK, MLA_HEADS, MLA_NOPE + MLA_ROPE)
    nope = w[:, :, :MLA_NOPE]
    pe = w[:, :, MLA_NOPE:][:, :, _deinterleave(MLA_ROPE)]
    pad = jnp.zeros((MLA_Q_RANK, MLA_HEADS, MLA_HD - MLA_NOPE - MLA_ROPE), w.dtype)
    return jnp.concatenate([nope, pe, pad], axis=-1).reshape(MLA_Q_RANK, MLA_HEADS * MLA_HD).astype(BF16)


def _rope_consts():
    inv_freq = ROPE_THETA ** (-(jnp.arange(0, MLA_ROPE, 2, dtype=F32) / MLA_ROPE))
    half = MLA_ROPE // 2
    z = jnp.zeros((LANES - MLA_ROPE,), F32)
    rows = [jnp.concatenate([inv_freq, inv_freq, z]),
            jnp.concatenate([-jnp.ones((half,), F32), jnp.zeros((half,), F32), z]),
            jnp.concatenate([jnp.zeros((half,), F32), jnp.ones((half,), F32), z])]
    return jnp.concatenate([jnp.stack(rows), jnp.zeros((5, LANES), F32)], axis=0)


def kernel(x, c, positions, w_ada, b_ada, norm_mix_w, w_in, mla_q_norm_w, mla_w_q_b, mla_kv_norm_w,
           mla_w_kv_b, gla_w_gk_up, gla_b_gk_up, gla_norm_w, w_o_mla, w_o_gla, w_out, norm_ffn_w,
           w_router, b_router, w1, b1, w2, b2, w_ada_final, b_ada_final, norm_final_w):
    assert w_ada.shape[0] == 1, "single-layer stack"
    x2d = x.reshape(N_TOK, D_MODEL)
    c_t = c.T

    mod = _adaln(c_t, w_ada[0], b_ada[0])
    fmod = _adaln(c_t, w_ada_final, b_ada_final)
    sh_m, sc_m, g_m, sh_f, sc_f, g_f = [m.reshape(BATCH, 1, D_MODEL) for m in jnp.split(mod, N_MOD, axis=-1)]
    sh_o, sc_o = [m.reshape(BATCH, 1, D_MODEL) for m in jnp.split(fmod, 2, axis=-1)]

    proj = _inproj(x2d, norm_mix_w[0].reshape(1, D_MODEL), sh_m, sc_m, _pack_w_in(w_in[0]))

    q, k, v = _mla_prep(proj, positions.reshape(N_TOK, 1), _rope_consts(),
                        mla_q_norm_w[0].reshape(1, MLA_Q_RANK), mla_kv_norm_w[0].reshape(1, MLA_KV_RANK),
                        _pack_w_q_b(mla_w_q_b[0]), mla_w_kv_b[0].astype(BF16))
    o_mla = _attention(q, k, v)

    wgk = jnp.concatenate([gla_w_gk_up[0], jnp.zeros((LANES - GLA_GATE_RANK, GLA_HEADS * GLA_DK), F32)], axis=0)
    o_gla = _gla(proj, wgk, gla_b_gk_up[0].reshape(1, GLA_HEADS * GLA_DK), gla_norm_w[0].reshape(1, GLA_DV))

    wr = jnp.concatenate([w_router[0], jnp.zeros((D_MODEL, LANES - N_EXPERTS), F32)], axis=1)
    wr_hi = wr.astype(BF16)
    wr_lo = (wr - wr_hi.astype(F32)).astype(BF16)
    br = jnp.concatenate([b_router[0], jnp.zeros((LANES - N_EXPERTS,), F32)]).reshape(1, LANES)
    x1, h2, top_idx, top_w = _merge(
        x2d, o_mla, o_gla, proj, g_m, sh_f, sc_f, norm_ffn_w[0].reshape(1, D_MODEL),
        w_o_mla[0].astype(BF16), w_o_gla[0].astype(BF16), w_out[0].astype(BF16), wr_hi, wr_lo, br)

    it_e, it_start, it_nblk, slot_src, slot_dst = _route_tables(top_idx[:, :TOP_K])
    y4 = _moe(it_e, it_start, it_nblk, slot_src, slot_dst, h2, w1[0], b1[0], w2[0], b2[0])

    out = _final(x1, y4, top_w, g_f, sh_o, sc_o, norm_final_w.reshape(1, D_MODEL))
    return out.reshape(BATCH, SEQ, D_MODEL)
```

```python
import functools

import jax
import jax.numpy as jnp
import numpy as np
from jax import lax
from jax.experimental import pallas as pl
from jax.experimental.pallas import tpu as pltpu

F32 = jnp.float32
BF16 = jnp.bfloat16

D_MODEL = 2048
BATCH = 2
SEQ = 4096
N_TOK = BATCH * SEQ

MLA_HEADS = 8
MLA_Q_RANK = 512
MLA_KV_RANK = 256
MLA_NOPE = 128
MLA_ROPE = 64
MLA_V = 128
ROPE_THETA = 10000.0
MLA_HD = 256

GLA_HEADS = 4
GLA_DK = 128
GLA_DV = 256
GLA_GATE_RANK = 16
GLA_GATE_NORM = 16.0
GLA_CHUNK = 64

N_EXPERTS = 32
TOP_K = 4
D_EXPERT = D_MODEL
SWIGLU_LIMIT = 7.0
SWIGLU_ALPHA = 1.702
EPS = 1e-6
N_MOD = 6

LANES = 128
ROW_TILES = D_MODEL // LANES
VMEM_LIMIT = 56 * 1024 * 1024

P_QLAT = 0
P_CKV = 512
P_KPE = 768
P_GLR = 896
P_GQ = 1024
P_GK = 1536
P_GV = 2048
P_GOUT = 3072
P_GATEA = 4096
P_GATEB = 6144
P_TOTAL = 8192

MOE_BLK = 256
MOE_ITEM_BLKS = 4
MOE_ITEM_ROWS = MOE_BLK * MOE_ITEM_BLKS
MOE_TH = 256
MOE_NCHUNK = D_EXPERT // MOE_TH
N_ASSIGN = N_TOK * TOP_K
MOE_NBLK_MAX = N_ASSIGN // MOE_BLK + N_EXPERTS
MOE_SLOTS = MOE_NBLK_MAX * MOE_BLK
MOE_NITEMS = N_ASSIGN // MOE_ITEM_ROWS + N_EXPERTS


def _cparams(sem, vmem=VMEM_LIMIT):
    return pltpu.CompilerParams(dimension_semantics=sem, vmem_limit_bytes=vmem)


def _rms(x, w):
    return x * lax.rsqrt(jnp.mean(x * x, axis=-1, keepdims=True) + EPS) * w


def _adaln_kernel(ct_ref, w_ref, b_ref, o_ref):
    ct = ct_ref[...]
    cond = ct * jax.nn.sigmoid(ct)
    w = w_ref[...]
    for b in range(BATCH):
        o_ref[b:b + 1, :] = jnp.sum(w * cond[:, b:b + 1], axis=0, keepdims=True) + b_ref[...]


def _adaln(c_t, w, b, tn=1024):
    d, n = w.shape
    return pl.pallas_call(
        _adaln_kernel,
        grid=(n // tn,),
        in_specs=[pl.BlockSpec((d, BATCH), lambda j: (0, 0)),
                  pl.BlockSpec((d, tn), lambda j: (0, j)),
                  pl.BlockSpec((1, tn), lambda j: (0, j))],
        out_specs=pl.BlockSpec((BATCH, tn), lambda j: (0, j)),
        out_shape=jax.ShapeDtypeStruct((BATCH, n), F32),
        compiler_params=_cparams(("arbitrary",)),
        name="adaln",
    )(c_t, w, b.reshape(1, n))


IN_TM = 1024
IN_TN = 1024


def _inproj_kernel(x_ref, nw_ref, sh_ref, sc_ref, w_ref, o_ref, h_ref):
    @pl.when(pl.program_id(1) == 0)
    def _():
        for r in range(0, IN_TM, 256):
            x = x_ref[r:r + 256, :]
            h = _rms(x, nw_ref[...]) * (1.0 + sc_ref[0]) + sh_ref[0]
            h_ref[r:r + 256, :] = h.astype(BF16)

    o_ref[...] = jnp.dot(h_ref[...], w_ref[...], preferred_element_type=F32).astype(BF16)


def _inproj(x2d, nw, sh, sc, w_packed):
    per_b = SEQ // IN_TM
    return pl.pallas_call(
        _inproj_kernel,
        grid=(N_TOK // IN_TM, P_TOTAL // IN_TN),
        in_specs=[pl.BlockSpec((IN_TM, D_MODEL), lambda i, j: (i, 0)),
                  pl.BlockSpec((1, D_MODEL), lambda i, j: (0, 0)),
                  pl.BlockSpec((1, 1, D_MODEL), lambda i, j: (i // per_b, 0, 0)),
                  pl.BlockSpec((1, 1, D_MODEL), lambda i, j: (i // per_b, 0, 0)),
                  pl.BlockSpec((D_MODEL, IN_TN), lambda i, j: (0, j))],
        out_specs=pl.BlockSpec((IN_TM, IN_TN), lambda i, j: (i, j)),
        out_shape=jax.ShapeDtypeStruct((N_TOK, P_TOTAL), BF16),
        scratch_shapes=[pltpu.VMEM((IN_TM, D_MODEL), BF16)],
        compiler_params=_cparams(("arbitrary", "arbitrary")),
        name="inproj",
    )(x2d, nw, sh, sc, w_packed)


PREP_TM = 512


def _mla_prep_kernel(ql_ref, ckv_ref, kpe_ref, pos_ref, rc_ref, qnw_ref, kvnw_ref, wq_ref, wkv_ref,
                     q_ref, k_ref, v_ref):
    scale = (MLA_NOPE + MLA_ROPE) ** -0.5
    qn = _rms(ql_ref[...].astype(F32), qnw_ref[...]).astype(BF16)
    q = jnp.dot(qn, wq_ref[...], preferred_element_type=F32)
    cn = _rms(ckv_ref[...].astype(F32), kvnw_ref[...]).astype(BF16)
    kv = jnp.dot(cn, wkv_ref[...], preferred_element_type=F32)

    ang = pos_ref[...].astype(F32) * rc_ref[0:1, :]
    cos_t = jnp.cos(ang)
    sin_t = jnp.sin(ang)
    sin_a = sin_t * rc_ref[1:2, :]
    sin_b = sin_t * rc_ref[2:3, :]

    def rope(t):
        return (t * cos_t + pltpu.roll(t, LANES - MLA_ROPE // 2, 1) * sin_a
                + pltpu.roll(t, MLA_ROPE // 2, 1) * sin_b)

    kpe = rope(kpe_ref[...].astype(F32)).astype(BF16)
    for h in range(MLA_HEADS):
        c0 = h * MLA_HD
        q_ref[0, h, :, 0:LANES] = (q[:, c0:c0 + LANES] * scale).astype(BF16)
        q_ref[0, h, :, LANES:MLA_HD] = (rope(q[:, c0 + LANES:c0 + MLA_HD]) * scale).astype(BF16)
        k_ref[0, h, :, 0:LANES] = kv[:, c0:c0 + LANES].astype(BF16)
        k_ref[0, h, :, LANES:MLA_HD] = kpe
        v_ref[0, h, :, :] = kv[:, c0 + LANES:c0 + MLA_HD].astype(BF16)


def _mla_prep(proj, pos_col, rope_consts, qnw, kvnw, wq, wkv):
    tm = PREP_TM
    per_b = SEQ // tm
    row = lambda i: (i, 0)
    const = lambda i: (0, 0)
    out_map = lambda i: (i // per_b, 0, i % per_b, 0)
    return pl.pallas_call(
        _mla_prep_kernel,
        grid=(N_TOK // tm,),
        in_specs=[pl.BlockSpec((tm, MLA_Q_RANK), lambda i: (i, P_QLAT // MLA_Q_RANK)),
                  pl.BlockSpec((tm, MLA_KV_RANK), lambda i: (i, P_CKV // MLA_KV_RANK)),
                  pl.BlockSpec((tm, LANES), lambda i: (i, P_KPE // LANES)),
                  pl.BlockSpec((tm, 1), row),
                  pl.BlockSpec((8, LANES), const),
                  pl.BlockSpec((1, MLA_Q_RANK), const),
                  pl.BlockSpec((1, MLA_KV_RANK), const),
                  pl.BlockSpec((MLA_Q_RANK, MLA_HEADS * MLA_HD), const),
                  pl.BlockSpec((MLA_KV_RANK, MLA_HEADS * MLA_HD), const)],
        out_specs=[pl.BlockSpec((1, MLA_HEADS, tm, MLA_HD), out_map),
                   pl.BlockSpec((1, MLA_HEADS, tm, MLA_HD), out_map),
                   pl.BlockSpec((1, MLA_HEADS, tm, MLA_V), out_map)],
        out_shape=[jax.ShapeDtypeStruct((BATCH, MLA_HEADS, SEQ, MLA_HD), BF16),
                   jax.ShapeDtypeStruct((BATCH, MLA_HEADS, SEQ, MLA_HD), BF16),
                   jax.ShapeDtypeStruct((BATCH, MLA_HEADS, SEQ, MLA_V), BF16)],
        compiler_params=_cparams(("arbitrary",)),
        name="mla_prep",
    )(proj, proj, proj, pos_col, rope_consts, qnw, kvnw, wq, wkv)


ATT_T = 512


def _attn_kernel(q_ref, k_ref, v_ref, o_ref, m_ref, l_ref, acc_ref):
    qi = pl.program_id(2)
    q = q_ref[0, 0]
    m_ref[...] = jnp.full(m_ref.shape, -jnp.inf, F32)
    l_ref[...] = jnp.zeros(l_ref.shape, F32)
    acc_ref[...] = jnp.zeros(acc_ref.shape, F32)

    def step(j, masked):
        r0 = pl.multiple_of(j * ATT_T, ATT_T)
        k = k_ref[0, 0, pl.ds(r0, ATT_T), :]
        v = v_ref[0, 0, pl.ds(r0, ATT_T), :]
        s = lax.dot_general(k, q, (((1,), (1,)), ((), ())), preferred_element_type=F32)
        if masked:
            key = lax.broadcasted_iota(jnp.int32, s.shape, 0)
            qry = lax.broadcasted_iota(jnp.int32, s.shape, 1)
            s = jnp.where(key <= qry, s, -jnp.inf)
        m_old = m_ref[...]
        m_new = jnp.maximum(m_old, jnp.max(s, axis=0, keepdims=True))
        p = jnp.exp(s - m_new)
        alpha = jnp.exp(m_old - m_new)
        l_ref[...] = alpha * l_ref[...] + jnp.sum(p, axis=0, keepdims=True)
        pv = lax.dot_general(v, p.astype(BF16), (((0,), (0,)), ((), ())), preferred_element_type=F32)
        acc_ref[...] = alpha * acc_ref[...] + pv
        m_ref[...] = m_new

    def body(j, carry):
        step(j, False)
        return carry

    lax.fori_loop(0, qi, body, 0)
    step(qi, True)
    o_ref[...] = jnp.transpose(acc_ref[...] / l_ref[...]).astype(BF16)


def _attention(q, k, v):
    t = ATT_T
    nq = SEQ // t
    return pl.pallas_call(
        _attn_kernel,
        grid=(BATCH, MLA_HEADS, nq),
        in_specs=[pl.BlockSpec((1, 1, t, MLA_HD), lambda b, h, i: (b, h, i, 0)),
                  pl.BlockSpec((1, 1, SEQ, MLA_HD), lambda b, h, i: (b, h, 0, 0)),
                  pl.BlockSpec((1, 1, SEQ, MLA_V), lambda b, h, i: (b, h, 0, 0))],
        out_specs=pl.BlockSpec((t, MLA_V), lambda b, h, i: (b * nq + i, h)),
        out_shape=jax.ShapeDtypeStruct((N_TOK, MLA_HEADS * MLA_V), BF16),
        scratch_shapes=[pltpu.VMEM((1, t), F32), pltpu.VMEM((1, t), F32), pltpu.VMEM((MLA_V, t), F32)],
        compiler_params=_cparams(("arbitrary", "arbitrary", "arbitrary")),
        name="mla_attn",
    )(q, k, v)


GLA_TG = 512
GLA_DIAG = 8


def _gla_kernel(q_ref, k_ref, v_ref, glr_ref, gout_ref, wgk_ref, bgk_ref, nw_ref, o_ref, st_ref, g_ref):
    C = GLA_CHUNK

    @pl.when(pl.program_id(2) == 0)
    def _():
        st_ref[...] = jnp.zeros(st_ref.shape, F32)

    z = jnp.dot(glr_ref[...].astype(F32), wgk_ref[...], preferred_element_type=F32,
                precision=lax.Precision.HIGHEST) + bgk_ref[...]
    g_ref[...] = jax.nn.log_sigmoid(z) / GLA_GATE_NORM

    row = lax.broadcasted_iota(jnp.int32, (C, GLA_DK), 0)
    ii = lax.broadcasted_iota(jnp.int32, (C, C), 0)
    jj = lax.broadcasted_iota(jnp.int32, (C, C), 1)
    nt = (((1,), (1,)), ((), ()))

    def chunk(ci, carry):
        r0 = pl.multiple_of(ci * C, C)
        b = g_ref[pl.ds(r0, C), :]
        s = 1
        while s < C:
            b = b + jnp.where(row >= s, pltpu.roll(b, s, 0), 0.0)
            s *= 2
        q = q_ref[pl.ds(r0, C), :].astype(F32) * (GLA_DK ** -0.5)
        k = k_ref[pl.ds(r0, C), :].astype(F32)
        v = v_ref[pl.ds(r0, C), :]
        st = st_ref[...]

        o = lax.dot_general((q * jnp.exp(b)).astype(BF16), st.astype(BF16), nt,
                            preferred_element_type=F32)

        a = jnp.zeros((C, C), F32)
        lvl = C // 2
        while lvl >= GLA_DIAG:
            pieces = [jnp.broadcast_to(b[t + lvl:t + lvl + 1, :], (2 * lvl, GLA_DK))
                      for t in range(0, C, 2 * lvl)]
            m = pieces[0] if len(pieces) == 1 else jnp.concatenate(pieces, axis=0)
            odd = (row // lvl) % 2 == 1
            ql = jnp.where(odd, q * jnp.exp(jnp.minimum(b - m, 0.0)), 0.0).astype(BF16)
            kl = jnp.where(odd, 0.0, k * jnp.exp(jnp.minimum(m - b, 0.0))).astype(BF16)
            al = lax.dot_general(ql, kl, nt, preferred_element_type=F32)
            a = a + jnp.where(ii // (2 * lvl) == jj // (2 * lvl), al, 0.0)
            lvl //= 2
        for d in range(GLA_DIAG):
            if d == 0:
                t = q * k
            else:
                t = q * pltpu.roll(k, d, 0) * jnp.exp(jnp.minimum(b - pltpu.roll(b, d, 0), 0.0))
            col = jnp.sum(t, axis=-1, keepdims=True)
            a = a + jnp.where((ii - jj == d) & (ii % GLA_DIAG >= d), col, 0.0)
        o = o + jnp.dot(a.astype(BF16), v, preferred_element_type=F32)

        b_last = b[C - 1:C, :]
        kdec = (k * jnp.exp(b_last - b)).astype(BF16)
        st_ref[...] = st * jnp.exp(b_last) + lax.dot_general(
            v, kdec, (((0,), (0,)), ((), ())), preferred_element_type=F32)

        gate = gout_ref[pl.ds(r0, C), :].astype(F32)
        y = _rms(o, nw_ref[...]) * (gate * jax.nn.sigmoid(gate))
        o_ref[pl.ds(r0, C), :] = y.astype(BF16)
        return carry

    lax.fori_loop(0, GLA_TG // C, chunk, 0)


def _gla(proj, wgk, bgk, nw):
    tg = GLA_TG
    per_b = SEQ // tg
    rb = lambda b, h, t: b * per_b + t
    return pl.pallas_call(
        _gla_kernel,
        grid=(BATCH, GLA_HEADS, per_b),
        in_specs=[pl.BlockSpec((tg, GLA_DK), lambda b, h, t: (rb(b, h, t), P_GQ // GLA_DK + h)),
                  pl.BlockSpec((tg, GLA_DK), lambda b, h, t: (rb(b, h, t), P_GK // GLA_DK + h)),
                  pl.BlockSpec((tg, GLA_DV), lambda b, h, t: (rb(b, h, t), P_GV // GLA_DV + h)),
                  pl.BlockSpec((tg, LANES), lambda b, h, t: (rb(b, h, t), P_GLR // LANES)),
                  pl.BlockSpec((tg, GLA_DV), lambda b, h, t: (rb(b, h, t), P_GOUT // GLA_DV + h)),
                  pl.BlockSpec((LANES, GLA_DK), lambda b, h, t: (0, h)),
                  pl.BlockSpec((1, GLA_DK), lambda b, h, t: (0, h)),
                  pl.BlockSpec((1, GLA_DV), lambda b, h, t: (0, 0))],
        out_specs=pl.BlockSpec((tg, GLA_DV), lambda b, h, t: (rb(b, h, t), h)),
        out_shape=jax.ShapeDtypeStruct((N_TOK, GLA_HEADS * GLA_DV), BF16),
        scratch_shapes=[pltpu.VMEM((GLA_DV, GLA_DK), F32), pltpu.VMEM((tg, GLA_DK), F32)],
        compiler_params=_cparams(("arbitrary", "arbitrary", "arbitrary")),
        name="gla",
    )(proj, proj, proj, proj, proj, wgk, bgk, nw)


MRG_TM = 256


def _merge_kernel(x_ref, oa_ref, ob_ref, ga_ref, gb_ref, gm_ref, shf_ref, scf_ref, nfw_ref,
                  woa_ref, wob_ref, wout_ref, wrh_ref, wrl_ref, br_ref,
                  x1_ref, h2_ref, idx_ref, tw_ref):
    ya = jnp.dot(oa_ref[...], woa_ref[...], preferred_element_type=F32)
    yb = jnp.dot(ob_ref[...], wob_ref[...], preferred_element_type=F32)
    merged = (jax.nn.sigmoid(ga_ref[...].astype(F32)) * ya
              + jax.nn.sigmoid(gb_ref[...].astype(F32)) * yb).astype(BF16)
    x1 = x_ref[...] + gm_ref[0] * jnp.dot(merged, wout_ref[...], preferred_element_type=F32)
    x1_ref[...] = x1
    h2 = _rms(x1, nfw_ref[...]) * (1.0 + scf_ref[0]) + shf_ref[0]
    for j in range(ROW_TILES):
        h2_ref[pl.ds(j, MRG_TM, stride=ROW_TILES), :] = h2[:, j * LANES:(j + 1) * LANES]

    hi = h2.astype(BF16)
    lo = (h2 - hi.astype(F32)).astype(BF16)
    logits = (jnp.dot(hi, wrh_ref[...], preferred_element_type=F32)
              + jnp.dot(hi, wrl_ref[...], preferred_element_type=F32)
              + jnp.dot(lo, wrh_ref[...], preferred_element_type=F32)) + br_ref[...]
    lane = lax.broadcasted_iota(jnp.int32, logits.shape, 1)
    vals = jnp.where(lane < N_EXPERTS, logits, -jnp.inf)
    idx_out = jnp.zeros(logits.shape, jnp.int32)
    w_out = jnp.zeros(logits.shape, F32)
    top = None
    denom = None
    for kk in range(TOP_K):
        m = jnp.max(vals, axis=-1, keepdims=True)
        sel = jnp.min(jnp.where(vals == m, lane, LANES), axis=-1, keepdims=True)
        if kk == 0:
            top = m
        e = jnp.exp(m - top)
        denom = e if kk == 0 else denom + e
        idx_out = jnp.where(lane == kk, sel, idx_out)
        w_out = jnp.where(lane == kk, e, w_out)
        vals = jnp.where(lane == sel, -jnp.inf, vals)
    idx_ref[...] = idx_out
    tw_ref[...] = w_out / denom


def _merge(x2d, o_mla, o_gla, proj, gm, shf, scf, nfw, woa, wob, wout, wrh, wrl, br):
    tm = MRG_TM
    per_b = SEQ // tm
    row = lambda i: (i, 0)
    const = lambda i: (0, 0)
    bvec = lambda i: (i // per_b, 0, 0)
    single = dict(pipeline_mode=pl.Buffered(1))
    return pl.pallas_call(
        _merge_kernel,
        grid=(N_TOK // tm,),
        in_specs=[pl.BlockSpec((tm, D_MODEL), row),
                  pl.BlockSpec((tm, MLA_HEADS * MLA_V), row),
                  pl.BlockSpec((tm, GLA_HEADS * GLA_DV), row),
                  pl.BlockSpec((tm, D_MODEL), lambda i: (i, P_GATEA // D_MODEL)),
                  pl.BlockSpec((tm, D_MODEL), lambda i: (i, P_GATEB // D_MODEL)),
                  pl.BlockSpec((1, 1, D_MODEL), bvec),
                  pl.BlockSpec((1, 1, D_MODEL), bvec),
                  pl.BlockSpec((1, 1, D_MODEL), bvec),
                  pl.BlockSpec((1, D_MODEL), const),
                  pl.BlockSpec((MLA_HEADS * MLA_V, D_MODEL), const, **single),
                  pl.BlockSpec((GLA_HEADS * GLA_DV, D_MODEL), const, **single),
                  pl.BlockSpec((D_MODEL, D_MODEL), const, **single),
                  pl.BlockSpec((D_MODEL, LANES), const, **single),
                  pl.BlockSpec((D_MODEL, LANES), const, **single),
                  pl.BlockSpec((1, LANES), const)],
        out_specs=[pl.BlockSpec((tm, D_MODEL), row),
                   pl.BlockSpec((tm * ROW_TILES, LANES), row),
                   pl.BlockSpec((tm, LANES), row),
                   pl.BlockSpec((tm, LANES), row)],
        out_shape=[jax.ShapeDtypeStruct((N_TOK, D_MODEL), F32),
                   jax.ShapeDtypeStruct((N_TOK * ROW_TILES, LANES), F32),
                   jax.ShapeDtypeStruct((N_TOK, LANES), jnp.int32),
                   jax.ShapeDtypeStruct((N_TOK, LANES), F32)],
        compiler_params=_cparams(("arbitrary",)),
        name="merge_router",
    )(x2d, o_mla, o_gla, proj, proj, gm, shf, scf, nfw, woa, wob, wout, wrh, wrl, br)


MOE_PAIRS = MOE_ITEM_BLKS // 2
MOE_QUOTA = MOE_ITEM_ROWS // (MOE_NCHUNK * MOE_PAIRS)


def _moe_kernel(it_e_ref, it_start_ref, it_nblk_ref, src_ref, dst_ref,
                h2_hbm, w1_ref, b1_ref, w2_ref, b2_ref, sel_ref, y4_hbm,
                stage_ref, xb_ref, acc_ref, yout_ref, w1b_ref, w2b_ref, gsem, ssem):
    it = pl.program_id(0)
    c = pl.program_id(1)
    nblk = it_nblk_ref[it]
    prev = jnp.maximum(it - 1, 0)
    nblk_prev = jnp.where(it > 0, it_nblk_ref[prev], 0)
    live = nblk > 0
    prev_live = nblk_prev > 0
    rows_prev = nblk_prev * MOE_BLK
    rows_next = it_nblk_ref[it + 1] * MOE_BLK
    base_prev = it_start_ref[prev]
    base_next = it_start_ref[it + 1]

    def token_rows(t):
        return pl.ds(pl.multiple_of(t * ROW_TILES, ROW_TILES), ROW_TILES)

    def gather_row(r, base, nrows):
        tok = jnp.where(r < nrows, src_ref[jnp.minimum(base + r, MOE_SLOTS - 1)], 0)
        pltpu.make_async_copy(h2_hbm.at[token_rows(tok), :], stage_ref.at[token_rows(r), :], gsem).start()

    def scatter_row(r):
        d = dst_ref[jnp.minimum(base_prev + r, MOE_SLOTS - 1)]
        d = jnp.where(jnp.logical_and(r < rows_prev, d >= 0), d, N_ASSIGN + r)
        pltpu.make_async_copy(yout_ref.at[token_rows(r), :], y4_hbm.at[token_rows(d), :], ssem).start()

    def issue_quota(pair, gather=True):
        r0 = (c * MOE_PAIRS + pair) * MOE_QUOTA
        for u in range(MOE_QUOTA):
            if gather:
                gather_row(r0 + u, base_next, rows_next)
            scatter_row(r0 + u)

    @pl.when(jnp.logical_and(it == 0, c == 0))
    def _():
        yout_ref[...] = jnp.zeros(yout_ref.shape, F32)

        def first(r, carry):
            gather_row(r, it_start_ref[0], nblk * MOE_BLK)
            return carry
        lax.fori_loop(0, MOE_ITEM_ROWS, first, 0)

    @pl.when(jnp.logical_and(c == 0, jnp.logical_or(it == 0, prev_live)))
    def _():
        pltpu.make_async_copy(h2_hbm.at[pl.ds(0, MOE_ITEM_ROWS * ROW_TILES), :], stage_ref, gsem).wait()

    @pl.when(jnp.logical_and(c == 0, live))
    def _():
        for s in range(MOE_ITEM_BLKS):
            @pl.when(s < nblk)
            def _():
                rows = slice(s * MOE_BLK, (s + 1) * MOE_BLK)
                for j in range(ROW_TILES):
                    piece = stage_ref[pl.ds(s * MOE_BLK * ROW_TILES + j, MOE_BLK, stride=ROW_TILES), :]
                    xb_ref[rows, j * LANES:(j + 1) * LANES] = piece.astype(BF16)
                acc_ref[rows, :] = jnp.broadcast_to(b2_ref[0], (MOE_BLK, D_MODEL))

    def compute(row0, nrows):
        rows = slice(row0, row0 + nrows)
        hm = jnp.dot(xb_ref[rows, :], w1b_ref[...], preferred_element_type=F32) + b1_ref[0]
        lin = pltpu.roll(hm, 2 * MOE_TH - 1, 1)
        glu = jnp.minimum(hm, SWIGLU_LIMIT)
        lin = jnp.clip(lin, -SWIGLU_LIMIT, SWIGLU_LIMIT)
        act = glu * jax.nn.sigmoid(SWIGLU_ALPHA * glu) * (lin + 1.0)
        lane = lax.broadcasted_iota(jnp.int32, act.shape, 1)
        act = jnp.where(lane % 2 == 0, act, 0.0).astype(BF16)
        act = jnp.dot(act, sel_ref[...], preferred_element_type=F32).astype(BF16)
        acc_ref[rows, :] += jnp.dot(act, w2b_ref[...], preferred_element_type=F32)

    @pl.when(live)
    def _():
        w1b_ref[...] = w1_ref[0].astype(BF16)
        w2b_ref[...] = w2_ref[0].astype(BF16)

    for pair in range(MOE_PAIRS):
        in_pair = jnp.clip(nblk - 2 * pair, 0, 2)

        @pl.when(in_pair == 2)
        def _():
            compute(2 * pair * MOE_BLK, 2 * MOE_BLK)
            issue_quota(pair)

        @pl.when(in_pair == 1)
        def _():
            compute(2 * pair * MOE_BLK, MOE_BLK)
            issue_quota(pair)

        @pl.when(jnp.logical_and(in_pair == 0, live))
        def _():
            issue_quota(pair)

        @pl.when(jnp.logical_and(jnp.logical_not(live), prev_live))
        def _():
            issue_quota(pair, gather=False)

    @pl.when(jnp.logical_and(c == MOE_NCHUNK - 1, jnp.logical_or(live, prev_live)))
    def _():
        pltpu.make_async_copy(yout_ref, y4_hbm.at[pl.ds(0, MOE_ITEM_ROWS * ROW_TILES), :], ssem).wait()

    @pl.when(jnp.logical_and(c == MOE_NCHUNK - 1, live))
    def _():
        for s in range(MOE_ITEM_BLKS):
            @pl.when(s < nblk)
            def _():
                rows = slice(s * MOE_BLK, (s + 1) * MOE_BLK)
                for j in range(ROW_TILES):
                    yout_ref[pl.ds(s * MOE_BLK * ROW_TILES + j, MOE_BLK, stride=ROW_TILES), :] = (
                        acc_ref[rows, j * LANES:(j + 1) * LANES])


def _moe(it_e, it_start, it_nblk, slot_src, slot_dst, h2, w1, b1, w2, b2):
    def chunk_of(it, c, n_ref):
        return jnp.where(n_ref[it] > 0, c, MOE_NCHUNK - 1)

    def w1_map(it, c, e_ref, s_ref, n_ref, *_):
        return (e_ref[it], 0, chunk_of(it, c, n_ref))

    def w2_map(it, c, e_ref, s_ref, n_ref, *_):
        return (e_ref[it], chunk_of(it, c, n_ref), 0)

    def b2_map(it, c, e_ref, *_):
        return (e_ref[it], 0, 0)

    sel = (jnp.arange(2 * MOE_TH, dtype=jnp.int32)[:, None]
           == 2 * jnp.arange(MOE_TH, dtype=jnp.int32)[None, :]).astype(BF16)

    grid_spec = pltpu.PrefetchScalarGridSpec(
        num_scalar_prefetch=5,
        grid=(MOE_NITEMS + 1, MOE_NCHUNK),
        in_specs=[pl.BlockSpec(memory_space=pl.ANY),
                  pl.BlockSpec((1, D_MODEL, 2 * MOE_TH), w1_map),
                  pl.BlockSpec((1, 1, 2 * MOE_TH), w1_map),
                  pl.BlockSpec((1, MOE_TH, D_MODEL), w2_map),
                  pl.BlockSpec((1, 1, D_MODEL), b2_map),
                  pl.BlockSpec((2 * MOE_TH, MOE_TH), lambda it, c, *_: (0, 0))],
        out_specs=pl.BlockSpec(memory_space=pl.ANY),
        scratch_shapes=[pltpu.VMEM((MOE_ITEM_ROWS * ROW_TILES, LANES), F32),
                        pltpu.VMEM((MOE_ITEM_ROWS, D_MODEL), BF16),
                        pltpu.VMEM((MOE_ITEM_ROWS, D_MODEL), F32),
                        pltpu.VMEM((MOE_ITEM_ROWS * ROW_TILES, LANES), F32),
                        pltpu.VMEM((D_MODEL, 2 * MOE_TH), BF16),
                        pltpu.VMEM((MOE_TH, D_MODEL), BF16),
                        pltpu.SemaphoreType.DMA(()),
                        pltpu.SemaphoreType.DMA(())],
    )
    return pl.pallas_call(
        _moe_kernel,
        grid_spec=grid_spec,
        out_shape=jax.ShapeDtypeStruct(((N_ASSIGN + MOE_ITEM_ROWS) * ROW_TILES, LANES), F32),
        compiler_params=_cparams(("arbitrary", "arbitrary")),
        name="moe_experts",
    )(it_e, it_start, it_nblk, slot_src, slot_dst, h2, w1,
      b1.reshape(N_EXPERTS, 1, 2 * D_EXPERT), w2, b2.reshape(N_EXPERTS, 1, D_MODEL), sel)


def _route_tables(top_idx):
    e_flat = top_idx.reshape(N_ASSIGN)
    onehot = (e_flat[:, None] == jnp.arange(N_EXPERTS, dtype=jnp.int32)[None, :]).astype(jnp.int32)
    csum = jnp.cumsum(onehot, axis=0)
    rank = jnp.sum(csum * onehot, axis=-1) - 1
    counts = csum[-1]
    nblk = (counts + MOE_BLK - 1) // MOE_BLK
    blk_end = jnp.cumsum(nblk)
    blk_start = blk_end - nblk
    dest = blk_start[e_flat] * MOE_BLK + rank
    slot_a = jnp.full((MOE_SLOTS,), -1, jnp.int32).at[dest].set(jnp.arange(N_ASSIGN, dtype=jnp.int32))
    slot_src = jnp.where(slot_a >= 0, slot_a // TOP_K, 0)
    slot_dst = jnp.where(slot_a >= 0, (slot_a % TOP_K) * N_TOK + slot_a // TOP_K, -1)

    nit = (nblk + MOE_ITEM_BLKS - 1) // MOE_ITEM_BLKS
    it_end = jnp.cumsum(nit)
    it_begin = it_end - nit
    n_used = it_end[-1]
    ids = jnp.arange(MOE_NITEMS + 2, dtype=jnp.int32)
    valid = ids < n_used
    last = jnp.minimum(ids, n_used - 1)
    e_of = jnp.minimum(jnp.sum((it_end[None, :] <= last[:, None]).astype(jnp.int32), axis=1), N_EXPERTS - 1)
    local = ids - it_begin[e_of]
    it_start = jnp.where(valid, (blk_start[e_of] + local * MOE_ITEM_BLKS) * MOE_BLK, 0).astype(jnp.int32)
    it_nblk = jnp.where(valid, jnp.clip(nblk[e_of] - local * MOE_ITEM_BLKS, 0, MOE_ITEM_BLKS), 0)
    return e_of, it_start, it_nblk.astype(jnp.int32), slot_src, slot_dst


FIN_TM = 256


def _final_kernel(x1_ref, ya_ref, yb_ref, yc_ref, yd_ref, tw_ref, gf_ref, sho_ref, sco_ref, nw_ref, o_ref):
    tw = tw_ref[...]
    pieces = []
    for j in range(ROW_TILES):
        acc = None
        for kk, y_ref in enumerate((ya_ref, yb_ref, yc_ref, yd_ref)):
            term = tw[:, kk:kk + 1] * y_ref[pl.ds(j, FIN_TM, stride=ROW_TILES), :]
            acc = term if acc is None else acc + term
        pieces.append(acc)
    moe = jnp.concatenate(pieces, axis=1)
    x2 = x1_ref[...] + gf_ref[0] * moe
    o_ref[...] = _rms(x2, nw_ref[...]) * (1.0 + sco_ref[0]) + sho_ref[0]


def _final(x1, y4, tw, gf, sho, sco, nw):
    tm = FIN_TM
    per_b = SEQ // tm
    row = lambda i: (i, 0)
    bvec = lambda i: (i // per_b, 0, 0)
    nb = N_TOK // tm
    kth = lambda kk: pl.BlockSpec((tm * ROW_TILES, LANES), lambda i: (kk * nb + i, 0))
    return pl.pallas_call(
        _final_kernel,
        grid=(nb,),
        in_specs=[pl.BlockSpec((tm, D_MODEL), row),
                  kth(0), kth(1), kth(2), kth(3),
                  pl.BlockSpec((tm, LANES), row),
                  pl.BlockSpec((1, 1, D_MODEL), bvec),
                  pl.BlockSpec((1, 1, D_MODEL), bvec),
                  pl.BlockSpec((1, 1, D_MODEL), bvec),
                  pl.BlockSpec((1, D_MODEL), lambda i: (0, 0))],
        out_specs=pl.BlockSpec((tm, D_MODEL), row),
        out_shape=jax.ShapeDtypeStruct((N_TOK, D_MODEL), F32),
        compiler_params=_cparams(("arbitrary",)),
        name="combine_final",
    )(x1, y4, y4, y4, y4, tw, gf, sho, sco, nw)


def _deinterleave(n):
    return np.concatenate([np.arange(0, n, 2), np.arange(1, n, 2)])


def _pack_w_in(w_in):
    o = np.cumsum([0, MLA_Q_RANK, MLA_KV_RANK + MLA_ROPE, GLA_HEADS * GLA_DK, GLA_HEADS * GLA_DK,
                   GLA_HEADS * GLA_DV, GLA_GATE_RANK, GLA_HEADS * GLA_DV, D_MODEL, D_MODEL])
    seg = lambda i: w_in[:, o[i]:o[i + 1]]
    kv = seg(1)
    zeros = lambda n: jnp.zeros((D_MODEL, n), w_in.dtype)
    kpe = kv[:, MLA_KV_RANK:][:, _deinterleave(MLA_ROPE)]
    parts = [seg(0), kv[:, :MLA_KV_RANK], kpe, zeros(LANES - MLA_ROPE),
             seg(5), zeros(LANES - GLA_GATE_RANK), seg(2), seg(3), seg(4), seg(6), seg(7), seg(8)]
    return jnp.concatenate(parts, axis=1).astype(BF16)


def _pack_w_q_b(w):
    w = w.reshape(MLA_Q_RANK, MLA_HEADS, MLA_NOPE + MLA_ROPE)
    nope = w[:, :, :MLA_NOPE]
    pe = w[:, :, MLA_NOPE:][:, :, _deinterleave(MLA_ROPE)]
    pad = jnp.zeros((MLA_Q_RANK, MLA_HEADS, MLA_HD - MLA_NOPE - MLA_ROPE), w.dtype)
    return jnp.concatenate([nope, pe, pad], axis=-1).reshape(MLA_Q_RANK, MLA_HEADS * MLA_HD).astype(BF16)


def _rope_consts():
    inv_freq = ROPE_THETA ** (-(jnp.arange(0, MLA_ROPE, 2, dtype=F32) / MLA_ROPE))
    half = MLA_ROPE // 2
    z = jnp.zeros((LANES - MLA_ROPE,), F32)
    rows = [jnp.concatenate([inv_freq, inv_freq, z]),
            jnp.concatenate([-jnp.ones((half,), F32), jnp.zeros((half,), F32), z]),
            jnp.concatenate([jnp.zeros((half,), F32), jnp.ones((half,), F32), z])]
    return jnp.concatenate([jnp.stack(rows), jnp.zeros((5, LANES), F32)], axis=0)


def kernel(x, c, positions, w_ada, b_ada, norm_mix_w, w_in, mla_q_norm_w, mla_w_q_b, mla_kv_norm_w,
           mla_w_kv_b, gla_w_gk_up, gla_b_gk_up, gla_norm_w, w_o_mla, w_o_gla, w_out, norm_ffn_w,
           w_router, b_router, w1, b1, w2, b2, w_ada_final, b_ada_final, norm_final_w):
    assert w_ada.shape[0] == 1, "single-layer stack"
    x2d = x.reshape(N_TOK, D_MODEL)
    c_t = c.T

    mod = _adaln(c_t, w_ada[0], b_ada[0])
    fmod = _adaln(c_t, w_ada_final, b_ada_final)
    sh_m, sc_m, g_m, sh_f, sc_f, g_f = [m.reshape(BATCH, 1, D_MODEL) for m in jnp.split(mod, N_MOD, axis=-1)]
    sh_o, sc_o = [m.reshape(BATCH, 1, D_MODEL) for m in jnp.split(fmod, 2, axis=-1)]

    proj = _inproj(x2d, norm_mix_w[0].reshape(1, D_MODEL), sh_m, sc_m, _pack_w_in(w_in[0]))

    q, k, v = _mla_prep(proj, positions.reshape(N_TOK, 1), _rope_consts(),
                        mla_q_norm_w[0].reshape(1, MLA_Q_RANK), mla_kv_norm_w[0].reshape(1, MLA_KV_RANK),
                        _pack_w_q_b(mla_w_q_b[0]), mla_w_kv_b[0].astype(BF16))
    o_mla = _attention(q, k, v)

    wgk = jnp.concatenate([gla_w_gk_up[0], jnp.zeros((LANES - GLA_GATE_RANK, GLA_HEADS * GLA_DK), F32)], axis=0)
    o_gla = _gla(proj, wgk, gla_b_gk_up[0].reshape(1, GLA_HEADS * GLA_DK), gla_norm_w[0].reshape(1, GLA_DV))

    wr = jnp.concatenate([w_router[0], jnp.zeros((D_MODEL, LANES - N_EXPERTS), F32)], axis=1)
    wr_hi = wr.astype(BF16)
    wr_lo = (wr - wr_hi.astype(F32)).astype(BF16)
    br = jnp.concatenate([b_router[0], jnp.zeros((LANES - N_EXPERTS,), F32)]).reshape(1, LANES)
    x1, h2, top_idx, top_w = _merge(
        x2d, o_mla, o_gla, proj, g_m, sh_f, sc_f, norm_ffn_w[0].reshape(1, D_MODEL),
        w_o_mla[0].astype(BF16), w_o_gla[0].astype(BF16), w_out[0].astype(BF16), wr_hi, wr_lo, br)

    it_e, it_start, it_nblk, slot_src, slot_dst = _route_tables(top_idx[:, :TOP_K])
    y4 = _moe(it_e, it_start, it_nblk, slot_src, slot_dst, h2, w1[0], b1[0], w2[0], b2[0])

    out = _final(x1, y4, top_w, g_f, sh_o, sc_o, norm_final_w.reshape(1, D_MODEL))
    return out.reshape(BATCH, SEQ, D_MODEL)
```

```python
import functools

import jax
import jax.numpy as jnp
import numpy as np
from jax import lax
from jax.experimental import pallas as pl
from jax.experimental.pallas import tpu as pltpu

F32 = jnp.float32
BF16 = jnp.bfloat16

D_MODEL = 2048
BATCH = 2
SEQ = 4096
N_TOK = BATCH * SEQ

MLA_HEADS = 8
MLA_Q_RANK = 512
MLA_KV_RANK = 256
MLA_NOPE = 128
MLA_ROPE = 64
MLA_V = 128
ROPE_THETA = 10000.0
MLA_HD = 256

GLA_HEADS = 4
GLA_DK = 128
GLA_DV = 256
GLA_GATE_RANK = 16
GLA_GATE_NORM = 16.0
GLA_CHUNK = 64

N_EXPERTS = 32
TOP_K = 4
D_EXPERT = D_MODEL
SWIGLU_LIMIT = 7.0
SWIGLU_ALPHA = 1.702
EPS = 1e-6
N_MOD = 6

LANES = 128
ROW_TILES = D_MODEL // LANES
VMEM_LIMIT = 56 * 1024 * 1024

P_QLAT = 0
P_CKV = 512
P_KPE = 768
P_GLR = 896
P_GQ = 1024
P_GK = 1536
P_GV = 2048
P_GOUT = 3072
P_GATEA = 4096
P_GATEB = 6144
P_TOTAL = 8192

MOE_BLK = 256
MOE_ITEM_BLKS = 4
MOE_ITEM_ROWS = MOE_BLK * MOE_ITEM_BLKS
MOE_TH = 256
MOE_NCHUNK = D_EXPERT // MOE_TH
N_ASSIGN = N_TOK * TOP_K
MOE_NBLK_MAX = N_ASSIGN // MOE_BLK + N_EXPERTS
MOE_SLOTS = MOE_NBLK_MAX * MOE_BLK
MOE_NITEMS = N_ASSIGN // MOE_ITEM_ROWS + N_EXPERTS


def _cparams(sem, vmem=VMEM_LIMIT):
    return pltpu.CompilerParams(dimension_semantics=sem, vmem_limit_bytes=vmem)


def _rms(x, w):
    return x * lax.rsqrt(jnp.mean(x * x, axis=-1, keepdims=True) + EPS) * w


def _adaln_kernel(ct_ref, w_ref, b_ref, o_ref):
    ct = ct_ref[...]
    cond = ct * jax.nn.sigmoid(ct)
    w = w_ref[...]
    for b in range(BATCH):
        o_ref[b:b + 1, :] = jnp.sum(w * cond[:, b:b + 1], axis=0, keepdims=True) + b_ref[...]


def _adaln(c_t, w, b, tn=1024):
    d, n = w.shape
    return pl.pallas_call(
        _adaln_kernel,
        grid=(n // tn,),
        in_specs=[pl.BlockSpec((d, BATCH), lambda j: (0, 0)),
                  pl.BlockSpec((d, tn), lambda j: (0, j)),
                  pl.BlockSpec((1, tn), lambda j: (0, j))],
        out_specs=pl.BlockSpec((BATCH, tn), lambda j: (0, j)),
        out_shape=jax.ShapeDtypeStruct((BATCH, n), F32),
        compiler_params=_cparams(("arbitrary",)),
        name="adaln",
    )(c_t, w, b.reshape(1, n))


IN_TM = 1024
IN_TN = 1024


def _inproj_kernel(x_ref, nw_ref, sh_ref, sc_ref, w_ref, o_ref, h_ref):
    @pl.when(pl.program_id(1) == 0)
    def _():
        for r in range(0, IN_TM, 256):
            x = x_ref[r:r + 256, :]
            h = _rms(x, nw_ref[...]) * (1.0 + sc_ref[0]) + sh_ref[0]
            h_ref[r:r + 256, :] = h.astype(BF16)

    o_ref[...] = jnp.dot(h_ref[...], w_ref[...], preferred_element_type=F32).astype(BF16)


def _inproj(x2d, nw, sh, sc, w_packed):
    per_b = SEQ // IN_TM
    return pl.pallas_call(
        _inproj_kernel,
        grid=(N_TOK // IN_TM, P_TOTAL // IN_TN),
        in_specs=[pl.BlockSpec((IN_TM, D_MODEL), lambda i, j: (i, 0)),
                  pl.BlockSpec((1, D_MODEL), lambda i, j: (0, 0)),
                  pl.BlockSpec((1, 1, D_MODEL), lambda i, j: (i // per_b, 0, 0)),
                  pl.BlockSpec((1, 1, D_MODEL), lambda i, j: (i // per_b, 0, 0)),
                  pl.BlockSpec((D_MODEL, IN_TN), lambda i, j: (0, j))],
        out_specs=pl.BlockSpec((IN_TM, IN_TN), lambda i, j: (i, j)),
        out_shape=jax.ShapeDtypeStruct((N_TOK, P_TOTAL), BF16),
        scratch_shapes=[pltpu.VMEM((IN_TM, D_MODEL), BF16)],
        compiler_params=_cparams(("arbitrary", "arbitrary")),
        name="inproj",
    )(x2d, nw, sh, sc, w_packed)


PREP_TM = 512


def _mla_prep_kernel(ql_ref, ckv_ref, kpe_ref, pos_ref, rc_ref, qnw_ref, kvnw_ref, wq_ref, wkv_ref,
                     q_ref, k_ref, v_ref):
    scale = (MLA_NOPE + MLA_ROPE) ** -0.5
    qn = _rms(ql_ref[...].astype(F32), qnw_ref[...]).astype(BF16)
    q = jnp.dot(qn, wq_ref[...], preferred_element_type=F32)
    cn = _rms(ckv_ref[...].astype(F32), kvnw_ref[...]).astype(BF16)
    kv = jnp.dot(cn, wkv_ref[...], preferred_element_type=F32)

    ang = pos_ref[...].astype(F32) * rc_ref[0:1, :]
    cos_t = jnp.cos(ang)
    sin_t = jnp.sin(ang)
    sin_a = sin_t * rc_ref[1:2, :]
    sin_b = sin_t * rc_ref[2:3, :]

    def rope(t):
        return (t * cos_t + pltpu.roll(t, LANES - MLA_ROPE // 2, 1) * sin_a
                + pltpu.roll(t, MLA_ROPE // 2, 1) * sin_b)

    kpe = rope(kpe_ref[...].astype(F32)).astype(BF16)
    for h in range(MLA_HEADS):
        c0 = h * MLA_HD
        q_ref[0, h, :, 0:LANES] = (q[:, c0:c0 + LANES] * scale).astype(BF16)
        q_ref[0, h, :, LANES:MLA_HD] = (rope(q[:, c0 + LANES:c0 + MLA_HD]) * scale).astype(BF16)
        k_ref[0, h, :, 0:LANES] = kv[:, c0:c0 + LANES].astype(BF16)
        k_ref[0, h, :, LANES:MLA_HD] = kpe
        v_ref[0, h, :, :] = kv[:, c0 + LANES:c0 + MLA_HD].astype(BF16)


def _mla_prep(proj, pos_col, rope_consts, qnw, kvnw, wq, wkv):
    tm = PREP_TM
    per_b = SEQ // tm
    row = lambda i: (i, 0)
    const = lambda i: (0, 0)
    out_map = lambda i: (i // per_b, 0, i % per_b, 0)
    return pl.pallas_call(
        _mla_prep_kernel,
        grid=(N_TOK // tm,),
        in_specs=[pl.BlockSpec((tm, MLA_Q_RANK), lambda i: (i, P_QLAT // MLA_Q_RANK)),
                  pl.BlockSpec((tm, MLA_KV_RANK), lambda i: (i, P_CKV // MLA_KV_RANK)),
                  pl.BlockSpec((tm, LANES), lambda i: (i, P_KPE // LANES)),
                  pl.BlockSpec((tm, 1), row),
                  pl.BlockSpec((8, LANES), const),
                  pl.BlockSpec((1, MLA_Q_RANK), const),
                  pl.BlockSpec((1, MLA_KV_RANK), const),
                  pl.BlockSpec((MLA_Q_RANK, MLA_HEADS * MLA_HD), const),
                  pl.BlockSpec((MLA_KV_RANK, MLA_HEADS * MLA_HD), const)],
        out_specs=[pl.BlockSpec((1, MLA_HEADS, tm, MLA_HD), out_map),
                   pl.BlockSpec((1, MLA_HEADS, tm, MLA_HD), out_map),
                   pl.BlockSpec((1, MLA_HEADS, tm, MLA_V), out_map)],
        out_shape=[jax.ShapeDtypeStruct((BATCH, MLA_HEADS, SEQ, MLA_HD), BF16),
                   jax.ShapeDtypeStruct((BATCH, MLA_HEADS, SEQ, MLA_HD), BF16),
                   jax.ShapeDtypeStruct((BATCH, MLA_HEADS, SEQ, MLA_V), BF16)],
        compiler_params=_cparams(("arbitrary",)),
        name="mla_prep",
    )(proj, proj, proj, pos_col, rope_consts, qnw, kvnw, wq, wkv)


ATT_T = 512


def _attn_kernel(q_ref, k_ref, v_ref, o_ref, m_ref, l_ref, acc_ref, s0_ref, s1_ref):
    qi = pl.program_id(2)
    q = q_ref[0, 0]
    m_ref[...] = jnp.full(m_ref.shape, -jnp.inf, F32)
    l_ref[...] = jnp.zeros(l_ref.shape, F32)
    acc_ref[...] = jnp.zeros(acc_ref.shape, F32)

    def scores(j):
        r0 = pl.multiple_of(j * ATT_T, ATT_T)
        k = k_ref[0, 0, pl.ds(r0, ATT_T), :]
        return lax.dot_general(k, q, (((1,), (1,)), ((), ())), preferred_element_type=F32)

    def step(j, src_ref, dst_ref):
        masked = dst_ref is None
        r0 = pl.multiple_of(j * ATT_T, ATT_T)
        v = v_ref[0, 0, pl.ds(r0, ATT_T), :]
        s = src_ref[...]
        if not masked:
            dst_ref[...] = scores(j + 1)
        if masked:
            key = lax.broadcasted_iota(jnp.int32, s.shape, 0)
            qry = lax.broadcasted_iota(jnp.int32, s.shape, 1)
            s = jnp.where(key <= qry, s, -jnp.inf)
        m_old = m_ref[...]
        m_new = jnp.maximum(m_old, jnp.max(s, axis=0, keepdims=True))
        p = jnp.exp(s - m_new)
        alpha = jnp.exp(m_old - m_new)
        l_ref[...] = alpha * l_ref[...] + jnp.sum(p, axis=0, keepdims=True)
        pv = lax.dot_general(v, p.astype(BF16), (((0,), (0,)), ((), ())), preferred_element_type=F32)
        acc_ref[...] = alpha * acc_ref[...] + pv
        m_ref[...] = m_new

    def body(jj, carry):
        step(2 * jj, s0_ref, s1_ref)
        step(2 * jj + 1, s1_ref, s0_ref)
        return carry

    s0_ref[...] = scores(0)
    lax.fori_loop(0, qi // 2, body, 0)

    @pl.when(qi % 2 == 1)
    def _():
        step(qi - 1, s0_ref, s1_ref)
        step(qi, s1_ref, None)

    @pl.when(qi % 2 == 0)
    def _():
        step(qi, s0_ref, None)

    o_ref[...] = jnp.transpose(acc_ref[...] / l_ref[...]).astype(BF16)


def _attention(q, k, v):
    t = ATT_T
    nq = SEQ // t
    return pl.pallas_call(
        _attn_kernel,
        grid=(BATCH, MLA_HEADS, nq),
        in_specs=[pl.BlockSpec((1, 1, t, MLA_HD), lambda b, h, i: (b, h, i, 0)),
                  pl.BlockSpec((1, 1, SEQ, MLA_HD), lambda b, h, i: (b, h, 0, 0)),
                  pl.BlockSpec((1, 1, SEQ, MLA_V), lambda b, h, i: (b, h, 0, 0))],
        out_specs=pl.BlockSpec((t, MLA_V), lambda b, h, i: (b * nq + i, h)),
        out_shape=jax.ShapeDtypeStruct((N_TOK, MLA_HEADS * MLA_V), BF16),
        scratch_shapes=[pltpu.VMEM((1, t), F32), pltpu.VMEM((1, t), F32), pltpu.VMEM((MLA_V, t), F32),
                        pltpu.VMEM((t, t), F32), pltpu.VMEM((t, t), F32)],
        compiler_params=_cparams(("arbitrary", "arbitrary", "arbitrary")),
        name="mla_attn",
    )(q, k, v)


GLA_TG = 512
GLA_DIAG = 8


def _gla_kernel(q_ref, k_ref, v_ref, glr_ref, gout_ref, wgk_ref, bgk_ref, nw_ref, o_ref, st_ref, g_ref):
    C = GLA_CHUNK

    @pl.when(pl.program_id(2) == 0)
    def _():
        st_ref[...] = jnp.zeros(st_ref.shape, F32)

    z = jnp.dot(glr_ref[...].astype(F32), wgk_ref[...], preferred_element_type=F32,
                precision=lax.Precision.HIGHEST) + bgk_ref[...]
    g_ref[...] = jax.nn.log_sigmoid(z) / GLA_GATE_NORM

    row = lax.broadcasted_iota(jnp.int32, (C, GLA_DK), 0)
    ii = lax.broadcasted_iota(jnp.int32, (C, C), 0)
    jj = lax.broadcasted_iota(jnp.int32, (C, C), 1)
    nt = (((1,), (1,)), ((), ()))

    def chunk(ci, carry):
        r0 = pl.multiple_of(ci * C, C)
        b = g_ref[pl.ds(r0, C), :]
        s = 1
        while s < C:
            b = b + jnp.where(row >= s, pltpu.roll(b, s, 0), 0.0)
            s *= 2
        q = q_ref[pl.ds(r0, C), :].astype(F32) * (GLA_DK ** -0.5)
        k = k_ref[pl.ds(r0, C), :].astype(F32)
        v = v_ref[pl.ds(r0, C), :]
        st = st_ref[...]

        o = lax.dot_general((q * jnp.exp(b)).astype(BF16), st.astype(BF16), nt,
                            preferred_element_type=F32)

        a = jnp.zeros((C, C), F32)
        lvl = C // 2
        while lvl >= GLA_DIAG:
            pieces = [jnp.broadcast_to(b[t + lvl:t + lvl + 1, :], (2 * lvl, GLA_DK))
                      for t in range(0, C, 2 * lvl)]
            m = pieces[0] if len(pieces) == 1 else jnp.concatenate(pieces, axis=0)
            odd = (row // lvl) % 2 == 1
            ql = jnp.where(odd, q * jnp.exp(jnp.minimum(b - m, 0.0)), 0.0).astype(BF16)
            kl = jnp.where(odd, 0.0, k * jnp.exp(jnp.minimum(m - b, 0.0))).astype(BF16)
            al = lax.dot_general(ql, kl, nt, preferred_element_type=F32)
            a = a + jnp.where(ii // (2 * lvl) == jj // (2 * lvl), al, 0.0)
            lvl //= 2
        for d in range(GLA_DIAG):
            if d == 0:
                t = q * k
            else:
                t = q * pltpu.roll(k, d, 0) * jnp.exp(jnp.minimum(b - pltpu.roll(b, d, 0), 0.0))
            col = jnp.sum(t, axis=-1, keepdims=True)
            a = a + jnp.where((ii - jj == d) & (ii % GLA_DIAG >= d), col, 0.0)
        o = o + jnp.dot(a.astype(BF16), v, preferred_element_type=F32)

        b_last = b[C - 1:C, :]
        kdec = (k * jnp.exp(b_last - b)).astype(BF16)
        st_ref[...] = st * jnp.exp(b_last) + lax.dot_general(
            v, kdec, (((0,), (0,)), ((), ())), preferred_element_type=F32)

        gate = gout_ref[pl.ds(r0, C), :].astype(F32)
        y = _rms(o, nw_ref[...]) * (gate * jax.nn.sigmoid(gate))
        o_ref[pl.ds(r0, C), :] = y.astype(BF16)
        return carry

    lax.fori_loop(0, GLA_TG // C, chunk, 0)


def _gla(proj, wgk, bgk, nw):
    tg = GLA_TG
    per_b = SEQ // tg
    rb = lambda b, h, t: b * per_b + t
    return pl.pallas_call(
        _gla_kernel,
        grid=(BATCH, GLA_HEADS, per_b),
        in_specs=[pl.BlockSpec((tg, GLA_DK), lambda b, h, t: (rb(b, h, t), P_GQ // GLA_DK + h)),
                  pl.BlockSpec((tg, GLA_DK), lambda b, h, t: (rb(b, h, t), P_GK // GLA_DK + h)),
                  pl.BlockSpec((tg, GLA_DV), lambda b, h, t: (rb(b, h, t), P_GV // GLA_DV + h)),
                  pl.BlockSpec((tg, LANES), lambda b, h, t: (rb(b, h, t), P_GLR // LANES)),
                  pl.BlockSpec((tg, GLA_DV), lambda b, h, t: (rb(b, h, t), P_GOUT // GLA_DV + h)),
                  pl.BlockSpec((LANES, GLA_DK), lambda b, h, t: (0, h)),
                  pl.BlockSpec((1, GLA_DK), lambda b, h, t: (0, h)),
                  pl.BlockSpec((1, GLA_DV), lambda b, h, t: (0, 0))],
        out_specs=pl.BlockSpec((tg, GLA_DV), lambda b, h, t: (rb(b, h, t), h)),
        out_shape=jax.ShapeDtypeStruct((N_TOK, GLA_HEADS * GLA_DV), BF16),
        scratch_shapes=[pltpu.VMEM((GLA_DV, GLA_DK), F32), pltpu.VMEM((tg, GLA_DK), F32)],
        compiler_params=_cparams(("arbitrary", "arbitrary", "arbitrary")),
        name="gla",
    )(proj, proj, proj, proj, proj, wgk, bgk, nw)


MRG_TM = 256


def _merge_kernel(x_ref, oa_ref, ob_ref, ga_ref, gb_ref, gm_ref, shf_ref, scf_ref, nfw_ref,
                  woa_ref, wob_ref, wout_ref, wrh_ref, wrl_ref, br_ref,
                  x1_ref, h2_ref, idx_ref, tw_ref):
    ya = jnp.dot(oa_ref[...], woa_ref[...], preferred_element_type=F32)
    yb = jnp.dot(ob_ref[...], wob_ref[...], preferred_element_type=F32)
    merged = (jax.nn.sigmoid(ga_ref[...].astype(F32)) * ya
              + jax.nn.sigmoid(gb_ref[...].astype(F32)) * yb).astype(BF16)
    x1 = x_ref[...] + gm_ref[0] * jnp.dot(merged, wout_ref[...], preferred_element_type=F32)
    x1_ref[...] = x1
    h2 = _rms(x1, nfw_ref[...]) * (1.0 + scf_ref[0]) + shf_ref[0]
    for j in range(ROW_TILES):
        h2_ref[pl.ds(j, MRG_TM, stride=ROW_TILES), :] = h2[:, j * LANES:(j + 1) * LANES]

    hi = h2.astype(BF16)
    lo = (h2 - hi.astype(F32)).astype(BF16)
    logits = (jnp.dot(hi, wrh_ref[...], preferred_element_type=F32)
              + jnp.dot(hi, wrl_ref[...], preferred_element_type=F32)
              + jnp.dot(lo, wrh_ref[...], preferred_element_type=F32)) + br_ref[...]
    lane = lax.broadcasted_iota(jnp.int32, logits.shape, 1)
    vals = jnp.where(lane < N_EXPERTS, logits, -jnp.inf)
    idx_out = jnp.zeros(logits.shape, jnp.int32)
    w_out = jnp.zeros(logits.shape, F32)
    top = None
    denom = None
    for kk in range(TOP_K):
        m = jnp.max(vals, axis=-1, keepdims=True)
        sel = jnp.min(jnp.where(vals == m, lane, LANES), axis=-1, keepdims=True)
        if kk == 0:
            top = m
        e = jnp.exp(m - top)
        denom = e if kk == 0 else denom + e
        idx_out = jnp.where(lane == kk, sel, idx_out)
        w_out = jnp.where(lane == kk, e, w_out)
        vals = jnp.where(lane == sel, -jnp.inf, vals)
    idx_ref[...] = idx_out
    tw_ref[...] = w_out / denom


def _merge(x2d, o_mla, o_gla, proj, gm, shf, scf, nfw, woa, wob, wout, wrh, wrl, br):
    tm = MRG_TM
    per_b = SEQ // tm
    row = lambda i: (i, 0)
    const = lambda i: (0, 0)
    bvec = lambda i: (i // per_b, 0, 0)
    single = dict(pipeline_mode=pl.Buffered(1))
    return pl.pallas_call(
        _merge_kernel,
        grid=(N_TOK // tm,),
        in_specs=[pl.BlockSpec((tm, D_MODEL), row),
                  pl.BlockSpec((tm, MLA_HEADS * MLA_V), row),
                  pl.BlockSpec((tm, GLA_HEADS * GLA_DV), row),
                  pl.BlockSpec((tm, D_MODEL), lambda i: (i, P_GATEA // D_MODEL)),
                  pl.BlockSpec((tm, D_MODEL), lambda i: (i, P_GATEB // D_MODEL)),
                  pl.BlockSpec((1, 1, D_MODEL), bvec),
                  pl.BlockSpec((1, 1, D_MODEL), bvec),
                  pl.BlockSpec((1, 1, D_MODEL), bvec),
                  pl.BlockSpec((1, D_MODEL), const),
                  pl.BlockSpec((MLA_HEADS * MLA_V, D_MODEL), const, **single),
                  pl.BlockSpec((GLA_HEADS * GLA_DV, D_MODEL), const, **single),
                  pl.BlockSpec((D_MODEL, D_MODEL), const, **single),
                  pl.BlockSpec((D_MODEL, LANES), const, **single),
                  pl.BlockSpec((D_MODEL, LANES), const, **single),
                  pl.BlockSpec((1, LANES), const)],
        out_specs=[pl.BlockSpec((tm, D_MODEL), row),
                   pl.BlockSpec((tm * ROW_TILES, LANES), row),
                   pl.BlockSpec((tm, LANES), row),
                   pl.BlockSpec((tm, LANES), row)],
        out_shape=[jax.ShapeDtypeStruct((N_TOK, D_MODEL), F32),
                   jax.ShapeDtypeStruct((N_TOK * ROW_TILES, LANES), F32),
                   jax.ShapeDtypeStruct((N_TOK, LANES), jnp.int32),
                   jax.ShapeDtypeStruct((N_TOK, LANES), F32)],
        compiler_params=_cparams(("arbitrary",)),
        name="merge_router",
    )(x2d, o_mla, o_gla, proj, proj, gm, shf, scf, nfw, woa, wob, wout, wrh, wrl, br)


MOE_PAIRS = MOE_ITEM_BLKS // 2
MOE_QSTEPS = MOE_NCHUNK // 2
MOE_QUOTA = MOE_ITEM_ROWS // (MOE_QSTEPS * MOE_PAIRS)


def _moe_kernel(it_e_ref, it_start_ref, it_nblk_ref, src_ref, dst_ref,
                h2_hbm, w1_ref, b1_ref, w2_ref, b2_ref, sel_ref, y4_hbm,
                stage_ref, xb_ref, acc_ref, yout_ref, w1b_ref, w2b_ref, gsem, ssem):
    it = pl.program_id(0)
    c = pl.program_id(1)
    nblk = it_nblk_ref[it]
    prev = jnp.maximum(it - 1, 0)
    nblk_prev = jnp.where(it > 0, it_nblk_ref[prev], 0)
    live = nblk > 0
    prev_live = nblk_prev > 0
    rows_prev = nblk_prev * MOE_BLK
    rows_next = it_nblk_ref[it + 1] * MOE_BLK
    base_prev = it_start_ref[prev]
    base_next = it_start_ref[it + 1]

    def token_rows(t):
        return pl.ds(pl.multiple_of(t * ROW_TILES, ROW_TILES), ROW_TILES)

    def gather_row(r, base, nrows):
        tok = jnp.where(r < nrows, src_ref[jnp.minimum(base + r, MOE_SLOTS - 1)], 0)
        pltpu.make_async_copy(h2_hbm.at[token_rows(tok), :], stage_ref.at[token_rows(r), :], gsem).start()

    def scatter_row(r):
        d = dst_ref[jnp.minimum(base_prev + r, MOE_SLOTS - 1)]
        d = jnp.where(jnp.logical_and(r < rows_prev, d >= 0), d, N_ASSIGN + r)
        pltpu.make_async_copy(yout_ref.at[token_rows(r), :], y4_hbm.at[token_rows(d), :], ssem).start()

    def issue_quota(pair, gather=True):
        r0 = (c * MOE_PAIRS + pair) * MOE_QUOTA
        for u in range(MOE_QUOTA):
            if gather:
                gather_row(r0 + u, base_next, rows_next)
            scatter_row(r0 + u)

    @pl.when(jnp.logical_and(it == 0, c == 0))
    def _():
        yout_ref[...] = jnp.zeros(yout_ref.shape, F32)

        def first(r, carry):
            gather_row(r, it_start_ref[0], nblk * MOE_BLK)
            return carry
        lax.fori_loop(0, MOE_ITEM_ROWS, first, 0)

    @pl.when(jnp.logical_and(c == 0, jnp.logical_or(it == 0, prev_live)))
    def _():
        pltpu.make_async_copy(h2_hbm.at[pl.ds(0, MOE_ITEM_ROWS * ROW_TILES), :], stage_ref, gsem).wait()

    @pl.when(jnp.logical_and(c == 0, live))
    def _():
        for s in range(MOE_ITEM_BLKS):
            @pl.when(s < nblk)
            def _():
                rows = slice(s * MOE_BLK, (s + 1) * MOE_BLK)
                for j in range(ROW_TILES):
                    piece = stage_ref[pl.ds(s * MOE_BLK * ROW_TILES + j, MOE_BLK, stride=ROW_TILES), :]
                    xb_ref[rows, j * LANES:(j + 1) * LANES] = piece.astype(BF16)
                acc_ref[rows, :] = jnp.broadcast_to(b2_ref[0], (MOE_BLK, D_MODEL))

    def compute(row0, nrows):
        rows = slice(row0, row0 + nrows)
        hm = jnp.dot(xb_ref[rows, :], w1b_ref[...], preferred_element_type=F32) + b1_ref[0]
        lin = pltpu.roll(hm, 2 * MOE_TH - 1, 1)
        glu = jnp.minimum(hm, SWIGLU_LIMIT)
        lin = jnp.clip(lin, -SWIGLU_LIMIT, SWIGLU_LIMIT)
        act = glu * jax.nn.sigmoid(SWIGLU_ALPHA * glu) * (lin + 1.0)
        lane = lax.broadcasted_iota(jnp.int32, act.shape, 1)
        act = jnp.where(lane % 2 == 0, act, 0.0).astype(BF16)
        act = jnp.dot(act, sel_ref[...], preferred_element_type=F32).astype(BF16)
        acc_ref[rows, :] += jnp.dot(act, w2b_ref[...], preferred_element_type=F32)

    @pl.when(live)
    def _():
        w1b_ref[...] = w1_ref[0].astype(BF16)
        w2b_ref[...] = w2_ref[0].astype(BF16)

    early = c < MOE_QSTEPS
    for pair in range(MOE_PAIRS):
        in_pair = jnp.clip(nblk - 2 * pair, 0, 2)
        for nb in (2, 1):
            @pl.when(jnp.logical_and(in_pair == nb, early))
            def _():
                compute(2 * pair * MOE_BLK, nb * MOE_BLK)
                issue_quota(pair)

            @pl.when(jnp.logical_and(in_pair == nb, jnp.logical_not(early)))
            def _():
                compute(2 * pair * MOE_BLK, nb * MOE_BLK)

        @pl.when(jnp.logical_and(jnp.logical_and(in_pair == 0, live), early))
        def _():
            issue_quota(pair)

        @pl.when(jnp.logical_and(jnp.logical_and(jnp.logical_not(live), prev_live), early))
        def _():
            issue_quota(pair, gather=False)

    @pl.when(jnp.logical_and(c == MOE_NCHUNK - 1, jnp.logical_or(live, prev_live)))
    def _():
        pltpu.make_async_copy(yout_ref, y4_hbm.at[pl.ds(0, MOE_ITEM_ROWS * ROW_TILES), :], ssem).wait()

    @pl.when(jnp.logical_and(c == MOE_NCHUNK - 1, live))
    def _():
        for s in range(MOE_ITEM_BLKS):
            @pl.when(s < nblk)
            def _():
                rows = slice(s * MOE_BLK, (s + 1) * MOE_BLK)
                for j in range(ROW_TILES):
                    yout_ref[pl.ds(s * MOE_BLK * ROW_TILES + j, MOE_BLK, stride=ROW_TILES), :] = (
                        acc_ref[rows, j * LANES:(j + 1) * LANES])


def _moe(it_e, it_start, it_nblk, slot_src, slot_dst, h2, w1, b1, w2, b2):
    def chunk_of(it, c, n_ref):
        return jnp.where(n_ref[it] > 0, c, MOE_NCHUNK - 1)

    def w1_map(it, c, e_ref, s_ref, n_ref, *_):
        return (e_ref[it], 0, chunk_of(it, c, n_ref))

    def w2_map(it, c, e_ref, s_ref, n_ref, *_):
        return (e_ref[it], chunk_of(it, c, n_ref), 0)

    def b2_map(it, c, e_ref, *_):
        return (e_ref[it], 0, 0)

    sel = (jnp.arange(2 * MOE_TH, dtype=jnp.int32)[:, None]
           == 2 * jnp.arange(MOE_TH, dtype=jnp.int32)[None, :]).astype(BF16)

    grid_spec = pltpu.PrefetchScalarGridSpec(
        num_scalar_prefetch=5,
        grid=(MOE_NITEMS + 1, MOE_NCHUNK),
        in_specs=[pl.BlockSpec(memory_space=pl.ANY),
                  pl.BlockSpec((1, D_MODEL, 2 * MOE_TH), w1_map),
                  pl.BlockSpec((1, 1, 2 * MOE_TH), w1_map),
                  pl.BlockSpec((1, MOE_TH, D_MODEL), w2_map),
                  pl.BlockSpec((1, 1, D_MODEL), b2_map),
                  pl.BlockSpec((2 * MOE_TH, MOE_TH), lambda it, c, *_: (0, 0))],
        out_specs=pl.BlockSpec(memory_space=pl.ANY),
        scratch_shapes=[pltpu.VMEM((MOE_ITEM_ROWS * ROW_TILES, LANES), F32),
                        pltpu.VMEM((MOE_ITEM_ROWS, D_MODEL), BF16),
                        pltpu.VMEM((MOE_ITEM_ROWS, D_MODEL), F32),
                        pltpu.VMEM((MOE_ITEM_ROWS * ROW_TILES, LANES), F32),
                        pltpu.VMEM((D_MODEL, 2 * MOE_TH), BF16),
                        pltpu.VMEM((MOE_TH, D_MODEL), BF16),
                        pltpu.SemaphoreType.DMA(()),
                        pltpu.SemaphoreType.DMA(())],
    )
    return pl.pallas_call(
        _moe_kernel,
        grid_spec=grid_spec,
        out_shape=jax.ShapeDtypeStruct(((N_ASSIGN + MOE_ITEM_ROWS) * ROW_TILES, LANES), F32),
        compiler_params=_cparams(("arbitrary", "arbitrary")),
        name="moe_experts",
    )(it_e, it_start, it_nblk, slot_src, slot_dst, h2, w1,
      b1.reshape(N_EXPERTS, 1, 2 * D_EXPERT), w2, b2.reshape(N_EXPERTS, 1, D_MODEL), sel)


def _route_tables(top_idx):
    e_flat = top_idx.reshape(N_ASSIGN)
    onehot = (e_flat[:, None] == jnp.arange(N_EXPERTS, dtype=jnp.int32)[None, :]).astype(jnp.int32)
    csum = jnp.cumsum(onehot, axis=0)
    rank = jnp.sum(csum * onehot, axis=-1) - 1
    counts = csum[-1]
    nblk = (counts + MOE_BLK - 1) // MOE_BLK
    blk_end = jnp.cumsum(nblk)
    blk_start = blk_end - nblk
    dest = blk_start[e_flat] * MOE_BLK + rank
    slot_a = jnp.full((MOE_SLOTS,), -1, jnp.int32).at[dest].set(jnp.arange(N_ASSIGN, dtype=jnp.int32))
    slot_src = jnp.where(slot_a >= 0, slot_a // TOP_K, 0)
    slot_dst = jnp.where(slot_a >= 0, (slot_a % TOP_K) * N_TOK + slot_a // TOP_K, -1)

    nit = (nblk + MOE_ITEM_BLKS - 1) // MOE_ITEM_BLKS
    it_end = jnp.cumsum(nit)
    it_begin = it_end - nit
    n_used = it_end[-1]
    ids = jnp.arange(MOE_NITEMS + 2, dtype=jnp.int32)
    valid = ids < n_used
    last = jnp.minimum(ids, n_used - 1)
    e_of = jnp.minimum(jnp.sum((it_end[None, :] <= last[:, None]).astype(jnp.int32), axis=1), N_EXPERTS - 1)
    local = ids - it_begin[e_of]
    it_start = jnp.where(valid, (blk_start[e_of] + local * MOE_ITEM_BLKS) * MOE_BLK, 0).astype(jnp.int32)
    it_nblk = jnp.where(valid, jnp.clip(nblk[e_of] - local * MOE_ITEM_BLKS, 0, MOE_ITEM_BLKS), 0)
    return e_of, it_start, it_nblk.astype(jnp.int32), slot_src, slot_dst


FIN_TM = 256


def _final_kernel(x1_ref, ya_ref, yb_ref, yc_ref, yd_ref, tw_ref, gf_ref, sho_ref, sco_ref, nw_ref, o_ref):
    tw = tw_ref[...]
    pieces = []
    for j in range(ROW_TILES):
        acc = None
        for kk, y_ref in enumerate((ya_ref, yb_ref, yc_ref, yd_ref)):
            term = tw[:, kk:kk + 1] * y_ref[pl.ds(j, FIN_TM, stride=ROW_TILES), :]
            acc = term if acc is None else acc + term
        pieces.append(acc)
    moe = jnp.concatenate(pieces, axis=1)
    x2 = x1_ref[...] + gf_ref[0] * moe
    o_ref[...] = _rms(x2, nw_ref[...]) * (1.0 + sco_ref[0]) + sho_ref[0]


def _final(x1, y4, tw, gf, sho, sco, nw):
    tm = FIN_TM
    per_b = SEQ // tm
    row = lambda i: (i, 0)
    bvec = lambda i: (i // per_b, 0, 0)
    nb = N_TOK // tm
    kth = lambda kk: pl.BlockSpec((tm * ROW_TILES, LANES), lambda i: (kk * nb + i, 0))
    return pl.pallas_call(
        _final_kernel,
        grid=(nb,),
        in_specs=[pl.BlockSpec((tm, D_MODEL), row),
                  kth(0), kth(1), kth(2), kth(3),
                  pl.BlockSpec((tm, LANES), row),
                  pl.BlockSpec((1, 1, D_MODEL), bvec),
                  pl.BlockSpec((1, 1, D_MODEL), bvec),
                  pl.BlockSpec((1, 1, D_MODEL), bvec),
                  pl.BlockSpec((1, D_MODEL), lambda i: (0, 0))],
        out_specs=pl.BlockSpec((tm, D_MODEL), row),
        out_shape=jax.ShapeDtypeStruct((N_TOK, D_MODEL), F32),
        compiler_params=_cparams(("arbitrary",)),
        name="combine_final",
    )(x1, y4, y4, y4, y4, tw, gf, sho, sco, nw)


def _deinterleave(n):
    return np.concatenate([np.arange(0, n, 2), np.arange(1, n, 2)])


def _pack_w_in(w_in):
    o = np.cumsum([0, MLA_Q_RANK, MLA_KV_RANK + MLA_ROPE, GLA_HEADS * GLA_DK, GLA_HEADS * GLA_DK,
                   GLA_HEADS * GLA_DV, GLA_GATE_RANK, GLA_HEADS * GLA_DV, D_MODEL, D_MODEL])
    seg = lambda i: w_in[:, o[i]:o[i + 1]]
    kv = seg(1)
    zeros = lambda n: jnp.zeros((D_MODEL, n), w_in.dtype)
    kpe = kv[:, MLA_KV_RANK:][:, _deinterleave(MLA_ROPE)]
    parts = [seg(0), kv[:, :MLA_KV_RANK], kpe, zeros(LANES - MLA_ROPE),
             seg(5), zeros(LANES - GLA_GATE_RANK), seg(2), seg(3), seg(4), seg(6), seg(7), seg(8)]
    return jnp.concatenate(parts, axis=1).astype(BF16)


def _pack_w_q_b(w):
    w = w.reshape(MLA_Q_RANK, MLA_HEADS, MLA_NOPE + MLA_ROPE)
    nope = w[:, :, :MLA_NOPE]
    pe = w[:, :, MLA_NOPE:][:, :, _deinterleave(MLA_ROPE)]
    pad = jnp.zeros((MLA_Q_RANK, MLA_HEADS, MLA_HD - MLA_NOPE - MLA_ROPE), w.dtype)
    return jnp.concatenate([nope, pe, pad], axis=-1).reshape(MLA_Q_RANK, MLA_HEADS * MLA_HD).astype(BF16)


def _rope_consts():
    inv_freq = ROPE_THETA ** (-(jnp.arange(0, MLA_ROPE, 2, dtype=F32) / MLA_ROPE))
    half = MLA_ROPE // 2
    z = jnp.zeros((LANES - MLA_ROPE,), F32)
    rows = [jnp.concatenate([inv_freq, inv_freq, z]),
            jnp.concatenate([-jnp.ones((half,), F32), jnp.zeros((half,), F32), z]),
            jnp.concatenate([jnp.zeros((half,), F32), jnp.ones((half,), F32), z])]
    return jnp.concatenate([jnp.stack(rows), jnp.zeros((5, LANES), F32)], axis=0)


def kernel(x, c, positions, w_ada, b_ada, norm_mix_w, w_in, mla_q_norm_w, mla_w_q_b, mla_kv_norm_w,
           mla_w_kv_b, gla_w_gk_up, gla_b_gk_up, gla_norm_w, w_o_mla, w_o_gla, w_out, norm_ffn_w,
           w_router, b_router, w1, b1, w2, b2, w_ada_final, b_ada_final, norm_final_w):
    assert w_ada.shape[0] == 1, "single-layer stack"
    x2d = x.reshape(N_TOK, D_MODEL)
    c_t = c.T

    mod = _adaln(c_t, w_ada[0], b_ada[0])
    fmod = _adaln(c_t, w_ada_final, b_ada_final)
    sh_m, sc_m, g_m, sh_f, sc_f, g_f = [m.reshape(BATCH, 1, D_MODEL) for m in jnp.split(mod, N_MOD, axis=-1)]
    sh_o, sc_o = [m.reshape(BATCH, 1, D_MODEL) for m in jnp.split(fmod, 2, axis=-1)]

    proj = _inproj(x2d, norm_mix_w[0].reshape(1, D_MODEL), sh_m, sc_m, _pack_w_in(w_in[0]))

    q, k, v = _mla_prep(proj, positions.reshape(N_TOK, 1), _rope_consts(),
                        mla_q_norm_w[0].reshape(1, MLA_Q_RANK), mla_kv_norm_w[0].reshape(1, MLA_KV_RANK),
                        _pack_w_q_b(mla_w_q_b[0]), mla_w_kv_b[0].astype(BF16))
    o_mla = _attention(q, k, v)

    wgk = jnp.concatenate([gla_w_gk_up[0], jnp.zeros((LANES - GLA_GATE_RANK, GLA_HEADS * GLA_DK), F32)], axis=0)
    o_gla = _gla(proj, wgk, gla_b_gk_up[0].reshape(1, GLA_HEADS * GLA_DK), gla_norm_w[0].reshape(1, GLA_DV))

    wr = jnp.concatenate([w_router[0], jnp.zeros((D_MODEL, LANES - N_EXPERTS), F32)], axis=1)
    wr_hi = wr.astype(BF16)
    wr_lo = (wr - wr_hi.astype(F32)).astype(BF16)
    br = jnp.concatenate([b_router[0], jnp.zeros((LANES - N_EXPERTS,), F32)]).reshape(1, LANES)
    x1, h2, top_idx, top_w = _merge(
        x2d, o_mla, o_gla, proj, g_m, sh_f, sc_f, norm_ffn_w[0].reshape(1, D_MODEL),
        w_o_mla[0].astype(BF16), w_o_gla[0].astype(BF16), w_out[0].astype(BF16), wr_hi, wr_lo, br)

    it_e, it_start, it_nblk, slot_src, slot_dst = _route_tables(top_idx[:, :TOP_K])
    y4 = _moe(it_e, it_start, it_nblk, slot_src, slot_dst, h2, w1[0], b1[0], w2[0], b2[0])

    out = _final(x1, y4, top_w, g_f, sh_o, sc_o, norm_final_w.reshape(1, D_MODEL))
    return out.reshape(BATCH, SEQ, D_MODEL)
```

```python
import functools

import jax
import jax.numpy as jnp
import numpy as np
from jax import lax
from jax.experimental import pallas as pl
from jax.experimental.pallas import tpu as pltpu

F32 = jnp.float32
BF16 = jnp.bfloat16

D_MODEL = 2048
BATCH = 2
SEQ = 4096
N_TOK = BATCH * SEQ

MLA_HEADS = 8
MLA_Q_RANK = 512
MLA_KV_RANK = 256
MLA_NOPE = 128
MLA_ROPE = 64
MLA_V = 128
ROPE_THETA = 10000.0
MLA_HD = 256

GLA_HEADS = 4
GLA_DK = 128
GLA_DV = 256
GLA_GATE_RANK = 16
GLA_GATE_NORM = 16.0
GLA_CHUNK = 64

N_EXPERTS = 32
TOP_K = 4
D_EXPERT = D_MODEL
SWIGLU_LIMIT = 7.0
SWIGLU_ALPHA = 1.702
EPS = 1e-6
N_MOD = 6

LANES = 128
ROW_TILES = D_MODEL // LANES
VMEM_LIMIT = 56 * 1024 * 1024

P_QLAT = 0
P_CKV = 512
P_KPE = 768
P_GLR = 896
P_GQ = 1024
P_GK = 1536
P_GV = 2048
P_GOUT = 3072
P_GATEA = 4096
P_GATEB = 6144
P_TOTAL = 8192

MOE_BLK = 256
MOE_ITEM_BLKS = 4
MOE_ITEM_ROWS = MOE_BLK * MOE_ITEM_BLKS
MOE_TH = 256
MOE_NCHUNK = D_EXPERT // MOE_TH
N_ASSIGN = N_TOK * TOP_K
MOE_NBLK_MAX = N_ASSIGN // MOE_BLK + N_EXPERTS
MOE_SLOTS = MOE_NBLK_MAX * MOE_BLK
MOE_NITEMS = N_ASSIGN // MOE_ITEM_ROWS + N_EXPERTS


def _cparams(sem, vmem=VMEM_LIMIT):
    return pltpu.CompilerParams(dimension_semantics=sem, vmem_limit_bytes=vmem)


def _rms(x, w):
    return x * lax.rsqrt(jnp.mean(x * x, axis=-1, keepdims=True) + EPS) * w


def _adaln_kernel(ct_ref, w_ref, b_ref, o_ref):
    ct = ct_ref[...]
    cond = ct * jax.nn.sigmoid(ct)
    w = w_ref[...]
    for b in range(BATCH):
        o_ref[b:b + 1, :] = jnp.sum(w * cond[:, b:b + 1], axis=0, keepdims=True) + b_ref[...]


def _adaln(c_t, w, b, tn=1024):
    d, n = w.shape
    return pl.pallas_call(
        _adaln_kernel,
        grid=(n // tn,),
        in_specs=[pl.BlockSpec((d, BATCH), lambda j: (0, 0)),
                  pl.BlockSpec((d, tn), lambda j: (0, j)),
                  pl.BlockSpec((1, tn), lambda j: (0, j))],
        out_specs=pl.BlockSpec((BATCH, tn), lambda j: (0, j)),
        out_shape=jax.ShapeDtypeStruct((BATCH, n), F32),
        compiler_params=_cparams(("arbitrary",)),
        name="adaln",
    )(c_t, w, b.reshape(1, n))


IN_TM = 1024
IN_TN = 1024


def _inproj_kernel(x_ref, nw_ref, sh_ref, sc_ref, w_ref, o_ref, h_ref):
    @pl.when(pl.program_id(1) == 0)
    def _():
        for r in range(0, IN_TM, 256):
            x = x_ref[r:r + 256, :]
            h = _rms(x, nw_ref[...]) * (1.0 + sc_ref[0]) + sh_ref[0]
            h_ref[r:r + 256, :] = h.astype(BF16)

    o_ref[...] = jnp.dot(h_ref[...], w_ref[...], preferred_element_type=F32).astype(BF16)


def _inproj(x2d, nw, sh, sc, w_packed):
    per_b = SEQ // IN_TM
    return pl.pallas_call(
        _inproj_kernel,
        grid=(N_TOK // IN_TM, P_TOTAL // IN_TN),
        in_specs=[pl.BlockSpec((IN_TM, D_MODEL), lambda i, j: (i, 0)),
                  pl.BlockSpec((1, D_MODEL), lambda i, j: (0, 0)),
                  pl.BlockSpec((1, 1, D_MODEL), lambda i, j: (i // per_b, 0, 0)),
                  pl.BlockSpec((1, 1, D_MODEL), lambda i, j: (i // per_b, 0, 0)),
                  pl.BlockSpec((D_MODEL, IN_TN), lambda i, j: (0, j))],
        out_specs=pl.BlockSpec((IN_TM, IN_TN), lambda i, j: (i, j)),
        out_shape=jax.ShapeDtypeStruct((N_TOK, P_TOTAL), BF16),
        scratch_shapes=[pltpu.VMEM((IN_TM, D_MODEL), BF16)],
        compiler_params=_cparams(("arbitrary", "arbitrary")),
        name="inproj",
    )(x2d, nw, sh, sc, w_packed)


PREP_TM = 512


def _mla_prep_kernel(ql_ref, ckv_ref, kpe_ref, pos_ref, rc_ref, qnw_ref, kvnw_ref, wq_ref, wkv_ref,
                     q_ref, k_ref, v_ref):
    scale = (MLA_NOPE + MLA_ROPE) ** -0.5
    qn = _rms(ql_ref[...].astype(F32), qnw_ref[...]).astype(BF16)
    q = jnp.dot(qn, wq_ref[...], preferred_element_type=F32)
    cn = _rms(ckv_ref[...].astype(F32), kvnw_ref[...]).astype(BF16)
    kv = jnp.dot(cn, wkv_ref[...], preferred_element_type=F32)

    ang = pos_ref[...].astype(F32) * rc_ref[0:1, :]
    cos_t = jnp.cos(ang)
    sin_t = jnp.sin(ang)
    sin_a = sin_t * rc_ref[1:2, :]
    sin_b = sin_t * rc_ref[2:3, :]

    def rope(t):
        return (t * cos_t + pltpu.roll(t, LANES - MLA_ROPE // 2, 1) * sin_a
                + pltpu.roll(t, MLA_ROPE // 2, 1) * sin_b)

    kpe = rope(kpe_ref[...].astype(F32)).astype(BF16)
    for h in range(MLA_HEADS):
        c0 = h * MLA_HD
        q_ref[0, h, :, 0:LANES] = (q[:, c0:c0 + LANES] * scale).astype(BF16)
        q_ref[0, h, :, LANES:MLA_HD] = (rope(q[:, c0 + LANES:c0 + MLA_HD]) * scale).astype(BF16)
        k_ref[0, h, :, 0:LANES] = kv[:, c0:c0 + LANES].astype(BF16)
        k_ref[0, h, :, LANES:MLA_HD] = kpe
        v_ref[0, h, :, :] = kv[:, c0 + LANES:c0 + MLA_HD].astype(BF16)


def _mla_prep(proj, pos_col, rope_consts, qnw, kvnw, wq, wkv):
    tm = PREP_TM
    per_b = SEQ // tm
    row = lambda i: (i, 0)
    const = lambda i: (0, 0)
    out_map = lambda i: (i // per_b, 0, i % per_b, 0)
    return pl.pallas_call(
        _mla_prep_kernel,
        grid=(N_TOK // tm,),
        in_specs=[pl.BlockSpec((tm, MLA_Q_RANK), lambda i: (i, P_QLAT // MLA_Q_RANK)),
                  pl.BlockSpec((tm, MLA_KV_RANK), lambda i: (i, P_CKV // MLA_KV_RANK)),
                  pl.BlockSpec((tm, LANES), lambda i: (i, P_KPE // LANES)),
                  pl.BlockSpec((tm, 1), row),
                  pl.BlockSpec((8, LANES), const),
                  pl.BlockSpec((1, MLA_Q_RANK), const),
                  pl.BlockSpec((1, MLA_KV_RANK), const),
                  pl.BlockSpec((MLA_Q_RANK, MLA_HEADS * MLA_HD), const),
                  pl.BlockSpec((MLA_KV_RANK, MLA_HEADS * MLA_HD), const)],
        out_specs=[pl.BlockSpec((1, MLA_HEADS, tm, MLA_HD), out_map),
                   pl.BlockSpec((1, MLA_HEADS, tm, MLA_HD), out_map),
                   pl.BlockSpec((1, MLA_HEADS, tm, MLA_V), out_map)],
        out_shape=[jax.ShapeDtypeStruct((BATCH, MLA_HEADS, SEQ, MLA_HD), BF16),
                   jax.ShapeDtypeStruct((BATCH, MLA_HEADS, SEQ, MLA_HD), BF16),
                   jax.ShapeDtypeStruct((BATCH, MLA_HEADS, SEQ, MLA_V), BF16)],
        compiler_params=_cparams(("arbitrary",)),
        name="mla_prep",
    )(proj, proj, proj, pos_col, rope_consts, qnw, kvnw, wq, wkv)


ATT_T = 512


def _attn_kernel(q_ref, k_ref, v_ref, o_ref, m_ref, l_ref, acc_ref, s0_ref, s1_ref):
    qi = pl.program_id(2)
    q = q_ref[0, 0]
    m_ref[...] = jnp.full(m_ref.shape, -jnp.inf, F32)
    l_ref[...] = jnp.zeros(l_ref.shape, F32)
    acc_ref[...] = jnp.zeros(acc_ref.shape, F32)

    def scores(j):
        r0 = pl.multiple_of(j * ATT_T, ATT_T)
        k = k_ref[0, 0, pl.ds(r0, ATT_T), :]
        return lax.dot_general(k, q, (((1,), (1,)), ((), ())), preferred_element_type=F32)

    def step(j, src_ref, dst_ref):
        masked = dst_ref is None
        r0 = pl.multiple_of(j * ATT_T, ATT_T)
        v = v_ref[0, 0, pl.ds(r0, ATT_T), :]
        s = src_ref[...]
        if not masked:
            dst_ref[...] = scores(j + 1)
        if masked:
            key = lax.broadcasted_iota(jnp.int32, s.shape, 0)
            qry = lax.broadcasted_iota(jnp.int32, s.shape, 1)
            s = jnp.where(key <= qry, s, -jnp.inf)
        m_old = m_ref[...]
        m_new = jnp.maximum(m_old, jnp.max(s, axis=0, keepdims=True))
        p = jnp.exp(s - m_new)
        alpha = jnp.exp(m_old - m_new)
        l_ref[...] = alpha * l_ref[...] + jnp.sum(p, axis=0, keepdims=True)
        pv = lax.dot_general(v, p.astype(BF16), (((0,), (0,)), ((), ())), preferred_element_type=F32)
        acc_ref[...] = alpha * acc_ref[...] + pv
        m_ref[...] = m_new

    def body(jj, carry):
        step(2 * jj, s0_ref, s1_ref)
        step(2 * jj + 1, s1_ref, s0_ref)
        return carry

    s0_ref[...] = scores(0)
    lax.fori_loop(0, qi // 2, body, 0)

    @pl.when(qi % 2 == 1)
    def _():
        step(qi - 1, s0_ref, s1_ref)
        step(qi, s1_ref, None)

    @pl.when(qi % 2 == 0)
    def _():
        step(qi, s0_ref, None)

    o_ref[...] = jnp.transpose(acc_ref[...] / l_ref[...]).astype(BF16)


def _attention(q, k, v):
    t = ATT_T
    nq = SEQ // t
    return pl.pallas_call(
        _attn_kernel,
        grid=(BATCH, MLA_HEADS, nq),
        in_specs=[pl.BlockSpec((1, 1, t, MLA_HD), lambda b, h, i: (b, h, i, 0)),
                  pl.BlockSpec((1, 1, SEQ, MLA_HD), lambda b, h, i: (b, h, 0, 0)),
                  pl.BlockSpec((1, 1, SEQ, MLA_V), lambda b, h, i: (b, h, 0, 0))],
        out_specs=pl.BlockSpec((t, MLA_V), lambda b, h, i: (b * nq + i, h)),
        out_shape=jax.ShapeDtypeStruct((N_TOK, MLA_HEADS * MLA_V), BF16),
        scratch_shapes=[pltpu.VMEM((1, t), F32), pltpu.VMEM((1, t), F32), pltpu.VMEM((MLA_V, t), F32),
                        pltpu.VMEM((t, t), F32), pltpu.VMEM((t, t), F32)],
        compiler_params=_cparams(("arbitrary", "arbitrary", "arbitrary")),
        name="mla_attn",
    )(q, k, v)


GLA_TG = 512
GLA_DIAG = 8


def _gla_kernel(q_ref, k_ref, v_ref, glr_ref, gout_ref, wgk_ref, bgk_ref, nw_ref, o_ref, st_ref, g_ref):
    C = GLA_CHUNK

    @pl.when(pl.program_id(2) == 0)
    def _():
        st_ref[...] = jnp.zeros(st_ref.shape, F32)

    z = jnp.dot(glr_ref[...].astype(F32), wgk_ref[...], preferred_element_type=F32,
                precision=lax.Precision.HIGHEST) + bgk_ref[...]
    g_ref[...] = jax.nn.log_sigmoid(z) / GLA_GATE_NORM

    row = lax.broadcasted_iota(jnp.int32, (C, GLA_DK), 0)
    ii = lax.broadcasted_iota(jnp.int32, (C, C), 0)
    jj = lax.broadcasted_iota(jnp.int32, (C, C), 1)
    nt = (((1,), (1,)), ((), ()))

    def chunk(ci, carry):
        r0 = pl.multiple_of(ci * C, C)
        b = g_ref[pl.ds(r0, C), :]
        s = 1
        while s < C:
            b = b + jnp.where(row >= s, pltpu.roll(b, s, 0), 0.0)
            s *= 2
        q = q_ref[pl.ds(r0, C), :].astype(F32) * (GLA_DK ** -0.5)
        k = k_ref[pl.ds(r0, C), :].astype(F32)
        v = v_ref[pl.ds(r0, C), :]
        st = st_ref[...]

        o = lax.dot_general((q * jnp.exp(b)).astype(BF16), st.astype(BF16), nt,
                            preferred_element_type=F32)

        a = jnp.zeros((C, C), F32)
        lvl = C // 2
        while lvl >= GLA_DIAG:
            pieces = [jnp.broadcast_to(b[t + lvl:t + lvl + 1, :], (2 * lvl, GLA_DK))
                      for t in range(0, C, 2 * lvl)]
            m = pieces[0] if len(pieces) == 1 else jnp.concatenate(pieces, axis=0)
            odd = (row // lvl) % 2 == 1
            ql = jnp.where(odd, q * jnp.exp(jnp.minimum(b - m, 0.0)), 0.0).astype(BF16)
            kl = jnp.where(odd, 0.0, k * jnp.exp(jnp.minimum(m - b, 0.0))).astype(BF16)
            al = lax.dot_general(ql, kl, nt, preferred_element_type=F32)
            a = a + jnp.where(ii // (2 * lvl) == jj // (2 * lvl), al, 0.0)
            lvl //= 2
        for d in range(GLA_DIAG):
            if d == 0:
                t = q * k
            else:
                t = q * pltpu.roll(k, d, 0) * jnp.exp(jnp.minimum(b - pltpu.roll(b, d, 0), 0.0))
            col = jnp.sum(t, axis=-1, keepdims=True)
            a = a + jnp.where((ii - jj == d) & (ii % GLA_DIAG >= d), col, 0.0)
        o = o + jnp.dot(a.astype(BF16), v, preferred_element_type=F32)

        b_last = b[C - 1:C, :]
        kdec = (k * jnp.exp(b_last - b)).astype(BF16)
        st_ref[...] = st * jnp.exp(b_last) + lax.dot_general(
            v, kdec, (((0,), (0,)), ((), ())), preferred_element_type=F32)

        gate = gout_ref[pl.ds(r0, C), :].astype(F32)
        y = _rms(o, nw_ref[...]) * (gate * jax.nn.sigmoid(gate))
        o_ref[pl.ds(r0, C), :] = y.astype(BF16)
        return carry

    lax.fori_loop(0, GLA_TG // C, chunk, 0)


def _gla(proj, wgk, bgk, nw):
    tg = GLA_TG
    per_b = SEQ // tg
    rb = lambda b, h, t: b * per_b + t
    return pl.pallas_call(
        _gla_kernel,
        grid=(BATCH, GLA_HEADS, per_b),
        in_specs=[pl.BlockSpec((tg, GLA_DK), lambda b, h, t: (rb(b, h, t), P_GQ // GLA_DK + h)),
                  pl.BlockSpec((tg, GLA_DK), lambda b, h, t: (rb(b, h, t), P_GK // GLA_DK + h)),
                  pl.BlockSpec((tg, GLA_DV), lambda b, h, t: (rb(b, h, t), P_GV // GLA_DV + h)),
                  pl.BlockSpec((tg, LANES), lambda b, h, t: (rb(b, h, t), P_GLR // LANES)),
                  pl.BlockSpec((tg, GLA_DV), lambda b, h, t: (rb(b, h, t), P_GOUT // GLA_DV + h)),
                  pl.BlockSpec((LANES, GLA_DK), lambda b, h, t: (0, h)),
                  pl.BlockSpec((1, GLA_DK), lambda b, h, t: (0, h)),
                  pl.BlockSpec((1, GLA_DV), lambda b, h, t: (0, 0))],
        out_specs=pl.BlockSpec((tg, GLA_DV), lambda b, h, t: (rb(b, h, t), h)),
        out_shape=jax.ShapeDtypeStruct((N_TOK, GLA_HEADS * GLA_DV), BF16),
        scratch_shapes=[pltpu.VMEM((GLA_DV, GLA_DK), F32), pltpu.VMEM((tg, GLA_DK), F32)],
        compiler_params=_cparams(("arbitrary", "arbitrary", "arbitrary")),
        name="gla",
    )(proj, proj, proj, proj, proj, wgk, bgk, nw)


MRG_TM = 256


def _merge_kernel(x_ref, oa_ref, ob_ref, ga_ref, gb_ref, gm_ref, shf_ref, scf_ref, nfw_ref,
                  woa_ref, wob_ref, wout_ref, wrh_ref, wrl_ref, br_ref,
                  x1_ref, h2_ref, idx_ref, tw_ref):
    ya = jnp.dot(oa_ref[...], woa_ref[...], preferred_element_type=F32)
    yb = jnp.dot(ob_ref[...], wob_ref[...], preferred_element_type=F32)
    merged = (jax.nn.sigmoid(ga_ref[...].astype(F32)) * ya
              + jax.nn.sigmoid(gb_ref[...].astype(F32)) * yb).astype(BF16)
    x1 = x_ref[...] + gm_ref[0] * jnp.dot(merged, wout_ref[...], preferred_element_type=F32)
    x1_ref[...] = x1
    h2 = _rms(x1, nfw_ref[...]) * (1.0 + scf_ref[0]) + shf_ref[0]
    for j in range(ROW_TILES):
        h2_ref[pl.ds(j, MRG_TM, stride=ROW_TILES), :] = h2[:, j * LANES:(j + 1) * LANES]

    hi = h2.astype(BF16)
    lo = (h2 - hi.astype(F32)).astype(BF16)
    logits = (jnp.dot(hi, wrh_ref[...], preferred_element_type=F32)
              + jnp.dot(hi, wrl_ref[...], preferred_element_type=F32)
              + jnp.dot(lo, wrh_ref[...], preferred_element_type=F32)) + br_ref[...]
    lane = lax.broadcasted_iota(jnp.int32, logits.shape, 1)
    vals = jnp.where(lane < N_EXPERTS, logits, -jnp.inf)
    idx_out = jnp.zeros(logits.shape, jnp.int32)
    w_out = jnp.zeros(logits.shape, F32)
    top = None
    denom = None
    for kk in range(TOP_K):
        m = jnp.max(vals, axis=-1, keepdims=True)
        sel = jnp.min(jnp.where(vals == m, lane, LANES), axis=-1, keepdims=True)
        if kk == 0:
            top = m
        e = jnp.exp(m - top)
        denom = e if kk == 0 else denom + e
        idx_out = jnp.where(lane == kk, sel, idx_out)
        w_out = jnp.where(lane == kk, e, w_out)
        vals = jnp.where(lane == sel, -jnp.inf, vals)
    idx_ref[...] = idx_out
    tw_ref[...] = w_out / denom


def _merge(x2d, o_mla, o_gla, proj, gm, shf, scf, nfw, woa, wob, wout, wrh, wrl, br):
    tm = MRG_TM
    per_b = SEQ // tm
    row = lambda i: (i, 0)
    const = lambda i: (0, 0)
    bvec = lambda i: (i // per_b, 0, 0)
    single = dict(pipeline_mode=pl.Buffered(1))
    return pl.pallas_call(
        _merge_kernel,
        grid=(N_TOK // tm,),
        in_specs=[pl.BlockSpec((tm, D_MODEL), row),
                  pl.BlockSpec((tm, MLA_HEADS * MLA_V), row),
                  pl.BlockSpec((tm, GLA_HEADS * GLA_DV), row),
                  pl.BlockSpec((tm, D_MODEL), lambda i: (i, P_GATEA // D_MODEL)),
                  pl.BlockSpec((tm, D_MODEL), lambda i: (i, P_GATEB // D_MODEL)),
                  pl.BlockSpec((1, 1, D_MODEL), bvec),
                  pl.BlockSpec((1, 1, D_MODEL), bvec),
                  pl.BlockSpec((1, 1, D_MODEL), bvec),
                  pl.BlockSpec((1, D_MODEL), const),
                  pl.BlockSpec((MLA_HEADS * MLA_V, D_MODEL), const, **single),
                  pl.BlockSpec((GLA_HEADS * GLA_DV, D_MODEL), const, **single),
                  pl.BlockSpec((D_MODEL, D_MODEL), const, **single),
                  pl.BlockSpec((D_MODEL, LANES), const, **single),
                  pl.BlockSpec((D_MODEL, LANES), const, **single),
                  pl.BlockSpec((1, LANES), const)],
        out_specs=[pl.BlockSpec((tm, D_MODEL), row),
                   pl.BlockSpec((tm * ROW_TILES, LANES), row),
                   pl.BlockSpec((tm, LANES), row),
                   pl.BlockSpec((tm, LANES), row)],
        out_shape=[jax.ShapeDtypeStruct((N_TOK, D_MODEL), F32),
                   jax.ShapeDtypeStruct((N_TOK * ROW_TILES, LANES), F32),
                   jax.ShapeDtypeStruct((N_TOK, LANES), jnp.int32),
                   jax.ShapeDtypeStruct((N_TOK, LANES), F32)],
        compiler_params=_cparams(("arbitrary",)),
        name="merge_router",
    )(x2d, o_mla, o_gla, proj, proj, gm, shf, scf, nfw, woa, wob, wout, wrh, wrl, br)


MOE_PAIRS = MOE_ITEM_BLKS // 2
ROW_COPY_PRIORITY = 1
MOE_QSTEPS = MOE_NCHUNK // 2
MOE_QUOTA = MOE_ITEM_ROWS // (MOE_QSTEPS * MOE_PAIRS)


def _moe_kernel(it_e_ref, it_start_ref, it_nblk_ref, src_ref, dst_ref,
                h2_hbm, w1_ref, b1_ref, w2_ref, b2_ref, sel_ref, y4_hbm,
                stage_ref, xb_ref, acc_ref, yout_ref, w1b_ref, w2b_ref, gsem, ssem):
    it = pl.program_id(0)
    c = pl.program_id(1)
    nblk = it_nblk_ref[it]
    prev = jnp.maximum(it - 1, 0)
    nblk_prev = jnp.where(it > 0, it_nblk_ref[prev], 0)
    live = nblk > 0
    prev_live = nblk_prev > 0
    rows_prev = nblk_prev * MOE_BLK
    rows_next = it_nblk_ref[it + 1] * MOE_BLK
    base_prev = it_start_ref[prev]
    base_next = it_start_ref[it + 1]

    def token_rows(t):
        return pl.ds(pl.multiple_of(t * ROW_TILES, ROW_TILES), ROW_TILES)

    def gather_row(r, base, nrows):
        tok = jnp.where(r < nrows, src_ref[jnp.minimum(base + r, MOE_SLOTS - 1)], 0)
        pltpu.make_async_copy(h2_hbm.at[token_rows(tok), :], stage_ref.at[token_rows(r), :],
                              gsem).start(priority=ROW_COPY_PRIORITY)

    def scatter_row(r):
        d = dst_ref[jnp.minimum(base_prev + r, MOE_SLOTS - 1)]
        d = jnp.where(jnp.logical_and(r < rows_prev, d >= 0), d, N_ASSIGN + r)
        pltpu.make_async_copy(yout_ref.at[token_rows(r), :], y4_hbm.at[token_rows(d), :],
                              ssem).start(priority=ROW_COPY_PRIORITY)

    def issue_quota(pair, gather=True):
        r0 = (c * MOE_PAIRS + pair) * MOE_QUOTA
        for u in range(MOE_QUOTA):
            if gather:
                gather_row(r0 + u, base_next, rows_next)
            scatter_row(r0 + u)

    @pl.when(jnp.logical_and(it == 0, c == 0))
    def _():
        yout_ref[...] = jnp.zeros(yout_ref.shape, F32)

        def first(r, carry):
            gather_row(r, it_start_ref[0], nblk * MOE_BLK)
            return carry
        lax.fori_loop(0, MOE_ITEM_ROWS, first, 0)

    @pl.when(jnp.logical_and(c == 0, jnp.logical_or(it == 0, prev_live)))
    def _():
        pltpu.make_async_copy(h2_hbm.at[pl.ds(0, MOE_ITEM_ROWS * ROW_TILES), :], stage_ref, gsem).wait()

    @pl.when(jnp.logical_and(c == 0, live))
    def _():
        for s in range(MOE_ITEM_BLKS):
            @pl.when(s < nblk)
            def _():
                rows = slice(s * MOE_BLK, (s + 1) * MOE_BLK)
                for j in range(ROW_TILES):
                    piece = stage_ref[pl.ds(s * MOE_BLK * ROW_TILES + j, MOE_BLK, stride=ROW_TILES), :]
                    xb_ref[rows, j * LANES:(j + 1) * LANES] = piece.astype(BF16)
                acc_ref[rows, :] = jnp.broadcast_to(b2_ref[0], (MOE_BLK, D_MODEL))

    def compute(row0, nrows):
        rows = slice(row0, row0 + nrows)
        hm = jnp.dot(xb_ref[rows, :], w1b_ref[...], preferred_element_type=F32) + b1_ref[0]
        lin = pltpu.roll(hm, 2 * MOE_TH - 1, 1)
        glu = jnp.minimum(hm, SWIGLU_LIMIT)
        lin = jnp.clip(lin, -SWIGLU_LIMIT, SWIGLU_LIMIT)
        act = glu * jax.nn.sigmoid(SWIGLU_ALPHA * glu) * (lin + 1.0)
        lane = lax.broadcasted_iota(jnp.int32, act.shape, 1)
        act = jnp.where(lane % 2 == 0, act, 0.0).astype(BF16)
        act = jnp.dot(act, sel_ref[...], preferred_element_type=F32).astype(BF16)
        acc_ref[rows, :] += jnp.dot(act, w2b_ref[...], preferred_element_type=F32)

    @pl.when(live)
    def _():
        w1b_ref[...] = w1_ref[0].astype(BF16)
        w2b_ref[...] = w2_ref[0].astype(BF16)

    early = c < MOE_QSTEPS
    for pair in range(MOE_PAIRS):
        in_pair = jnp.clip(nblk - 2 * pair, 0, 2)
        for nb in (2, 1):
            @pl.when(jnp.logical_and(in_pair == nb, early))
            def _():
                compute(2 * pair * MOE_BLK, nb * MOE_BLK)
                issue_quota(pair)

            @pl.when(jnp.logical_and(in_pair == nb, jnp.logical_not(early)))
            def _():
                compute(2 * pair * MOE_BLK, nb * MOE_BLK)

        @pl.when(jnp.logical_and(jnp.logical_and(in_pair == 0, live), early))
        def _():
            issue_quota(pair)

        @pl.when(jnp.logical_and(jnp.logical_and(jnp.logical_not(live), prev_live), early))
        def _():
            issue_quota(pair, gather=False)

    @pl.when(jnp.logical_and(c == MOE_NCHUNK - 1, jnp.logical_or(live, prev_live)))
    def _():
        pltpu.make_async_copy(yout_ref, y4_hbm.at[pl.ds(0, MOE_ITEM_ROWS * ROW_TILES), :], ssem).wait()

    @pl.when(jnp.logical_and(c == MOE_NCHUNK - 1, live))
    def _():
        for s in range(MOE_ITEM_BLKS):
            @pl.when(s < nblk)
            def _():
                rows = slice(s * MOE_BLK, (s + 1) * MOE_BLK)
                for j in range(ROW_TILES):
                    yout_ref[pl.ds(s * MOE_BLK * ROW_TILES + j, MOE_BLK, stride=ROW_TILES), :] = (
                        acc_ref[rows, j * LANES:(j + 1) * LANES])


def _moe(it_e, it_start, it_nblk, slot_src, slot_dst, h2, w1, b1, w2, b2):
    def chunk_of(it, c, n_ref):
        return jnp.where(n_ref[it] > 0, c, MOE_NCHUNK - 1)

    def w1_map(it, c, e_ref, s_ref, n_ref, *_):
        return (e_ref[it], 0, chunk_of(it, c, n_ref))

    def w2_map(it, c, e_ref, s_ref, n_ref, *_):
        return (e_ref[it], chunk_of(it, c, n_ref), 0)

    def b2_map(it, c, e_ref, *_):
        return (e_ref[it], 0, 0)

    sel = (jnp.arange(2 * MOE_TH, dtype=jnp.int32)[:, None]
           == 2 * jnp.arange(MOE_TH, dtype=jnp.int32)[None, :]).astype(BF16)

    grid_spec = pltpu.PrefetchScalarGridSpec(
        num_scalar_prefetch=5,
        grid=(MOE_NITEMS + 1, MOE_NCHUNK),
        in_specs=[pl.BlockSpec(memory_space=pl.ANY),
                  pl.BlockSpec((1, D_MODEL, 2 * MOE_TH), w1_map),
                  pl.BlockSpec((1, 1, 2 * MOE_TH), w1_map),
                  pl.BlockSpec((1, MOE_TH, D_MODEL), w2_map),
                  pl.BlockSpec((1, 1, D_MODEL), b2_map),
                  pl.BlockSpec((2 * MOE_TH, MOE_TH), lambda it, c, *_: (0, 0))],
        out_specs=pl.BlockSpec(memory_space=pl.ANY),
        scratch_shapes=[pltpu.VMEM((MOE_ITEM_ROWS * ROW_TILES, LANES), F32),
                        pltpu.VMEM((MOE_ITEM_ROWS, D_MODEL), BF16),
                        pltpu.VMEM((MOE_ITEM_ROWS, D_MODEL), F32),
                        pltpu.VMEM((MOE_ITEM_ROWS * ROW_TILES, LANES), F32),
                        pltpu.VMEM((D_MODEL, 2 * MOE_TH), BF16),
                        pltpu.VMEM((MOE_TH, D_MODEL), BF16),
                        pltpu.SemaphoreType.DMA(()),
                        pltpu.SemaphoreType.DMA(())],
    )
    return pl.pallas_call(
        _moe_kernel,
        grid_spec=grid_spec,
        out_shape=jax.ShapeDtypeStruct(((N_ASSIGN + MOE_ITEM_ROWS) * ROW_TILES, LANES), F32),
        compiler_params=_cparams(("arbitrary", "arbitrary")),
        name="moe_experts",
    )(it_e, it_start, it_nblk, slot_src, slot_dst, h2, w1,
      b1.reshape(N_EXPERTS, 1, 2 * D_EXPERT), w2, b2.reshape(N_EXPERTS, 1, D_MODEL), sel)


def _route_tables(top_idx):
    e_flat = top_idx.reshape(N_ASSIGN)
    onehot = (e_flat[:, None] == jnp.arange(N_EXPERTS, dtype=jnp.int32)[None, :]).astype(jnp.int32)
    csum = jnp.cumsum(onehot, axis=0)
    rank = jnp.sum(csum * onehot, axis=-1) - 1
    counts = csum[-1]
    nblk = (counts + MOE_BLK - 1) // MOE_BLK
    blk_end = jnp.cumsum(nblk)
    blk_start = blk_end - nblk
    dest = blk_start[e_flat] * MOE_BLK + rank
    slot_a = jnp.full((MOE_SLOTS,), -1, jnp.int32).at[dest].set(jnp.arange(N_ASSIGN, dtype=jnp.int32))
    slot_src = jnp.where(slot_a >= 0, slot_a // TOP_K, 0)
    slot_dst = jnp.where(slot_a >= 0, (slot_a % TOP_K) * N_TOK + slot_a // TOP_K, -1)

    nit = (nblk + MOE_ITEM_BLKS - 1) // MOE_ITEM_BLKS
    it_end = jnp.cumsum(nit)
    it_begin = it_end - nit
    n_used = it_end[-1]
    ids = jnp.arange(MOE_NITEMS + 2, dtype=jnp.int32)
    valid = ids < n_used
    last = jnp.minimum(ids, n_used - 1)
    e_of = jnp.minimum(jnp.sum((it_end[None, :] <= last[:, None]).astype(jnp.int32), axis=1), N_EXPERTS - 1)
    local = ids - it_begin[e_of]
    it_start = jnp.where(valid, (blk_start[e_of] + local * MOE_ITEM_BLKS) * MOE_BLK, 0).astype(jnp.int32)
    it_nblk = jnp.where(valid, jnp.clip(nblk[e_of] - local * MOE_ITEM_BLKS, 0, MOE_ITEM_BLKS), 0)
    return e_of, it_start, it_nblk.astype(jnp.int32), slot_src, slot_dst


FIN_TM = 256


def _final_kernel(x1_ref, ya_ref, yb_ref, yc_ref, yd_ref, tw_ref, gf_ref, sho_ref, sco_ref, nw_ref, o_ref):
    tw = tw_ref[...]
    pieces = []
    for j in range(ROW_TILES):
        acc = None
        for kk, y_ref in enumerate((ya_ref, yb_ref, yc_ref, yd_ref)):
            term = tw[:, kk:kk + 1] * y_ref[pl.ds(j, FIN_TM, stride=ROW_TILES), :]
            acc = term if acc is None else acc + term
        pieces.append(acc)
    moe = jnp.concatenate(pieces, axis=1)
    x2 = x1_ref[...] + gf_ref[0] * moe
    o_ref[...] = _rms(x2, nw_ref[...]) * (1.0 + sco_ref[0]) + sho_ref[0]


def _final(x1, y4, tw, gf, sho, sco, nw):
    tm = FIN_TM
    per_b = SEQ // tm
    row = lambda i: (i, 0)
    bvec = lambda i: (i // per_b, 0, 0)
    nb = N_TOK // tm
    kth = lambda kk: pl.BlockSpec((tm * ROW_TILES, LANES), lambda i: (kk * nb + i, 0))
    return pl.pallas_call(
        _final_kernel,
        grid=(nb,),
        in_specs=[pl.BlockSpec((tm, D_MODEL), row),
                  kth(0), kth(1), kth(2), kth(3),
                  pl.BlockSpec((tm, LANES), row),
                  pl.BlockSpec((1, 1, D_MODEL), bvec),
                  pl.BlockSpec((1, 1, D_MODEL), bvec),
                  pl.BlockSpec((1, 1, D_MODEL), bvec),
                  pl.BlockSpec((1, D_MODEL), lambda i: (0, 0))],
        out_specs=pl.BlockSpec((tm, D_MODEL), row),
        out_shape=jax.ShapeDtypeStruct((N_TOK, D_MODEL), F32),
        compiler_params=_cparams(("arbitrary",)),
        name="combine_final",
    )(x1, y4, y4, y4, y4, tw, gf, sho, sco, nw)


def _deinterleave(n):
    return np.concatenate([np.arange(0, n, 2), np.arange(1, n, 2)])


_IN_OFFS = np.cumsum([0, MLA_Q_RANK, MLA_KV_RANK + MLA_ROPE, GLA_HEADS * GLA_DK, GLA_HEADS * GLA_DK,
                      GLA_HEADS * GLA_DV, GLA_GATE_RANK, GLA_HEADS * GLA_DV, D_MODEL, D_MODEL])
_IN_MOVES = ((_IN_OFFS[0], MLA_Q_RANK, P_QLAT), (_IN_OFFS[1], MLA_KV_RANK, P_CKV),
             (_IN_OFFS[5], GLA_GATE_RANK, P_GLR), (_IN_OFFS[2], GLA_HEADS * GLA_DK, P_GQ),
             (_IN_OFFS[3], GLA_HEADS * GLA_DK, P_GK), (_IN_OFFS[4], GLA_HEADS * GLA_DV, P_GV),
             (_IN_OFFS[6], GLA_HEADS * GLA_DV, P_GOUT), (_IN_OFFS[7], D_MODEL, P_GATEA),
             (_IN_OFFS[8], D_MODEL, P_GATEB))
PACK_TM = 256


def _pack_kernel(w_ref, kpe_ref, o_ref):
    o_ref[:, P_KPE:P_KPE + LANES] = kpe_ref[...]
    o_ref[:, P_GLR:P_GLR + LANES] = jnp.zeros((PACK_TM, LANES), BF16)
    for src, width, dst in _IN_MOVES:
        o_ref[:, dst:dst + width] = w_ref[:, int(src):int(src) + width].astype(BF16)


def _pack_w_in(w_in):
    d_in = w_in.shape[1]
    kpe = w_in[:, _IN_OFFS[1] + MLA_KV_RANK:_IN_OFFS[2]][:, _deinterleave(MLA_ROPE)]
    kpe = jnp.concatenate([kpe, jnp.zeros((D_MODEL, LANES - MLA_ROPE), w_in.dtype)], axis=1).astype(BF16)
    return pl.pallas_call(
        _pack_kernel,
        grid=(D_MODEL // PACK_TM,),
        in_specs=[pl.BlockSpec((PACK_TM, d_in), lambda i: (i, 0)),
                  pl.BlockSpec((PACK_TM, LANES), lambda i: (i, 0))],
        out_specs=pl.BlockSpec((PACK_TM, P_TOTAL), lambda i: (i, 0)),
        out_shape=jax.ShapeDtypeStruct((D_MODEL, P_TOTAL), BF16),
        compiler_params=_cparams(("arbitrary",)),
        name="pack_w_in",
    )(w_in, kpe)


def _pack_w_q_b(w):
    w = w.reshape(MLA_Q_RANK, MLA_HEADS, MLA_NOPE + MLA_ROPE)
    nope = w[:, :, :MLA_NOPE]
    pe = w[:, :, MLA_NOPE:][:, :, _deinterleave(MLA_ROPE)]
    pad = jnp.zeros((MLA_Q_RANK, MLA_HEADS, MLA_HD - MLA_NOPE - MLA_ROPE), w.dtype)
    return jnp.concatenate([nope, pe, pad], axis=-1).reshape(MLA_Q_RANK, MLA_HEADS * MLA_HD).astype(BF16)


def _rope_consts():
    inv_freq = ROPE_THETA ** (-(jnp.arange(0, MLA_ROPE, 2, dtype=F32) / MLA_ROPE))
    half = MLA_ROPE // 2
    z = jnp.zeros((LANES - MLA_ROPE,), F32)
    rows = [jnp.concatenate([inv_freq, inv_freq, z]),
            jnp.concatenate([-jnp.ones((half,), F32), jnp.zeros((half,), F32), z]),
            jnp.concatenate([jnp.zeros((half,), F32), jnp.ones((half,), F32), z])]
    return jnp.concatenate([jnp.stack(rows), jnp.zeros((5, LANES), F32)], axis=0)


def kernel(x, c, positions, w_ada, b_ada, norm_mix_w, w_in, mla_q_norm_w, mla_w_q_b, mla_kv_norm_w,
           mla_w_kv_b, gla_w_gk_up, gla_b_gk_up, gla_norm_w, w_o_mla, w_o_gla, w_out, norm_ffn_w,
           w_router, b_router, w1, b1, w2, b2, w_ada_final, b_ada_final, norm_final_w):
    assert w_ada.shape[0] == 1, "single-layer stack"
    x2d = x.reshape(N_TOK, D_MODEL)
    c_t = c.T

    mod = _adaln(c_t, w_ada[0], b_ada[0])
    fmod = _adaln(c_t, w_ada_final, b_ada_final)
    sh_m, sc_m, g_m, sh_f, sc_f, g_f = [m.reshape(BATCH, 1, D_MODEL) for m in jnp.split(mod, N_MOD, axis=-1)]
    sh_o, sc_o = [m.reshape(BATCH, 1, D_MODEL) for m in jnp.split(fmod, 2, axis=-1)]

    proj = _inproj(x2d, norm_mix_w[0].reshape(1, D_MODEL), sh_m, sc_m, _pack_w_in(w_in[0]))

    q, k, v = _mla_prep(proj, positions.reshape(N_TOK, 1), _rope_consts(),
                        mla_q_norm_w[0].reshape(1, MLA_Q_RANK), mla_kv_norm_w[0].reshape(1, MLA_KV_RANK),
                        _pack_w_q_b(mla_w_q_b[0]), mla_w_kv_b[0].astype(BF16))
    o_mla = _attention(q, k, v)

    wgk = jnp.concatenate([gla_w_gk_up[0], jnp.zeros((LANES - GLA_GATE_RANK, GLA_HEADS * GLA_DK), F32)], axis=0)
    o_gla = _gla(proj, wgk, gla_b_gk_up[0].reshape(1, GLA_HEADS * GLA_DK), gla_norm_w[0].reshape(1, GLA_DV))

    wr = jnp.concatenate([w_router[0], jnp.zeros((D_MODEL, LANES - N_EXPERTS), F32)], axis=1)
    wr_hi = wr.astype(BF16)
    wr_lo = (wr - wr_hi.astype(F32)).astype(BF16)
    br = jnp.concatenate([b_router[0], jnp.zeros((LANES - N_EXPERTS,), F32)]).reshape(1, LANES)
    x1, h2, top_idx, top_w = _merge(
        x2d, o_mla, o_gla, proj, g_m, sh_f, sc_f, norm_ffn_w[0].reshape(1, D_MODEL),
        w_o_mla[0].astype(BF16), w_o_gla[0].astype(BF16), w_out[0].astype(BF16), wr_hi, wr_lo, br)

    it_e, it_start, it_nblk, slot_src, slot_dst = _route_tables(top_idx[:, :TOP_K])
    y4 = _moe(it_e, it_start, it_nblk, slot_src, slot_dst, h2, w1[0], b1[0], w2[0], b2[0])

    out = _final(x1, y4, top_w, g_f, sh_o, sc_o, norm_final_w.reshape(1, D_MODEL))
    return out.reshape(BATCH, SEQ, D_MODEL)
```

```python
import functools

import jax
import jax.numpy as jnp
import numpy as np
from jax import lax
from jax.experimental import pallas as pl
from jax.experimental.pallas import tpu as pltpu

F32 = jnp.float32
BF16 = jnp.bfloat16

D_MODEL = 2048
BATCH = 2
SEQ = 4096
N_TOK = BATCH * SEQ

MLA_HEADS = 8
MLA_Q_RANK = 512
MLA_KV_RANK = 256
MLA_NOPE = 128
MLA_ROPE = 64
MLA_V = 128
ROPE_THETA = 10000.0
MLA_HD = 256

GLA_HEADS = 4
GLA_DK = 128
GLA_DV = 256
GLA_GATE_RANK = 16
GLA_GATE_NORM = 16.0
GLA_CHUNK = 64

N_EXPERTS = 32
TOP_K = 4
D_EXPERT = D_MODEL
SWIGLU_LIMIT = 7.0
SWIGLU_ALPHA = 1.702
EPS = 1e-6
N_MOD = 6

LANES = 128
ROW_TILES = D_MODEL // LANES
VMEM_LIMIT = 56 * 1024 * 1024

P_QLAT = 0
P_CKV = 512
P_KPE = 768
P_GLR = 896
P_GQ = 1024
P_GK = 1536
P_GV = 2048
P_GOUT = 3072
P_GATEA = 4096
P_GATEB = 6144
P_TOTAL = 8192

MOE_BLK = 256
MOE_ITEM_BLKS = 5
MOE_ITEM_ROWS = MOE_BLK * MOE_ITEM_BLKS
MOE_TH = 256
MOE_NCHUNK = D_EXPERT // MOE_TH
N_ASSIGN = N_TOK * TOP_K
MOE_NBLK_MAX = N_ASSIGN // MOE_BLK + N_EXPERTS
MOE_SLOTS = MOE_NBLK_MAX * MOE_BLK
MOE_NITEMS = N_ASSIGN // MOE_ITEM_ROWS + N_EXPERTS


def _cparams(sem, vmem=VMEM_LIMIT):
    return pltpu.CompilerParams(dimension_semantics=sem, vmem_limit_bytes=vmem)


def _rms(x, w):
    return x * lax.rsqrt(jnp.mean(x * x, axis=-1, keepdims=True) + EPS) * w


def _adaln_kernel(ct_ref, w_ref, b_ref, o_ref):
    ct = ct_ref[...]
    cond = ct * jax.nn.sigmoid(ct)
    w = w_ref[...]
    for b in range(BATCH):
        o_ref[b:b + 1, :] = jnp.sum(w * cond[:, b:b + 1], axis=0, keepdims=True) + b_ref[...]


def _adaln(c_t, w, b, tn=1024):
    d, n = w.shape
    return pl.pallas_call(
        _adaln_kernel,
        grid=(n // tn,),
        in_specs=[pl.BlockSpec((d, BATCH), lambda j: (0, 0)),
                  pl.BlockSpec((d, tn), lambda j: (0, j)),
                  pl.BlockSpec((1, tn), lambda j: (0, j))],
        out_specs=pl.BlockSpec((BATCH, tn), lambda j: (0, j)),
        out_shape=jax.ShapeDtypeStruct((BATCH, n), F32),
        compiler_params=_cparams(("arbitrary",)),
        name="adaln",
    )(c_t, w, b.reshape(1, n))


IN_TM = 1024
IN_TN = 1024


def _inproj_kernel(x_ref, nw_ref, sh_ref, sc_ref, w_ref, o_ref, h_ref):
    @pl.when(pl.program_id(1) == 0)
    def _():
        for r in range(0, IN_TM, 256):
            x = x_ref[r:r + 256, :]
            h = _rms(x, nw_ref[...]) * (1.0 + sc_ref[0]) + sh_ref[0]
            h_ref[r:r + 256, :] = h.astype(BF16)

    o_ref[...] = jnp.dot(h_ref[...], w_ref[...], preferred_element_type=F32).astype(BF16)


def _inproj(x2d, nw, sh, sc, w_packed):
    per_b = SEQ // IN_TM
    return pl.pallas_call(
        _inproj_kernel,
        grid=(N_TOK // IN_TM, P_TOTAL // IN_TN),
        in_specs=[pl.BlockSpec((IN_TM, D_MODEL), lambda i, j: (i, 0)),
                  pl.BlockSpec((1, D_MODEL), lambda i, j: (0, 0)),
                  pl.BlockSpec((1, 1, D_MODEL), lambda i, j: (i // per_b, 0, 0)),
                  pl.BlockSpec((1, 1, D_MODEL), lambda i, j: (i // per_b, 0, 0)),
                  pl.BlockSpec((D_MODEL, IN_TN), lambda i, j: (0, j))],
        out_specs=pl.BlockSpec((IN_TM, IN_TN), lambda i, j: (i, j)),
        out_shape=jax.ShapeDtypeStruct((N_TOK, P_TOTAL), BF16),
        scratch_shapes=[pltpu.VMEM((IN_TM, D_MODEL), BF16)],
        compiler_params=_cparams(("arbitrary", "arbitrary")),
        name="inproj",
    )(x2d, nw, sh, sc, w_packed)


PREP_TM = 512


def _mla_prep_kernel(ql_ref, ckv_ref, kpe_ref, pos_ref, rc_ref, qnw_ref, kvnw_ref, wq_ref, wkv_ref,
                     q_ref, k_ref, v_ref):
    scale = (MLA_NOPE + MLA_ROPE) ** -0.5
    qn = _rms(ql_ref[...].astype(F32), qnw_ref[...]).astype(BF16)
    q = jnp.dot(qn, wq_ref[...], preferred_element_type=F32)
    cn = _rms(ckv_ref[...].astype(F32), kvnw_ref[...]).astype(BF16)
    kv = jnp.dot(cn, wkv_ref[...], preferred_element_type=F32)

    ang = pos_ref[...].astype(F32) * rc_ref[0:1, :]
    cos_t = jnp.cos(ang)
    sin_t = jnp.sin(ang)
    sin_a = sin_t * rc_ref[1:2, :]
    sin_b = sin_t * rc_ref[2:3, :]

    def rope(t):
        return (t * cos_t + pltpu.roll(t, LANES - MLA_ROPE // 2, 1) * sin_a
                + pltpu.roll(t, MLA_ROPE // 2, 1) * sin_b)

    kpe = rope(kpe_ref[...].astype(F32)).astype(BF16)
    for h in range(MLA_HEADS):
        c0 = h * MLA_HD
        q_ref[0, h, :, 0:LANES] = (q[:, c0:c0 + LANES] * scale).astype(BF16)
        q_ref[0, h, :, LANES:MLA_HD] = (rope(q[:, c0 + LANES:c0 + MLA_HD]) * scale).astype(BF16)
        k_ref[0, h, :, 0:LANES] = kv[:, c0:c0 + LANES].astype(BF16)
        k_ref[0, h, :, LANES:MLA_HD] = kpe
        v_ref[0, h, :, :] = kv[:, c0 + LANES:c0 + MLA_HD].astype(BF16)


def _mla_prep(proj, pos_col, rope_consts, qnw, kvnw, wq, wkv):
    tm = PREP_TM
    per_b = SEQ // tm
    row = lambda i: (i, 0)
    const = lambda i: (0, 0)
    out_map = lambda i: (i // per_b, 0, i % per_b, 0)
    return pl.pallas_call(
        _mla_prep_kernel,
        grid=(N_TOK // tm,),
        in_specs=[pl.BlockSpec((tm, MLA_Q_RANK), lambda i: (i, P_QLAT // MLA_Q_RANK)),
                  pl.BlockSpec((tm, MLA_KV_RANK), lambda i: (i, P_CKV // MLA_KV_RANK)),
                  pl.BlockSpec((tm, LANES), lambda i: (i, P_KPE // LANES)),
                  pl.BlockSpec((tm, 1), row),
                  pl.BlockSpec((8, LANES), const),
                  pl.BlockSpec((1, MLA_Q_RANK), const),
                  pl.BlockSpec((1, MLA_KV_RANK), const),
                  pl.BlockSpec((MLA_Q_RANK, MLA_HEADS * MLA_HD), const),
                  pl.BlockSpec((MLA_KV_RANK, MLA_HEADS * MLA_HD), const)],
        out_specs=[pl.BlockSpec((1, MLA_HEADS, tm, MLA_HD), out_map),
                   pl.BlockSpec((1, MLA_HEADS, tm, MLA_HD), out_map),
                   pl.BlockSpec((1, MLA_HEADS, tm, MLA_V), out_map)],
        out_shape=[jax.ShapeDtypeStruct((BATCH, MLA_HEADS, SEQ, MLA_HD), BF16),
                   jax.ShapeDtypeStruct((BATCH, MLA_HEADS, SEQ, MLA_HD), BF16),
                   jax.ShapeDtypeStruct((BATCH, MLA_HEADS, SEQ, MLA_V), BF16)],
        compiler_params=_cparams(("arbitrary",)),
        name="mla_prep",
    )(proj, proj, proj, pos_col, rope_consts, qnw, kvnw, wq, wkv)


ATT_T = 512


def _attn_kernel(q_ref, k_ref, v_ref, o_ref, m_ref, l_ref, acc_ref, s0_ref, s1_ref):
    qi = pl.program_id(2)
    q = q_ref[0, 0]
    m_ref[...] = jnp.full(m_ref.shape, -jnp.inf, F32)
    l_ref[...] = jnp.zeros(l_ref.shape, F32)
    acc_ref[...] = jnp.zeros(acc_ref.shape, F32)

    def scores(j):
        r0 = pl.multiple_of(j * ATT_T, ATT_T)
        k = k_ref[0, 0, pl.ds(r0, ATT_T), :]
        return lax.dot_general(k, q, (((1,), (1,)), ((), ())), preferred_element_type=F32)

    def step(j, src_ref, dst_ref):
        masked = dst_ref is None
        r0 = pl.multiple_of(j * ATT_T, ATT_T)
        v = v_ref[0, 0, pl.ds(r0, ATT_T), :]
        s = src_ref[...]
        if not masked:
            dst_ref[...] = scores(j + 1)
        if masked:
            key = lax.broadcasted_iota(jnp.int32, s.shape, 0)
            qry = lax.broadcasted_iota(jnp.int32, s.shape, 1)
            s = jnp.where(key <= qry, s, -jnp.inf)
        m_old = m_ref[...]
        m_new = jnp.maximum(m_old, jnp.max(s, axis=0, keepdims=True))
        p = jnp.exp(s - m_new)
        alpha = jnp.exp(m_old - m_new)
        l_ref[...] = alpha * l_ref[...] + jnp.sum(p, axis=0, keepdims=True)
        pv = lax.dot_general(v, p.astype(BF16), (((0,), (0,)), ((), ())), preferred_element_type=F32)
        acc_ref[...] = alpha * acc_ref[...] + pv
        m_ref[...] = m_new

    def body(jj, carry):
        step(2 * jj, s0_ref, s1_ref)
        step(2 * jj + 1, s1_ref, s0_ref)
        return carry

    s0_ref[...] = scores(0)
    lax.fori_loop(0, qi // 2, body, 0)

    @pl.when(qi % 2 == 1)
    def _():
        step(qi - 1, s0_ref, s1_ref)
        step(qi, s1_ref, None)

    @pl.when(qi % 2 == 0)
    def _():
        step(qi, s0_ref, None)

    o_ref[...] = jnp.transpose(acc_ref[...] / l_ref[...]).astype(BF16)


def _attention(q, k, v):
    t = ATT_T
    nq = SEQ // t
    return pl.pallas_call(
        _attn_kernel,
        grid=(BATCH, MLA_HEADS, nq),
        in_specs=[pl.BlockSpec((1, 1, t, MLA_HD), lambda b, h, i: (b, h, i, 0)),
                  pl.BlockSpec((1, 1, SEQ, MLA_HD), lambda b, h, i: (b, h, 0, 0)),
                  pl.BlockSpec((1, 1, SEQ, MLA_V), lambda b, h, i: (b, h, 0, 0))],
        out_specs=pl.BlockSpec((t, MLA_V), lambda b, h, i: (b * nq + i, h)),
        out_shape=jax.ShapeDtypeStruct((N_TOK, MLA_HEADS * MLA_V), BF16),
        scratch_shapes=[pltpu.VMEM((1, t), F32), pltpu.VMEM((1, t), F32), pltpu.VMEM((MLA_V, t), F32),
                        pltpu.VMEM((t, t), F32), pltpu.VMEM((t, t), F32)],
        compiler_params=_cparams(("arbitrary", "arbitrary", "arbitrary")),
        name="mla_attn",
    )(q, k, v)


GLA_TG = 512
GLA_DIAG = 8
GLA_HPS = 2


def _gla_kernel(q_ref, k_ref, v_ref, glr_ref, gout_ref, wgk_ref, bgk_ref, nw_ref, o_ref, st_ref, g_ref):
    C = GLA_CHUNK

    @pl.when(pl.program_id(2) == 0)
    def _():
        st_ref[...] = jnp.zeros(st_ref.shape, F32)

    z = jnp.dot(glr_ref[...].astype(F32), wgk_ref[...], preferred_element_type=F32,
                precision=lax.Precision.HIGHEST) + bgk_ref[...]
    g_ref[...] = jax.nn.log_sigmoid(z) / GLA_GATE_NORM

    row = lax.broadcasted_iota(jnp.int32, (C, GLA_DK), 0)
    ii = lax.broadcasted_iota(jnp.int32, (C, C), 0)
    jj = lax.broadcasted_iota(jnp.int32, (C, C), 1)
    nt = (((1,), (1,)), ((), ()))

    def chunk(ci, carry):
        for hh in range(GLA_HPS):
            head_chunk(ci, hh)
        return carry

    def head_chunk(ci, hh):
        r0 = pl.multiple_of(ci * C, C)
        kcols = slice(hh * GLA_DK, (hh + 1) * GLA_DK)
        vcols = slice(hh * GLA_DV, (hh + 1) * GLA_DV)
        b = g_ref[pl.ds(r0, C), kcols]
        s = 1
        while s < C:
            b = b + jnp.where(row >= s, pltpu.roll(b, s, 0), 0.0)
            s *= 2
        q = q_ref[pl.ds(r0, C), kcols].astype(F32) * (GLA_DK ** -0.5)
        k = k_ref[pl.ds(r0, C), kcols].astype(F32)
        v = v_ref[pl.ds(r0, C), vcols]
        st = st_ref[hh]

        o = lax.dot_general((q * jnp.exp(b)).astype(BF16), st.astype(BF16), nt,
                            preferred_element_type=F32)

        a = jnp.zeros((C, C), F32)
        lvl = C // 2
        while lvl >= GLA_DIAG:
            pieces = [jnp.broadcast_to(b[t + lvl:t + lvl + 1, :], (2 * lvl, GLA_DK))
                      for t in range(0, C, 2 * lvl)]
            m = pieces[0] if len(pieces) == 1 else jnp.concatenate(pieces, axis=0)
            odd = (row // lvl) % 2 == 1
            ql = jnp.where(odd, q * jnp.exp(jnp.minimum(b - m, 0.0)), 0.0).astype(BF16)
            kl = jnp.where(odd, 0.0, k * jnp.exp(jnp.minimum(m - b, 0.0))).astype(BF16)
            al = lax.dot_general(ql, kl, nt, preferred_element_type=F32)
            a = a + jnp.where(ii // (2 * lvl) == jj // (2 * lvl), al, 0.0)
            lvl //= 2
        for d in range(GLA_DIAG):
            if d == 0:
                t = q * k
            else:
                t = q * pltpu.roll(k, d, 0) * jnp.exp(jnp.minimum(b - pltpu.roll(b, d, 0), 0.0))
            col = jnp.sum(t, axis=-1, keepdims=True)
            a = a + jnp.where((ii - jj == d) & (ii % GLA_DIAG >= d), col, 0.0)
        o = o + jnp.dot(a.astype(BF16), v, preferred_element_type=F32)

        b_last = b[C - 1:C, :]
        kdec = (k * jnp.exp(b_last - b)).astype(BF16)
        st_ref[hh] = st * jnp.exp(b_last) + lax.dot_general(
            v, kdec, (((0,), (0,)), ((), ())), preferred_element_type=F32)

        gate = gout_ref[pl.ds(r0, C), vcols].astype(F32)
        y = _rms(o, nw_ref[...]) * (gate * jax.nn.sigmoid(gate))
        o_ref[pl.ds(r0, C), vcols] = y.astype(BF16)

    lax.fori_loop(0, GLA_TG // C, chunk, 0)


def _gla(proj, wgk, bgk, nw):
    tg = GLA_TG
    per_b = SEQ // tg
    kw = GLA_HPS * GLA_DK
    vw = GLA_HPS * GLA_DV
    rb = lambda b, h, t: b * per_b + t
    return pl.pallas_call(
        _gla_kernel,
        grid=(BATCH, GLA_HEADS // GLA_HPS, per_b),
        in_specs=[pl.BlockSpec((tg, kw), lambda b, h, t: (rb(b, h, t), P_GQ // kw + h)),
                  pl.BlockSpec((tg, kw), lambda b, h, t: (rb(b, h, t), P_GK // kw + h)),
                  pl.BlockSpec((tg, vw), lambda b, h, t: (rb(b, h, t), P_GV // vw + h)),
                  pl.BlockSpec((tg, LANES), lambda b, h, t: (rb(b, h, t), P_GLR // LANES)),
                  pl.BlockSpec((tg, vw), lambda b, h, t: (rb(b, h, t), P_GOUT // vw + h)),
                  pl.BlockSpec((LANES, kw), lambda b, h, t: (0, h)),
                  pl.BlockSpec((1, kw), lambda b, h, t: (0, h)),
                  pl.BlockSpec((1, GLA_DV), lambda b, h, t: (0, 0))],
        out_specs=pl.BlockSpec((tg, vw), lambda b, h, t: (rb(b, h, t), h)),
        out_shape=jax.ShapeDtypeStruct((N_TOK, GLA_HEADS * GLA_DV), BF16),
        scratch_shapes=[pltpu.VMEM((GLA_HPS, GLA_DV, GLA_DK), F32), pltpu.VMEM((tg, kw), F32)],
        compiler_params=_cparams(("arbitrary", "arbitrary", "arbitrary")),
        name="gla",
    )(proj, proj, proj, proj, proj, wgk, bgk, nw)


MRG_TM = 256


def _merge_kernel(x_ref, oa_ref, ob_ref, ga_ref, gb_ref, gm_ref, shf_ref, scf_ref, nfw_ref,
                  woa_ref, wob_ref, wout_ref, wrh_ref, wrl_ref, br_ref,
                  x1_ref, h2_ref, idx_ref, tw_ref):
    ya = jnp.dot(oa_ref[...], woa_ref[...], preferred_element_type=F32)
    yb = jnp.dot(ob_ref[...], wob_ref[...], preferred_element_type=F32)
    merged = (jax.nn.sigmoid(ga_ref[...].astype(F32)) * ya
              + jax.nn.sigmoid(gb_ref[...].astype(F32)) * yb).astype(BF16)
    x1 = x_ref[...] + gm_ref[0] * jnp.dot(merged, wout_ref[...], preferred_element_type=F32)
    x1_ref[...] = x1
    h2 = _rms(x1, nfw_ref[...]) * (1.0 + scf_ref[0]) + shf_ref[0]
    for j in range(ROW_TILES):
        h2_ref[pl.ds(j, MRG_TM, stride=ROW_TILES), :] = h2[:, j * LANES:(j + 1) * LANES]

    hi = h2.astype(BF16)
    lo = (h2 - hi.astype(F32)).astype(BF16)
    logits = (jnp.dot(hi, wrh_ref[...], preferred_element_type=F32)
              + jnp.dot(hi, wrl_ref[...], preferred_element_type=F32)
              + jnp.dot(lo, wrh_ref[...], preferred_element_type=F32)) + br_ref[...]
    lane = lax.broadcasted_iota(jnp.int32, logits.shape, 1)
    vals = jnp.where(lane < N_EXPERTS, logits, -jnp.inf)
    idx_out = jnp.zeros(logits.shape, jnp.int32)
    w_out = jnp.zeros(logits.shape, F32)
    top = None
    denom = None
    for kk in range(TOP_K):
        m = jnp.max(vals, axis=-1, keepdims=True)
        sel = jnp.min(jnp.where(vals == m, lane, LANES), axis=-1, keepdims=True)
        if kk == 0:
            top = m
        e = jnp.exp(m - top)
        denom = e if kk == 0 else denom + e
        idx_out = jnp.where(lane == kk, sel, idx_out)
        w_out = jnp.where(lane == kk, e, w_out)
        vals = jnp.where(lane == sel, -jnp.inf, vals)
    idx_ref[...] = idx_out
    tw_ref[...] = w_out / denom


def _merge(x2d, o_mla, o_gla, proj, gm, shf, scf, nfw, woa, wob, wout, wrh, wrl, br):
    tm = MRG_TM
    per_b = SEQ // tm
    row = lambda i: (i, 0)
    const = lambda i: (0, 0)
    bvec = lambda i: (i // per_b, 0, 0)
    single = dict(pipeline_mode=pl.Buffered(1))
    return pl.pallas_call(
        _merge_kernel,
        grid=(N_TOK // tm,),
        in_specs=[pl.BlockSpec((tm, D_MODEL), row),
                  pl.BlockSpec((tm, MLA_HEADS * MLA_V), row),
                  pl.BlockSpec((tm, GLA_HEADS * GLA_DV), row),
                  pl.BlockSpec((tm, D_MODEL), lambda i: (i, P_GATEA // D_MODEL)),
                  pl.BlockSpec((tm, D_MODEL), lambda i: (i, P_GATEB // D_MODEL)),
                  pl.BlockSpec((1, 1, D_MODEL), bvec),
                  pl.BlockSpec((1, 1, D_MODEL), bvec),
                  pl.BlockSpec((1, 1, D_MODEL), bvec),
                  pl.BlockSpec((1, D_MODEL), const),
                  pl.BlockSpec((MLA_HEADS * MLA_V, D_MODEL), const, **single),
                  pl.BlockSpec((GLA_HEADS * GLA_DV, D_MODEL), const, **single),
                  pl.BlockSpec((D_MODEL, D_MODEL), const, **single),
                  pl.BlockSpec((D_MODEL, LANES), const, **single),
                  pl.BlockSpec((D_MODEL, LANES), const, **single),
                  pl.BlockSpec((1, LANES), const)],
        out_specs=[pl.BlockSpec((tm, D_MODEL), row),
                   pl.BlockSpec((tm * ROW_TILES, LANES), row),
                   pl.BlockSpec((tm, LANES), row),
                   pl.BlockSpec((tm, LANES), row)],
        out_shape=[jax.ShapeDtypeStruct((N_TOK, D_MODEL), F32),
                   jax.ShapeDtypeStruct((N_TOK * ROW_TILES, LANES), F32),
                   jax.ShapeDtypeStruct((N_TOK, LANES), jnp.int32),
                   jax.ShapeDtypeStruct((N_TOK, LANES), F32)],
        compiler_params=_cparams(("arbitrary",)),
        name="merge_router",
    )(x2d, o_mla, o_gla, proj, proj, gm, shf, scf, nfw, woa, wob, wout, wrh, wrl, br)


ROW_COPY_PRIORITY = 1
MOE_CHAINS = ((0, 2), (2, 2), (4, 1))
MOE_STEP_ROWS = MOE_ITEM_ROWS // MOE_NCHUNK
MOE_QUOTAS = tuple(MOE_STEP_ROWS * w // MOE_ITEM_BLKS for _, w in MOE_CHAINS)
assert sum(w for _, w in MOE_CHAINS) == MOE_ITEM_BLKS and sum(MOE_QUOTAS) == MOE_STEP_ROWS


def _moe_kernel(it_e_ref, it_start_ref, it_nblk_ref, src_ref, dst_ref,
                h2_hbm, w1_ref, b1_ref, w2_ref, b2_ref, sel_ref, y4_hbm,
                stage_ref, xb_ref, acc_ref, yout_ref, w1b_ref, w2b_ref, gsem, ssem):
    it = pl.program_id(0)
    c = pl.program_id(1)
    nblk = it_nblk_ref[it]
    prev = jnp.maximum(it - 1, 0)
    nblk_prev = jnp.where(it > 0, it_nblk_ref[prev], 0)
    live = nblk > 0
    prev_live = nblk_prev > 0
    rows_prev = nblk_prev * MOE_BLK
    rows_next = it_nblk_ref[it + 1] * MOE_BLK
    base_prev = it_start_ref[prev]
    base_next = it_start_ref[it + 1]

    def token_rows(t):
        return pl.ds(pl.multiple_of(t * ROW_TILES, ROW_TILES), ROW_TILES)

    def gather_row(r, base, nrows):
        tok = jnp.where(r < nrows, src_ref[jnp.minimum(base + r, MOE_SLOTS - 1)], 0)
        pltpu.make_async_copy(h2_hbm.at[token_rows(tok), :], stage_ref.at[token_rows(r), :],
                              gsem).start(priority=ROW_COPY_PRIORITY)

    def scatter_row(r):
        d = dst_ref[jnp.minimum(base_prev + r, MOE_SLOTS - 1)]
        d = jnp.where(jnp.logical_and(r < rows_prev, d >= 0), d, N_ASSIGN + r)
        pltpu.make_async_copy(yout_ref.at[token_rows(r), :], y4_hbm.at[token_rows(d), :],
                              ssem).start(priority=ROW_COPY_PRIORITY)

    def issue_quota(ci, gather=True):
        r0 = c * MOE_STEP_ROWS + sum(MOE_QUOTAS[:ci])
        for u in range(MOE_QUOTAS[ci]):
            if gather:
                gather_row(r0 + u, base_next, rows_next)
            scatter_row(r0 + u)

    @pl.when(jnp.logical_and(it == 0, c == 0))
    def _():
        yout_ref[...] = jnp.zeros(yout_ref.shape, F32)

        def first(r, carry):
            gather_row(r, it_start_ref[0], nblk * MOE_BLK)
            return carry
        lax.fori_loop(0, MOE_ITEM_ROWS, first, 0)

    @pl.when(jnp.logical_and(c == 0, jnp.logical_or(it == 0, prev_live)))
    def _():
        pltpu.make_async_copy(h2_hbm.at[pl.ds(0, MOE_ITEM_ROWS * ROW_TILES), :], stage_ref, gsem).wait()

    @pl.when(jnp.logical_and(c == 0, live))
    def _():
        for s in range(MOE_ITEM_BLKS):
            @pl.when(s < nblk)
            def _():
                rows = slice(s * MOE_BLK, (s + 1) * MOE_BLK)
                for j in range(ROW_TILES):
                    piece = stage_ref[pl.ds(s * MOE_BLK * ROW_TILES + j, MOE_BLK, stride=ROW_TILES), :]
                    xb_ref[rows, j * LANES:(j + 1) * LANES] = piece.astype(BF16)
                acc_ref[rows, :] = jnp.broadcast_to(b2_ref[0], (MOE_BLK, D_MODEL))

    def compute(row0, nrows):
        rows = slice(row0, row0 + nrows)
        hm = jnp.dot(xb_ref[rows, :], w1b_ref[...], preferred_element_type=F32) + b1_ref[0]
        lin = pltpu.roll(hm, 2 * MOE_TH - 1, 1)
        glu = jnp.minimum(hm, SWIGLU_LIMIT)
        lin = jnp.clip(lin, -SWIGLU_LIMIT, SWIGLU_LIMIT)
        act = glu * jax.nn.sigmoid(SWIGLU_ALPHA * glu) * (lin + 1.0)
        lane = lax.broadcasted_iota(jnp.int32, act.shape, 1)
        act = jnp.where(lane % 2 == 0, act, 0.0).astype(BF16)
        act = jnp.dot(act, sel_ref[...], preferred_element_type=F32).astype(BF16)
        acc_ref[rows, :] += jnp.dot(act, w2b_ref[...], preferred_element_type=F32)

    @pl.when(live)
    def _():
        w1b_ref[...] = w1_ref[0].astype(BF16)
        w2b_ref[...] = w2_ref[0].astype(BF16)

    for ci, (first, width) in enumerate(MOE_CHAINS):
        in_chain = jnp.clip(nblk - first, 0, width)
        for nb in range(width, 0, -1):
            @pl.when(in_chain == nb)
            def _():
                compute(first * MOE_BLK, nb * MOE_BLK)
                issue_quota(ci)

        @pl.when(jnp.logical_and(in_chain == 0, live))
        def _():
            issue_quota(ci)

        @pl.when(jnp.logical_and(jnp.logical_not(live), prev_live))
        def _():
            issue_quota(ci, gather=False)

    @pl.when(jnp.logical_and(c == MOE_NCHUNK - 1, jnp.logical_or(live, prev_live)))
    def _():
        pltpu.make_async_copy(yout_ref, y4_hbm.at[pl.ds(0, MOE_ITEM_ROWS * ROW_TILES), :], ssem).wait()

    @pl.when(jnp.logical_and(c == MOE_NCHUNK - 1, live))
    def _():
        for s in range(MOE_ITEM_BLKS):
            @pl.when(s < nblk)
            def _():
                rows = slice(s * MOE_BLK, (s + 1) * MOE_BLK)
                for j in range(ROW_TILES):
                    yout_ref[pl.ds(s * MOE_BLK * ROW_TILES + j, MOE_BLK, stride=ROW_TILES), :] = (
                        acc_ref[rows, j * LANES:(j + 1) * LANES])


def _moe(it_e, it_start, it_nblk, slot_src, slot_dst, h2, w1, b1, w2, b2):
    def chunk_of(it, c, n_ref):
        return jnp.where(n_ref[it] > 0, c, MOE_NCHUNK - 1)

    def w1_map(it, c, e_ref, s_ref, n_ref, *_):
        return (e_ref[it], 0, chunk_of(it, c, n_ref))

    def w2_map(it, c, e_ref, s_ref, n_ref, *_):
        return (e_ref[it], chunk_of(it, c, n_ref), 0)

    def b2_map(it, c, e_ref, *_):
        return (e_ref[it], 0, 0)

    sel = (jnp.arange(2 * MOE_TH, dtype=jnp.int32)[:, None]
           == 2 * jnp.arange(MOE_TH, dtype=jnp.int32)[None, :]).astype(BF16)

    grid_spec = pltpu.PrefetchScalarGridSpec(
        num_scalar_prefetch=5,
        grid=(MOE_NITEMS + 1, MOE_NCHUNK),
        in_specs=[pl.BlockSpec(memory_space=pl.ANY),
                  pl.BlockSpec((1, D_MODEL, 2 * MOE_TH), w1_map),
                  pl.BlockSpec((1, 1, 2 * MOE_TH), w1_map),
                  pl.BlockSpec((1, MOE_TH, D_MODEL), w2_map),
                  pl.BlockSpec((1, 1, D_MODEL), b2_map),
                  pl.BlockSpec((2 * MOE_TH, MOE_TH), lambda it, c, *_: (0, 0))],
        out_specs=pl.BlockSpec(memory_space=pl.ANY),
        scratch_shapes=[pltpu.VMEM((MOE_ITEM_ROWS * ROW_TILES, LANES), F32),
                        pltpu.VMEM((MOE_ITEM_ROWS, D_MODEL), BF16),
                        pltpu.VMEM((MOE_ITEM_ROWS, D_MODEL), F32),
                        pltpu.VMEM((MOE_ITEM_ROWS * ROW_TILES, LANES), F32),
                        pltpu.VMEM((D_MODEL, 2 * MOE_TH), BF16),
                        pltpu.VMEM((MOE_TH, D_MODEL), BF16),
                        pltpu.SemaphoreType.DMA(()),
                        pltpu.SemaphoreType.DMA(())],
    )
    return pl.pallas_call(
        _moe_kernel,
        grid_spec=grid_spec,
        out_shape=jax.ShapeDtypeStruct(((N_ASSIGN + MOE_ITEM_ROWS) * ROW_TILES, LANES), F32),
        compiler_params=_cparams(("arbitrary", "arbitrary")),
        name="moe_experts",
    )(it_e, it_start, it_nblk, slot_src, slot_dst, h2, w1,
      b1.reshape(N_EXPERTS, 1, 2 * D_EXPERT), w2, b2.reshape(N_EXPERTS, 1, D_MODEL), sel)


def _route_tables(top_idx):
    e_flat = top_idx.reshape(N_ASSIGN)
    onehot = (e_flat[:, None] == jnp.arange(N_EXPERTS, dtype=jnp.int32)[None, :]).astype(jnp.int32)
    csum = jnp.cumsum(onehot, axis=0)
    rank = jnp.sum(csum * onehot, axis=-1) - 1
    counts = csum[-1]
    nblk = (counts + MOE_BLK - 1) // MOE_BLK
    blk_end = jnp.cumsum(nblk)
    blk_start = blk_end - nblk
    dest = blk_start[e_flat] * MOE_BLK + rank
    slot_a = jnp.full((MOE_SLOTS,), -1, jnp.int32).at[dest].set(jnp.arange(N_ASSIGN, dtype=jnp.int32))
    slot_src = jnp.where(slot_a >= 0, slot_a // TOP_K, 0)
    slot_dst = jnp.where(slot_a >= 0, (slot_a % TOP_K) * N_TOK + slot_a // TOP_K, -1)

    nit = (nblk + MOE_ITEM_BLKS - 1) // MOE_ITEM_BLKS
    it_end = jnp.cumsum(nit)
    it_begin = it_end - nit
    n_used = it_end[-1]
    ids = jnp.arange(MOE_NITEMS + 2, dtype=jnp.int32)
    valid = ids < n_used
    last = jnp.minimum(ids, n_used - 1)
    e_of = jnp.minimum(jnp.sum((it_end[None, :] <= last[:, None]).astype(jnp.int32), axis=1), N_EXPERTS - 1)
    local = ids - it_begin[e_of]
    it_start = jnp.where(valid, (blk_start[e_of] + local * MOE_ITEM_BLKS) * MOE_BLK, 0).astype(jnp.int32)
    it_nblk = jnp.where(valid, jnp.clip(nblk[e_of] - local * MOE_ITEM_BLKS, 0, MOE_ITEM_BLKS), 0)
    return e_of, it_start, it_nblk.astype(jnp.int32), slot_src, slot_dst


FIN_TM = 256


def _final_kernel(x1_ref, ya_ref, yb_ref, yc_ref, yd_ref, tw_ref, gf_ref, sho_ref, sco_ref, nw_ref, o_ref):
    tw = tw_ref[...]
    pieces = []
    for j in range(ROW_TILES):
        acc = None
        for kk, y_ref in enumerate((ya_ref, yb_ref, yc_ref, yd_ref)):
            term = tw[:, kk:kk + 1] * y_ref[pl.ds(j, FIN_TM, stride=ROW_TILES), :]
            acc = term if acc is None else acc + term
        pieces.append(acc)
    moe = jnp.concatenate(pieces, axis=1)
    x2 = x1_ref[...] + gf_ref[0] * moe
    o_ref[...] = _rms(x2, nw_ref[...]) * (1.0 + sco_ref[0]) + sho_ref[0]


def _final(x1, y4, tw, gf, sho, sco, nw):
    tm = FIN_TM
    per_b = SEQ // tm
    row = lambda i: (i, 0)
    bvec = lambda i: (i // per_b, 0, 0)
    nb = N_TOK // tm
    kth = lambda kk: pl.BlockSpec((tm * ROW_TILES, LANES), lambda i: (kk * nb + i, 0))
    return pl.pallas_call(
        _final_kernel,
        grid=(nb,),
        in_specs=[pl.BlockSpec((tm, D_MODEL), row),
                  kth(0), kth(1), kth(2), kth(3),
                  pl.BlockSpec((tm, LANES), row),
                  pl.BlockSpec((1, 1, D_MODEL), bvec),
                  pl.BlockSpec((1, 1, D_MODEL), bvec),
                  pl.BlockSpec((1, 1, D_MODEL), bvec),
                  pl.BlockSpec((1, D_MODEL), lambda i: (0, 0))],
        out_specs=pl.BlockSpec((tm, D_MODEL), row),
        out_shape=jax.ShapeDtypeStruct((N_TOK, D_MODEL), F32),
        compiler_params=_cparams(("arbitrary",)),
        name="combine_final",
    )(x1, y4, y4, y4, y4, tw, gf, sho, sco, nw)


def _deinterleave(n):
    return np.concatenate([np.arange(0, n, 2), np.arange(1, n, 2)])


_IN_OFFS = np.cumsum([0, MLA_Q_RANK, MLA_KV_RANK + MLA_ROPE, GLA_HEADS * GLA_DK, GLA_HEADS * GLA_DK,
                      GLA_HEADS * GLA_DV, GLA_GATE_RANK, GLA_HEADS * GLA_DV, D_MODEL, D_MODEL])
_IN_MOVES = ((_IN_OFFS[0], MLA_Q_RANK, P_QLAT), (_IN_OFFS[1], MLA_KV_RANK, P_CKV),
             (_IN_OFFS[5], GLA_GATE_RANK, P_GLR), (_IN_OFFS[2], GLA_HEADS * GLA_DK, P_GQ),
             (_IN_OFFS[3], GLA_HEADS * GLA_DK, P_GK), (_IN_OFFS[4], GLA_HEADS * GLA_DV, P_GV),
             (_IN_OFFS[6], GLA_HEADS * GLA_DV, P_GOUT), (_IN_OFFS[7], D_MODEL, P_GATEA),
             (_IN_OFFS[8], D_MODEL, P_GATEB))
PACK_TM = 256


def _pack_kernel(w_ref, kpe_ref, o_ref):
    o_ref[:, P_KPE:P_KPE + LANES] = kpe_ref[...]
    o_ref[:, P_GLR:P_GLR + LANES] = jnp.zeros((PACK_TM, LANES), BF16)
    for src, width, dst in _IN_MOVES:
        o_ref[:, dst:dst + width] = w_ref[:, int(src):int(src) + width].astype(BF16)


def _pack_w_in(w_in):
    d_in = w_in.shape[1]
    kpe = w_in[:, _IN_OFFS[1] + MLA_KV_RANK:_IN_OFFS[2]][:, _deinterleave(MLA_ROPE)]
    kpe = jnp.concatenate([kpe, jnp.zeros((D_MODEL, LANES - MLA_ROPE), w_in.dtype)], axis=1).astype(BF16)
    return pl.pallas_call(
        _pack_kernel,
        grid=(D_MODEL // PACK_TM,),
        in_specs=[pl.BlockSpec((PACK_TM, d_in), lambda i: (i, 0)),
                  pl.BlockSpec((PACK_TM, LANES), lambda i: (i, 0))],
        out_specs=pl.BlockSpec((PACK_TM, P_TOTAL), lambda i: (i, 0)),
        out_shape=jax.ShapeDtypeStruct((D_MODEL, P_TOTAL), BF16),
        compiler_params=_cparams(("arbitrary",)),
        name="pack_w_in",
    )(w_in, kpe)


def _pack_w_q_b(w):
    w = w.reshape(MLA_Q_RANK, MLA_HEADS, MLA_NOPE + MLA_ROPE)
    nope = w[:, :, :MLA_NOPE]
    pe = w[:, :, MLA_NOPE:][:, :, _deinterleave(MLA_ROPE)]
    pad = jnp.zeros((MLA_Q_RANK, MLA_HEADS, MLA_HD - MLA_NOPE - MLA_ROPE), w.dtype)
    return jnp.concatenate([nope, pe, pad], axis=-1).reshape(MLA_Q_RANK, MLA_HEADS * MLA_HD).astype(BF16)


def _rope_consts():
    inv_freq = ROPE_THETA ** (-(jnp.arange(0, MLA_ROPE, 2, dtype=F32) / MLA_ROPE))
    half = MLA_ROPE // 2
    z = jnp.zeros((LANES - MLA_ROPE,), F32)
    rows = [jnp.concatenate([inv_freq, inv_freq, z]),
            jnp.concatenate([-jnp.ones((half,), F32), jnp.zeros((half,), F32), z]),
            jnp.concatenate([jnp.zeros((half,), F32), jnp.ones((half,), F32), z])]
    return jnp.concatenate([jnp.stack(rows), jnp.zeros((5, LANES), F32)], axis=0)


def kernel(x, c, positions, w_ada, b_ada, norm_mix_w, w_in, mla_q_norm_w, mla_w_q_b, mla_kv_norm_w,
           mla_w_kv_b, gla_w_gk_up, gla_b_gk_up, gla_norm_w, w_o_mla, w_o_gla, w_out, norm_ffn_w,
           w_router, b_router, w1, b1, w2, b2, w_ada_final, b_ada_final, norm_final_w):
    assert w_ada.shape[0] == 1, "single-layer stack"
    x2d = x.reshape(N_TOK, D_MODEL)
    c_t = c.T

    mod = _adaln(c_t, w_ada[0], b_ada[0])
    fmod = _adaln(c_t, w_ada_final, b_ada_final)
    sh_m, sc_m, g_m, sh_f, sc_f, g_f = [m.reshape(BATCH, 1, D_MODEL) for m in jnp.split(mod, N_MOD, axis=-1)]
    sh_o, sc_o = [m.reshape(BATCH, 1, D_MODEL) for m in jnp.split(fmod, 2, axis=-1)]

    proj = _inproj(x2d, norm_mix_w[0].reshape(1, D_MODEL), sh_m, sc_m, _pack_w_in(w_in[0]))

    q, k, v = _mla_prep(proj, positions.reshape(N_TOK, 1), _rope_consts(),
                        mla_q_norm_w[0].reshape(1, MLA_Q_RANK), mla_kv_norm_w[0].reshape(1, MLA_KV_RANK),
                        _pack_w_q_b(mla_w_q_b[0]), mla_w_kv_b[0].astype(BF16))
    o_mla = _attention(q, k, v)

    wgk = jnp.concatenate([gla_w_gk_up[0], jnp.zeros((LANES - GLA_GATE_RANK, GLA_HEADS * GLA_DK), F32)], axis=0)
    o_gla = _gla(proj, wgk, gla_b_gk_up[0].reshape(1, GLA_HEADS * GLA_DK), gla_norm_w[0].reshape(1, GLA_DV))

    wr = jnp.concatenate([w_router[0], jnp.zeros((D_MODEL, LANES - N_EXPERTS), F32)], axis=1)
    wr_hi = wr.astype(BF16)
    wr_lo = (wr - wr_hi.astype(F32)).astype(BF16)
    br = jnp.concatenate([b_router[0], jnp.zeros((LANES - N_EXPERTS,), F32)]).reshape(1, LANES)
    x1, h2, top_idx, top_w = _merge(
        x2d, o_mla, o_gla, proj, g_m, sh_f, sc_f, norm_ffn_w[0].reshape(1, D_MODEL),
        w_o_mla[0].astype(BF16), w_o_gla[0].astype(BF16), w_out[0].astype(BF16), wr_hi, wr_lo, br)

    it_e, it_start, it_nblk, slot_src, slot_dst = _route_tables(top_idx[:, :TOP_K])
    y4 = _moe(it_e, it_start, it_nblk, slot_src, slot_dst, h2, w1[0], b1[0], w2[0], b2[0])

    out = _final(x1, y4, top_w, g_f, sh_o, sc_o, norm_final_w.reshape(1, D_MODEL))
    return out.reshape(BATCH, SEQ, D_MODEL)
```

```python
import functools

import jax
import jax.numpy as jnp
import numpy as np
from jax import lax
from jax.experimental import pallas as pl
from jax.experimental.pallas import tpu as pltpu

F32 = jnp.float32
BF16 = jnp.bfloat16

D_MODEL = 2048
BATCH = 2
SEQ = 4096
N_TOK = BATCH * SEQ

MLA_HEADS = 8
MLA_Q_RANK = 512
MLA_KV_RANK = 256
MLA_NOPE = 128
MLA_ROPE = 64
MLA_V = 128
ROPE_THETA = 10000.0
MLA_HD = 256

GLA_HEADS = 4
GLA_DK = 128
GLA_DV = 256
GLA_GATE_RANK = 16
GLA_GATE_NORM = 16.0
GLA_CHUNK = 64

N_EXPERTS = 32
TOP_K = 4
D_EXPERT = D_MODEL
SWIGLU_LIMIT = 7.0
SWIGLU_ALPHA = 1.702
EPS = 1e-6
N_MOD = 6

LANES = 128
ROW_TILES = D_MODEL // LANES
VMEM_LIMIT = 56 * 1024 * 1024

P_QLAT = 0
P_CKV = 512
P_KPE = 768
P_GLR = 896
P_GQ = 1024
P_GK = 1536
P_GV = 2048
P_GOUT = 3072
P_GATEA = 4096
P_GATEB = 6144
P_TOTAL = 8192

MOE_BLK = 256
MOE_ITEM_BLKS = 5
MOE_ITEM_ROWS = MOE_BLK * MOE_ITEM_BLKS
MOE_TH = 256
MOE_NCHUNK = D_EXPERT // MOE_TH
N_ASSIGN = N_TOK * TOP_K
MOE_NBLK_MAX = N_ASSIGN // MOE_BLK + N_EXPERTS
MOE_SLOTS = MOE_NBLK_MAX * MOE_BLK
MOE_NITEMS = N_ASSIGN // MOE_ITEM_ROWS + N_EXPERTS


def _cparams(sem, vmem=VMEM_LIMIT):
    return pltpu.CompilerParams(dimension_semantics=sem, vmem_limit_bytes=vmem)


def _rms(x, w):
    return x * lax.rsqrt(jnp.mean(x * x, axis=-1, keepdims=True) + EPS) * w


def _adaln_kernel(ct_ref, w_ref, b_ref, o_ref):
    ct = ct_ref[...]
    cond = ct * jax.nn.sigmoid(ct)
    w = w_ref[...]
    for b in range(BATCH):
        o_ref[b:b + 1, :] = jnp.sum(w * cond[:, b:b + 1], axis=0, keepdims=True) + b_ref[...]


def _adaln(c_t, w, b, tn=1024):
    d, n = w.shape
    return pl.pallas_call(
        _adaln_kernel,
        grid=(n // tn,),
        in_specs=[pl.BlockSpec((d, BATCH), lambda j: (0, 0)),
                  pl.BlockSpec((d, tn), lambda j: (0, j)),
                  pl.BlockSpec((1, tn), lambda j: (0, j))],
        out_specs=pl.BlockSpec((BATCH, tn), lambda j: (0, j)),
        out_shape=jax.ShapeDtypeStruct((BATCH, n), F32),
        compiler_params=_cparams(("arbitrary",)),
        name="adaln",
    )(c_t, w, b.reshape(1, n))


IN_TM = 1024
IN_TN = 2048


def _inproj_kernel(x_ref, nw_ref, sh_ref, sc_ref, w_ref, o_ref, h_ref):
    @pl.when(pl.program_id(1) == 0)
    def _():
        for r in range(0, IN_TM, 256):
            x = x_ref[r:r + 256, :]
            h = _rms(x, nw_ref[...]) * (1.0 + sc_ref[0]) + sh_ref[0]
            h_ref[r:r + 256, :] = h.astype(BF16)

    o_ref[...] = jnp.dot(h_ref[...], w_ref[...], preferred_element_type=F32).astype(BF16)


def _inproj(x2d, nw, sh, sc, w_packed):
    per_b = SEQ // IN_TM
    return pl.pallas_call(
        _inproj_kernel,
        grid=(N_TOK // IN_TM, P_TOTAL // IN_TN),
        in_specs=[pl.BlockSpec((IN_TM, D_MODEL), lambda i, j: (i, 0)),
                  pl.BlockSpec((1, D_MODEL), lambda i, j: (0, 0)),
                  pl.BlockSpec((1, 1, D_MODEL), lambda i, j: (i // per_b, 0, 0)),
                  pl.BlockSpec((1, 1, D_MODEL), lambda i, j: (i // per_b, 0, 0)),
                  pl.BlockSpec((D_MODEL, IN_TN), lambda i, j: (0, j))],
        out_specs=pl.BlockSpec((IN_TM, IN_TN), lambda i, j: (i, j)),
        out_shape=jax.ShapeDtypeStruct((N_TOK, P_TOTAL), BF16),
        scratch_shapes=[pltpu.VMEM((IN_TM, D_MODEL), BF16)],
        compiler_params=_cparams(("arbitrary", "arbitrary")),
        name="inproj",
    )(x2d, nw, sh, sc, w_packed)


PREP_TM = 512


def _mla_prep_kernel(ql_ref, ckv_ref, kpe_ref, pos_ref, rc_ref, qnw_ref, kvnw_ref, wq_ref, wkv_ref,
                     q_ref, k_ref, v_ref):
    scale = (MLA_NOPE + MLA_ROPE) ** -0.5
    qn = _rms(ql_ref[...].astype(F32), qnw_ref[...]).astype(BF16)
    q = jnp.dot(qn, wq_ref[...], preferred_element_type=F32)
    cn = _rms(ckv_ref[...].astype(F32), kvnw_ref[...]).astype(BF16)
    kv = jnp.dot(cn, wkv_ref[...], preferred_element_type=F32)

    ang = pos_ref[...].astype(F32) * rc_ref[0:1, :]
    cos_t = jnp.cos(ang)
    sin_t = jnp.sin(ang)
    sin_a = sin_t * rc_ref[1:2, :]
    sin_b = sin_t * rc_ref[2:3, :]

    def rope(t):
        return (t * cos_t + pltpu.roll(t, LANES - MLA_ROPE // 2, 1) * sin_a
                + pltpu.roll(t, MLA_ROPE // 2, 1) * sin_b)

    kpe = rope(kpe_ref[...].astype(F32)).astype(BF16)
    for h in range(MLA_HEADS):
        c0 = h * MLA_HD
        q_ref[0, h, :, 0:LANES] = (q[:, c0:c0 + LANES] * scale).astype(BF16)
        q_ref[0, h, :, LANES:MLA_HD] = (rope(q[:, c0 + LANES:c0 + MLA_HD]) * scale).astype(BF16)
        k_ref[0, h, :, 0:LANES] = kv[:, c0:c0 + LANES].astype(BF16)
        k_ref[0, h, :, LANES:MLA_HD] = kpe
        v_ref[0, h, :, :] = kv[:, c0 + LANES:c0 + MLA_HD].astype(BF16)


def _mla_prep(proj, pos_col, rope_consts, qnw, kvnw, wq, wkv):
    tm = PREP_TM
    per_b = SEQ // tm
    row = lambda i: (i, 0)
    const = lambda i: (0, 0)
    out_map = lambda i: (i // per_b, 0, i % per_b, 0)
    return pl.pallas_call(
        _mla_prep_kernel,
        grid=(N_TOK // tm,),
        in_specs=[pl.BlockSpec((tm, MLA_Q_RANK), lambda i: (i, P_QLAT // MLA_Q_RANK)),
                  pl.BlockSpec((tm, MLA_KV_RANK), lambda i: (i, P_CKV // MLA_KV_RANK)),
                  pl.BlockSpec((tm, LANES), lambda i: (i, P_KPE // LANES)),
                  pl.BlockSpec((tm, 1), row),
                  pl.BlockSpec((8, LANES), const),
                  pl.BlockSpec((1, MLA_Q_RANK), const),
                  pl.BlockSpec((1, MLA_KV_RANK), const),
                  pl.BlockSpec((MLA_Q_RANK, MLA_HEADS * MLA_HD), const),
                  pl.BlockSpec((MLA_KV_RANK, MLA_HEADS * MLA_HD), const)],
        out_specs=[pl.BlockSpec((1, MLA_HEADS, tm, MLA_HD), out_map),
                   pl.BlockSpec((1, MLA_HEADS, tm, MLA_HD), out_map),
                   pl.BlockSpec((1, MLA_HEADS, tm, MLA_V), out_map)],
        out_shape=[jax.ShapeDtypeStruct((BATCH, MLA_HEADS, SEQ, MLA_HD), BF16),
                   jax.ShapeDtypeStruct((BATCH, MLA_HEADS, SEQ, MLA_HD), BF16),
                   jax.ShapeDtypeStruct((BATCH, MLA_HEADS, SEQ, MLA_V), BF16)],
        compiler_params=_cparams(("arbitrary",)),
        name="mla_prep",
    )(proj, proj, proj, pos_col, rope_consts, qnw, kvnw, wq, wkv)


ATT_T = 1024


def _attn_kernel(q_ref, k_ref, v_ref, o_ref, m_ref, l_ref, acc_ref, s0_ref, s1_ref):
    qi = pl.program_id(2)
    q = q_ref[0, 0]
    m_ref[...] = jnp.full(m_ref.shape, -jnp.inf, F32)
    l_ref[...] = jnp.zeros(l_ref.shape, F32)
    acc_ref[...] = jnp.zeros(acc_ref.shape, F32)

    def scores(j):
        r0 = pl.multiple_of(j * ATT_T, ATT_T)
        k = k_ref[0, 0, pl.ds(r0, ATT_T), :]
        return lax.dot_general(k, q, (((1,), (1,)), ((), ())), preferred_element_type=F32)

    def step(j, src_ref, dst_ref):
        masked = dst_ref is None
        r0 = pl.multiple_of(j * ATT_T, ATT_T)
        v = v_ref[0, 0, pl.ds(r0, ATT_T), :]
        s = src_ref[...]
        if not masked:
            dst_ref[...] = scores(j + 1)
        if masked:
            key = lax.broadcasted_iota(jnp.int32, s.shape, 0)
            qry = lax.broadcasted_iota(jnp.int32, s.shape, 1)
            s = jnp.where(key <= qry, s, -jnp.inf)
        m_old = m_ref[...]
        m_new = jnp.maximum(m_old, jnp.max(s, axis=0, keepdims=True))
        p = jnp.exp(s - m_new)
        alpha = jnp.exp(m_old - m_new)
        l_ref[...] = alpha * l_ref[...] + jnp.sum(p, axis=0, keepdims=True)
        pv = lax.dot_general(v, p.astype(BF16), (((0,), (0,)), ((), ())), preferred_element_type=F32)
        acc_ref[...] = alpha * acc_ref[...] + pv
        m_ref[...] = m_new

    def body(jj, carry):
        step(2 * jj, s0_ref, s1_ref)
        step(2 * jj + 1, s1_ref, s0_ref)
        return carry

    s0_ref[...] = scores(0)
    lax.fori_loop(0, qi // 2, body, 0)

    @pl.when(qi % 2 == 1)
    def _():
        step(qi - 1, s0_ref, s1_ref)
        step(qi, s1_ref, None)

    @pl.when(qi % 2 == 0)
    def _():
        step(qi, s0_ref, None)

    o_ref[...] = jnp.transpose(acc_ref[...] / l_ref[...]).astype(BF16)


def _attention(q, k, v):
    t = ATT_T
    nq = SEQ // t
    return pl.pallas_call(
        _attn_kernel,
        grid=(BATCH, MLA_HEADS, nq),
        in_specs=[pl.BlockSpec((1, 1, t, MLA_HD), lambda b, h, i: (b, h, i, 0)),
                  pl.BlockSpec((1, 1, SEQ, MLA_HD), lambda b, h, i: (b, h, 0, 0)),
                  pl.BlockSpec((1, 1, SEQ, MLA_V), lambda b, h, i: (b, h, 0, 0))],
        out_specs=pl.BlockSpec((t, MLA_V), lambda b, h, i: (b * nq + i, h)),
        out_shape=jax.ShapeDtypeStruct((N_TOK, MLA_HEADS * MLA_V), BF16),
        scratch_shapes=[pltpu.VMEM((1, t), F32), pltpu.VMEM((1, t), F32), pltpu.VMEM((MLA_V, t), F32),
                        pltpu.VMEM((t, t), F32), pltpu.VMEM((t, t), F32)],
        compiler_params=_cparams(("arbitrary", "arbitrary", "arbitrary")),
        name="mla_attn",
    )(q, k, v)


GLA_TG = 512
GLA_DIAG = 8
GLA_HPS = 4


def _gla_kernel(q_ref, k_ref, v_ref, glr_ref, gout_ref, wgk_ref, bgk_ref, nw_ref, o_ref, st_ref, g_ref):
    C = GLA_CHUNK

    @pl.when(pl.program_id(2) == 0)
    def _():
        st_ref[...] = jnp.zeros(st_ref.shape, F32)

    z = jnp.dot(glr_ref[...].astype(F32), wgk_ref[...], preferred_element_type=F32,
                precision=lax.Precision.HIGHEST) + bgk_ref[...]
    g_ref[...] = jax.nn.log_sigmoid(z) / GLA_GATE_NORM

    row = lax.broadcasted_iota(jnp.int32, (C, GLA_DK), 0)
    ii = lax.broadcasted_iota(jnp.int32, (C, C), 0)
    jj = lax.broadcasted_iota(jnp.int32, (C, C), 1)
    nt = (((1,), (1,)), ((), ()))

    def chunk(ci, carry):
        for hh in range(GLA_HPS):
            head_chunk(ci, hh)
        return carry

    def head_chunk(ci, hh):
        r0 = pl.multiple_of(ci * C, C)
        kcols = slice(hh * GLA_DK, (hh + 1) * GLA_DK)
        vcols = slice(hh * GLA_DV, (hh + 1) * GLA_DV)
        b = g_ref[pl.ds(r0, C), kcols]
        s = 1
        while s < C:
            b = b + jnp.where(row >= s, pltpu.roll(b, s, 0), 0.0)
            s *= 2
        q = q_ref[pl.ds(r0, C), kcols].astype(F32) * (GLA_DK ** -0.5)
        k = k_ref[pl.ds(r0, C), kcols].astype(F32)
        v = v_ref[pl.ds(r0, C), vcols]
        st = st_ref[hh]

        o = lax.dot_general((q * jnp.exp(b)).astype(BF16), st.astype(BF16), nt,
                            preferred_element_type=F32)

        a = jnp.zeros((C, C), F32)
        lvl = C // 2
        while lvl >= GLA_DIAG:
            pieces = [jnp.broadcast_to(b[t + lvl:t + lvl + 1, :], (2 * lvl, GLA_DK))
                      for t in range(0, C, 2 * lvl)]
            m = pieces[0] if len(pieces) == 1 else jnp.concatenate(pieces, axis=0)
            odd = (row // lvl) % 2 == 1
            ql = jnp.where(odd, q * jnp.exp(jnp.minimum(b - m, 0.0)), 0.0).astype(BF16)
            kl = jnp.where(odd, 0.0, k * jnp.exp(jnp.minimum(m - b, 0.0))).astype(BF16)
            al = lax.dot_general(ql, kl, nt, preferred_element_type=F32)
            a = a + jnp.where(ii // (2 * lvl) == jj // (2 * lvl), al, 0.0)
            lvl //= 2
        for d in range(GLA_DIAG):
            if d == 0:
                t = q * k
            else:
                t = q * pltpu.roll(k, d, 0) * jnp.exp(jnp.minimum(b - pltpu.roll(b, d, 0), 0.0))
            col = jnp.sum(t, axis=-1, keepdims=True)
            a = a + jnp.where((ii - jj == d) & (ii % GLA_DIAG >= d), col, 0.0)
        o = o + jnp.dot(a.astype(BF16), v, preferred_element_type=F32)

        b_last = b[C - 1:C, :]
        kdec = (k * jnp.exp(b_last - b)).astype(BF16)
        st_ref[hh] = st * jnp.exp(b_last) + lax.dot_general(
            v, kdec, (((0,), (0,)), ((), ())), preferred_element_type=F32)

        gate = gout_ref[pl.ds(r0, C), vcols].astype(F32)
        y = _rms(o, nw_ref[...]) * (gate * jax.nn.sigmoid(gate))
        o_ref[pl.ds(r0, C), vcols] = y.astype(BF16)

    lax.fori_loop(0, GLA_TG // C, chunk, 0)


def _gla(proj, wgk, bgk, nw):
    tg = GLA_TG
    per_b = SEQ // tg
    kw = GLA_HPS * GLA_DK
    vw = GLA_HPS * GLA_DV
    rb = lambda b, h, t: b * per_b + t
    return pl.pallas_call(
        _gla_kernel,
        grid=(BATCH, GLA_HEADS // GLA_HPS, per_b),
        in_specs=[pl.BlockSpec((tg, kw), lambda b, h, t: (rb(b, h, t), P_GQ // kw + h)),
                  pl.BlockSpec((tg, kw), lambda b, h, t: (rb(b, h, t), P_GK // kw + h)),
                  pl.BlockSpec((tg, vw), lambda b, h, t: (rb(b, h, t), P_GV // vw + h)),
                  pl.BlockSpec((tg, LANES), lambda b, h, t: (rb(b, h, t), P_GLR // LANES)),
                  pl.BlockSpec((tg, vw), lambda b, h, t: (rb(b, h, t), P_GOUT // vw + h)),
                  pl.BlockSpec((LANES, kw), lambda b, h, t: (0, h)),
                  pl.BlockSpec((1, kw), lambda b, h, t: (0, h)),
                  pl.BlockSpec((1, GLA_DV), lambda b, h, t: (0, 0))],
        out_specs=pl.BlockSpec((tg, vw), lambda b, h, t: (rb(b, h, t), h)),
        out_shape=jax.ShapeDtypeStruct((N_TOK, GLA_HEADS * GLA_DV), BF16),
        scratch_shapes=[pltpu.VMEM((GLA_HPS, GLA_DV, GLA_DK), F32), pltpu.VMEM((tg, kw), F32)],
        compiler_params=_cparams(("arbitrary", "arbitrary", "arbitrary")),
        name="gla",
    )(proj, proj, proj, proj, proj, wgk, bgk, nw)


MRG_TM = 256


def _merge_kernel(x_ref, oa_ref, ob_ref, ga_ref, gb_ref, gm_ref, shf_ref, scf_ref, nfw_ref,
                  woa_ref, wob_ref, wout_ref, wrh_ref, wrl_ref, br_ref,
                  x1_ref, h2_ref, idx_ref, tw_ref):
    ya = jnp.dot(oa_ref[...], woa_ref[...], preferred_element_type=F32)
    yb = jnp.dot(ob_ref[...], wob_ref[...], preferred_element_type=F32)
    merged = (jax.nn.sigmoid(ga_ref[...].astype(F32)) * ya
              + jax.nn.sigmoid(gb_ref[...].astype(F32)) * yb).astype(BF16)
    x1 = x_ref[...] + gm_ref[0] * jnp.dot(merged, wout_ref[...], preferred_element_type=F32)
    x1_ref[...] = x1
    h2 = _rms(x1, nfw_ref[...]) * (1.0 + scf_ref[0]) + shf_ref[0]
    for j in range(ROW_TILES):
        h2_ref[pl.ds(j, MRG_TM, stride=ROW_TILES), :] = h2[:, j * LANES:(j + 1) * LANES]

    hi = h2.astype(BF16)
    lo = (h2 - hi.astype(F32)).astype(BF16)
    logits = (jnp.dot(hi, wrh_ref[...], preferred_element_type=F32)
              + jnp.dot(hi, wrl_ref[...], preferred_element_type=F32)
              + jnp.dot(lo, wrh_ref[...], preferred_element_type=F32)) + br_ref[...]
    lane = lax.broadcasted_iota(jnp.int32, logits.shape, 1)
    vals = jnp.where(lane < N_EXPERTS, logits, -jnp.inf)
    idx_out = jnp.zeros(logits.shape, jnp.int32)
    w_out = jnp.zeros(logits.shape, F32)
    top = None
    denom = None
    for kk in range(TOP_K):
        m = jnp.max(vals, axis=-1, keepdims=True)
        sel = jnp.min(jnp.where(vals == m, lane, LANES), axis=-1, keepdims=True)
        if kk == 0:
            top = m
        e = jnp.exp(m - top)
        denom = e if kk == 0 else denom + e
        idx_out = jnp.where(lane == kk, sel, idx_out)
        w_out = jnp.where(lane == kk, e, w_out)
        vals = jnp.where(lane == sel, -jnp.inf, vals)
    idx_ref[...] = idx_out
    tw_ref[...] = w_out / denom


def _merge(x2d, o_mla, o_gla, proj, gm, shf, scf, nfw, woa, wob, wout, wrh, wrl, br):
    tm = MRG_TM
    per_b = SEQ // tm
    row = lambda i: (i, 0)
    const = lambda i: (0, 0)
    bvec = lambda i: (i // per_b, 0, 0)
    single = dict(pipeline_mode=pl.Buffered(1))
    return pl.pallas_call(
        _merge_kernel,
        grid=(N_TOK // tm,),
        in_specs=[pl.BlockSpec((tm, D_MODEL), row),
                  pl.BlockSpec((tm, MLA_HEADS * MLA_V), row),
                  pl.BlockSpec((tm, GLA_HEADS * GLA_DV), row),
                  pl.BlockSpec((tm, D_MODEL), lambda i: (i, P_GATEA // D_MODEL)),
                  pl.BlockSpec((tm, D_MODEL), lambda i: (i, P_GATEB // D_MODEL)),
                  pl.BlockSpec((1, 1, D_MODEL), bvec),
                  pl.BlockSpec((1, 1, D_MODEL), bvec),
                  pl.BlockSpec((1, 1, D_MODEL), bvec),
                  pl.BlockSpec((1, D_MODEL), const),
                  pl.BlockSpec((MLA_HEADS * MLA_V, D_MODEL), const, **single),
                  pl.BlockSpec((GLA_HEADS * GLA_DV, D_MODEL), const, **single),
                  pl.BlockSpec((D_MODEL, D_MODEL), const, **single),
                  pl.BlockSpec((D_MODEL, LANES), const, **single),
                  pl.BlockSpec((D_MODEL, LANES), const, **single),
                  pl.BlockSpec((1, LANES), const)],
        out_specs=[pl.BlockSpec((tm, D_MODEL), row),
                   pl.BlockSpec((tm * ROW_TILES, LANES), row),
                   pl.BlockSpec((tm, LANES), row),
                   pl.BlockSpec((tm, LANES), row)],
        out_shape=[jax.ShapeDtypeStruct((N_TOK, D_MODEL), F32),
                   jax.ShapeDtypeStruct((N_TOK * ROW_TILES, LANES), F32),
                   jax.ShapeDtypeStruct((N_TOK, LANES), jnp.int32),
                   jax.ShapeDtypeStruct((N_TOK, LANES), F32)],
        compiler_params=_cparams(("arbitrary",)),
        name="merge_router",
    )(x2d, o_mla, o_gla, proj, proj, gm, shf, scf, nfw, woa, wob, wout, wrh, wrl, br)


ROW_COPY_PRIORITY = 1
MOE_CHAINS = ((0, 2), (2, 2), (4, 1))
MOE_STEP_ROWS = MOE_ITEM_ROWS // MOE_NCHUNK
MOE_QUOTAS = tuple(MOE_STEP_ROWS * w // MOE_ITEM_BLKS for _, w in MOE_CHAINS)
assert sum(w for _, w in MOE_CHAINS) == MOE_ITEM_BLKS and sum(MOE_QUOTAS) == MOE_STEP_ROWS


def _moe_kernel(it_e_ref, it_start_ref, it_nblk_ref, src_ref, dst_ref,
                h2_hbm, w1_ref, b1_ref, w2_ref, b2_ref, sel_ref, y4_hbm,
                stage_ref, xb_ref, acc_ref, yout_ref, w1b_ref, w2b_ref, gsem, ssem):
    it = pl.program_id(0)
    c = pl.program_id(1)
    nblk = it_nblk_ref[it]
    prev = jnp.maximum(it - 1, 0)
    nblk_prev = jnp.where(it > 0, it_nblk_ref[prev], 0)
    live = nblk > 0
    prev_live = nblk_prev > 0
    rows_prev = nblk_prev * MOE_BLK
    rows_next = it_nblk_ref[it + 1] * MOE_BLK
    base_prev = it_start_ref[prev]
    base_next = it_start_ref[it + 1]

    def token_rows(t):
        return pl.ds(pl.multiple_of(t * ROW_TILES, ROW_TILES), ROW_TILES)

    def gather_row(r, base, nrows, priority=ROW_COPY_PRIORITY):
        tok = jnp.where(r < nrows, src_ref[jnp.minimum(base + r, MOE_SLOTS - 1)], 0)
        pltpu.make_async_copy(h2_hbm.at[token_rows(tok), :], stage_ref.at[token_rows(r), :],
                              gsem).start(priority=priority)

    def scatter_row(r, priority=ROW_COPY_PRIORITY):
        d = dst_ref[jnp.minimum(base_prev + r, MOE_SLOTS - 1)]
        d = jnp.where(jnp.logical_and(r < rows_prev, d >= 0), d, N_ASSIGN + r)
        pltpu.make_async_copy(yout_ref.at[token_rows(r), :], y4_hbm.at[token_rows(d), :],
                              ssem).start(priority=priority)

    def issue_quota(ci, gather=True):
        r0 = c * MOE_STEP_ROWS + sum(MOE_QUOTAS[:ci])
        for u in range(MOE_QUOTAS[ci]):
            if gather:
                gather_row(r0 + u, base_next, rows_next)
            scatter_row(r0 + u)

    @pl.when(jnp.logical_and(it == 0, c == 0))
    def _():
        yout_ref[...] = jnp.zeros(yout_ref.shape, F32)

        def first(r, carry):
            gather_row(r, it_start_ref[0], nblk * MOE_BLK)
            return carry
        lax.fori_loop(0, MOE_ITEM_ROWS, first, 0)

    @pl.when(jnp.logical_and(c == 0, jnp.logical_or(it == 0, prev_live)))
    def _():
        pltpu.make_async_copy(h2_hbm.at[pl.ds(0, MOE_ITEM_ROWS * ROW_TILES), :], stage_ref, gsem).wait()

    @pl.when(jnp.logical_and(c == 0, live))
    def _():
        for s in range(MOE_ITEM_BLKS):
            @pl.when(s < nblk)
            def _():
                rows = slice(s * MOE_BLK, (s + 1) * MOE_BLK)
                for j in range(ROW_TILES):
                    piece = stage_ref[pl.ds(s * MOE_BLK * ROW_TILES + j, MOE_BLK, stride=ROW_TILES), :]
                    xb_ref[rows, j * LANES:(j + 1) * LANES] = piece.astype(BF16)
                acc_ref[rows, :] = jnp.broadcast_to(b2_ref[0], (MOE_BLK, D_MODEL))

    def compute(row0, nrows):
        rows = slice(row0, row0 + nrows)
        hm = jnp.dot(xb_ref[rows, :], w1b_ref[...], preferred_element_type=F32) + b1_ref[0]
        lin = pltpu.roll(hm, 2 * MOE_TH - 1, 1)
        glu = jnp.minimum(hm, SWIGLU_LIMIT)
        lin = jnp.clip(lin, -SWIGLU_LIMIT, SWIGLU_LIMIT)
        act = glu * jax.nn.sigmoid(SWIGLU_ALPHA * glu) * (lin + 1.0)
        lane = lax.broadcasted_iota(jnp.int32, act.shape, 1)
        act = jnp.where(lane % 2 == 0, act, 0.0).astype(BF16)
        act = jnp.dot(act, sel_ref[...], preferred_element_type=F32).astype(BF16)
        acc_ref[rows, :] += jnp.dot(act, w2b_ref[...], preferred_element_type=F32)

    @pl.when(live)
    def _():
        w1b_ref[...] = w1_ref[0].astype(BF16)
        w2b_ref[...] = w2_ref[0].astype(BF16)

    for ci, (first, width) in enumerate(MOE_CHAINS):
        in_chain = jnp.clip(nblk - first, 0, width)
        for nb in range(width, 0, -1):
            @pl.when(in_chain == nb)
            def _():
                compute(first * MOE_BLK, nb * MOE_BLK)
                issue_quota(ci)

        @pl.when(jnp.logical_and(in_chain == 0, live))
        def _():
            issue_quota(ci)

        @pl.when(jnp.logical_and(jnp.logical_not(live), prev_live))
        def _():
            issue_quota(ci, gather=False)

    @pl.when(jnp.logical_and(c == MOE_NCHUNK - 1, jnp.logical_or(live, prev_live)))
    def _():
        pltpu.make_async_copy(yout_ref, y4_hbm.at[pl.ds(0, MOE_ITEM_ROWS * ROW_TILES), :], ssem).wait()

    @pl.when(jnp.logical_and(c == MOE_NCHUNK - 1, live))
    def _():
        for s in range(MOE_ITEM_BLKS):
            @pl.when(s < nblk)
            def _():
                rows = slice(s * MOE_BLK, (s + 1) * MOE_BLK)
                for j in range(ROW_TILES):
                    yout_ref[pl.ds(s * MOE_BLK * ROW_TILES + j, MOE_BLK, stride=ROW_TILES), :] = (
                        acc_ref[rows, j * LANES:(j + 1) * LANES])


def _moe(it_e, it_start, it_nblk, slot_src, slot_dst, h2, w1, b1, w2, b2):
    def chunk_of(it, c, n_ref):
        return jnp.where(n_ref[it] > 0, c, MOE_NCHUNK - 1)

    def w1_map(it, c, e_ref, s_ref, n_ref, *_):
        return (e_ref[it], 0, chunk_of(it, c, n_ref))

    def w2_map(it, c, e_ref, s_ref, n_ref, *_):
        return (e_ref[it], chunk_of(it, c, n_ref), 0)

    def b2_map(it, c, e_ref, *_):
        return (e_ref[it], 0, 0)

    sel = (jnp.arange(2 * MOE_TH, dtype=jnp.int32)[:, None]
           == 2 * jnp.arange(MOE_TH, dtype=jnp.int32)[None, :]).astype(BF16)

    grid_spec = pltpu.PrefetchScalarGridSpec(
        num_scalar_prefetch=5,
        grid=(MOE_NITEMS + 1, MOE_NCHUNK),
        in_specs=[pl.BlockSpec(memory_space=pl.ANY),
                  pl.BlockSpec((1, D_MODEL, 2 * MOE_TH), w1_map),
                  pl.BlockSpec((1, 1, 2 * MOE_TH), w1_map),
                  pl.BlockSpec((1, MOE_TH, D_MODEL), w2_map),
                  pl.BlockSpec((1, 1, D_MODEL), b2_map),
                  pl.BlockSpec((2 * MOE_TH, MOE_TH), lambda it, c, *_: (0, 0))],
        out_specs=pl.BlockSpec(memory_space=pl.ANY),
        scratch_shapes=[pltpu.VMEM((MOE_ITEM_ROWS * ROW_TILES, LANES), F32),
                        pltpu.VMEM((MOE_ITEM_ROWS, D_MODEL), BF16),
                        pltpu.VMEM((MOE_ITEM_ROWS, D_MODEL), F32),
                        pltpu.VMEM((MOE_ITEM_ROWS * ROW_TILES, LANES), F32),
                        pltpu.VMEM((D_MODEL, 2 * MOE_TH), BF16),
                        pltpu.VMEM((MOE_TH, D_MODEL), BF16),
                        pltpu.SemaphoreType.DMA(()),
                        pltpu.SemaphoreType.DMA(())],
    )
    return pl.pallas_call(
        _moe_kernel,
        grid_spec=grid_spec,
        out_shape=jax.ShapeDtypeStruct(((N_ASSIGN + MOE_ITEM_ROWS) * ROW_TILES, LANES), F32),
        compiler_params=_cparams(("arbitrary", "arbitrary")),
        name="moe_experts",
    )(it_e, it_start, it_nblk, slot_src, slot_dst, h2, w1,
      b1.reshape(N_EXPERTS, 1, 2 * D_EXPERT), w2, b2.reshape(N_EXPERTS, 1, D_MODEL), sel)


def _route_tables(top_idx):
    e_flat = top_idx.reshape(N_ASSIGN)
    onehot = (e_flat[:, None] == jnp.arange(N_EXPERTS, dtype=jnp.int32)[None, :]).astype(jnp.int32)
    csum = jnp.cumsum(onehot, axis=0)
    rank = jnp.sum(csum * onehot, axis=-1) - 1
    counts = csum[-1]
    nblk = (counts + MOE_BLK - 1) // MOE_BLK
    blk_end = jnp.cumsum(nblk)
    blk_start = blk_end - nblk
    dest = blk_start[e_flat] * MOE_BLK + rank
    slot_a = jnp.full((MOE_SLOTS,), -1, jnp.int32).at[dest].set(jnp.arange(N_ASSIGN, dtype=jnp.int32))
    slot_src = jnp.where(slot_a >= 0, slot_a // TOP_K, 0)
    slot_dst = jnp.where(slot_a >= 0, (slot_a % TOP_K) * N_TOK + slot_a // TOP_K, -1)

    nit = (nblk + MOE_ITEM_BLKS - 1) // MOE_ITEM_BLKS
    it_end = jnp.cumsum(nit)
    it_begin = it_end - nit
    n_used = it_end[-1]
    ids = jnp.arange(MOE_NITEMS + 2, dtype=jnp.int32)
    valid = ids < n_used
    last = jnp.minimum(ids, n_used - 1)
    e_of = jnp.minimum(jnp.sum((it_end[None, :] <= last[:, None]).astype(jnp.int32), axis=1), N_EXPERTS - 1)
    local = ids - it_begin[e_of]
    it_start = jnp.where(valid, (blk_start[e_of] + local * MOE_ITEM_BLKS) * MOE_BLK, 0).astype(jnp.int32)
    it_nblk = jnp.where(valid, jnp.clip(nblk[e_of] - local * MOE_ITEM_BLKS, 0, MOE_ITEM_BLKS), 0)
    return e_of, it_start, it_nblk.astype(jnp.int32), slot_src, slot_dst


FIN_TM = 512


def _final_kernel(x1_ref, ya_ref, yb_ref, yc_ref, yd_ref, tw_ref, gf_ref, sho_ref, sco_ref, nw_ref, o_ref):
    tw = tw_ref[...]
    pieces = []
    for j in range(ROW_TILES):
        acc = None
        for kk, y_ref in enumerate((ya_ref, yb_ref, yc_ref, yd_ref)):
            term = tw[:, kk:kk + 1] * y_ref[pl.ds(j, FIN_TM, stride=ROW_TILES), :]
            acc = term if acc is None else acc + term
        pieces.append(acc)
    moe = jnp.concatenate(pieces, axis=1)
    x2 = x1_ref[...] + gf_ref[0] * moe
    o_ref[...] = _rms(x2, nw_ref[...]) * (1.0 + sco_ref[0]) + sho_ref[0]


def _final(x1, y4, tw, gf, sho, sco, nw):
    tm = FIN_TM
    per_b = SEQ // tm
    row = lambda i: (i, 0)
    bvec = lambda i: (i // per_b, 0, 0)
    nb = N_TOK // tm
    kth = lambda kk: pl.BlockSpec((tm * ROW_TILES, LANES), lambda i: (kk * nb + i, 0))
    return pl.pallas_call(
        _final_kernel,
        grid=(nb,),
        in_specs=[pl.BlockSpec((tm, D_MODEL), row),
                  kth(0), kth(1), kth(2), kth(3),
                  pl.BlockSpec((tm, LANES), row),
                  pl.BlockSpec((1, 1, D_MODEL), bvec),
                  pl.BlockSpec((1, 1, D_MODEL), bvec),
                  pl.BlockSpec((1, 1, D_MODEL), bvec),
                  pl.BlockSpec((1, D_MODEL), lambda i: (0, 0))],
        out_specs=pl.BlockSpec((tm, D_MODEL), row),
        out_shape=jax.ShapeDtypeStruct((N_TOK, D_MODEL), F32),
        compiler_params=_cparams(("arbitrary",)),
        name="combine_final",
    )(x1, y4, y4, y4, y4, tw, gf, sho, sco, nw)


def _deinterleave(n):
    return np.concatenate([np.arange(0, n, 2), np.arange(1, n, 2)])


_IN_OFFS = np.cumsum([0, MLA_Q_RANK, MLA_KV_RANK + MLA_ROPE, GLA_HEADS * GLA_DK, GLA_HEADS * GLA_DK,
                      GLA_HEADS * GLA_DV, GLA_GATE_RANK, GLA_HEADS * GLA_DV, D_MODEL, D_MODEL])
_IN_MOVES = ((_IN_OFFS[0], MLA_Q_RANK, P_QLAT), (_IN_OFFS[1], MLA_KV_RANK, P_CKV),
             (_IN_OFFS[5], GLA_GATE_RANK, P_GLR), (_IN_OFFS[2], GLA_HEADS * GLA_DK, P_GQ),
             (_IN_OFFS[3], GLA_HEADS * GLA_DK, P_GK), (_IN_OFFS[4], GLA_HEADS * GLA_DV, P_GV),
             (_IN_OFFS[6], GLA_HEADS * GLA_DV, P_GOUT), (_IN_OFFS[7], D_MODEL, P_GATEA),
             (_IN_OFFS[8], D_MODEL, P_GATEB))
PACK_TM = 256


def _pack_kernel(w_ref, kpe_ref, o_ref):
    o_ref[:, P_KPE:P_KPE + LANES] = kpe_ref[...]
    o_ref[:, P_GLR:P_GLR + LANES] = jnp.zeros((PACK_TM, LANES), BF16)
    for src, width, dst in _IN_MOVES:
        o_ref[:, dst:dst + width] = w_ref[:, int(src):int(src) + width].astype(BF16)


def _pack_w_in(w_in):
    d_in = w_in.shape[1]
    kpe = w_in[:, _IN_OFFS[1] + MLA_KV_RANK:_IN_OFFS[2]][:, _deinterleave(MLA_ROPE)]
    kpe = jnp.concatenate([kpe, jnp.zeros((D_MODEL, LANES - MLA_ROPE), w_in.dtype)], axis=1).astype(BF16)
    return pl.pallas_call(
        _pack_kernel,
        grid=(D_MODEL // PACK_TM,),
        in_specs=[pl.BlockSpec((PACK_TM, d_in), lambda i: (i, 0)),
                  pl.BlockSpec((PACK_TM, LANES), lambda i: (i, 0))],
        out_specs=pl.BlockSpec((PACK_TM, P_TOTAL), lambda i: (i, 0)),
        out_shape=jax.ShapeDtypeStruct((D_MODEL, P_TOTAL), BF16),
        compiler_params=_cparams(("arbitrary",)),
        name="pack_w_in",
    )(w_in, kpe)


def _pack_w_q_b(w):
    w = w.reshape(MLA_Q_RANK, MLA_HEADS, MLA_NOPE + MLA_ROPE)
    nope = w[:, :, :MLA_NOPE]
    pe = w[:, :, MLA_NOPE:][:, :, _deinterleave(MLA_ROPE)]
    pad = jnp.zeros((MLA_Q_RANK, MLA_HEADS, MLA_HD - MLA_NOPE - MLA_ROPE), w.dtype)
    return jnp.concatenate([nope, pe, pad], axis=-1).reshape(MLA_Q_RANK, MLA_HEADS * MLA_HD).astype(BF16)


def _rope_consts():
    inv_freq = ROPE_THETA ** (-(jnp.arange(0, MLA_ROPE, 2, dtype=F32) / MLA_ROPE))
    half = MLA_ROPE // 2
    z = jnp.zeros((LANES - MLA_ROPE,), F32)
    rows = [jnp.concatenate([inv_freq, inv_freq, z]),
            jnp.concatenate([-jnp.ones((half,), F32), jnp.zeros((half,), F32), z]),
            jnp.concatenate([jnp.zeros((half,), F32), jnp.ones((half,), F32), z])]
    return jnp.concatenate([jnp.stack(rows), jnp.zeros((5, LANES), F32)], axis=0)


def kernel(x, c, positions, w_ada, b_ada, norm_mix_w, w_in, mla_q_norm_w, mla_w_q_b, mla_kv_norm_w,
           mla_w_kv_b, gla_w_gk_up, gla_b_gk_up, gla_norm_w, w_o_mla, w_o_gla, w_out, norm_ffn_w,
           w_router, b_router, w1, b1, w2, b2, w_ada_final, b_ada_final, norm_final_w):
    assert w_ada.shape[0] == 1, "single-layer stack"
    x2d = x.reshape(N_TOK, D_MODEL)
    c_t = c.T

    mod = _adaln(c_t, w_ada[0], b_ada[0])
    fmod = _adaln(c_t, w_ada_final, b_ada_final)
    sh_m, sc_m, g_m, sh_f, sc_f, g_f = [m.reshape(BATCH, 1, D_MODEL) for m in jnp.split(mod, N_MOD, axis=-1)]
    sh_o, sc_o = [m.reshape(BATCH, 1, D_MODEL) for m in jnp.split(fmod, 2, axis=-1)]

    proj = _inproj(x2d, norm_mix_w[0].reshape(1, D_MODEL), sh_m, sc_m, _pack_w_in(w_in[0]))

    q, k, v = _mla_prep(proj, positions.reshape(N_TOK, 1), _rope_consts(),
                        mla_q_norm_w[0].reshape(1, MLA_Q_RANK), mla_kv_norm_w[0].reshape(1, MLA_KV_RANK),
                        _pack_w_q_b(mla_w_q_b[0]), mla_w_kv_b[0].astype(BF16))
    o_mla = _attention(q, k, v)

    wgk = jnp.concatenate([gla_w_gk_up[0], jnp.zeros((LANES - GLA_GATE_RANK, GLA_HEADS * GLA_DK), F32)], axis=0)
    o_gla = _gla(proj, wgk, gla_b_gk_up[0].reshape(1, GLA_HEADS * GLA_DK), gla_norm_w[0].reshape(1, GLA_DV))

    wr = jnp.concatenate([w_router[0], jnp.zeros((D_MODEL, LANES - N_EXPERTS), F32)], axis=1)
    wr_hi = wr.astype(BF16)
    wr_lo = (wr - wr_hi.astype(F32)).astype(BF16)
    br = jnp.concatenate([b_router[0], jnp.zeros((LANES - N_EXPERTS,), F32)]).reshape(1, LANES)
    x1, h2, top_idx, top_w = _merge(
        x2d, o_mla, o_gla, proj, g_m, sh_f, sc_f, norm_ffn_w[0].reshape(1, D_MODEL),
        w_o_mla[0].astype(BF16), w_o_gla[0].astype(BF16), w_out[0].astype(BF16), wr_hi, wr_lo, br)

    tables = _route_tables(top_idx[:, :TOP_K])
    y4 = _moe(*tables, h2, w1[0], b1[0], w2[0], b2[0])

    out = _final(x1, y4, top_w, g_f, sh_o, sc_o, norm_final_w.reshape(1, D_MODEL))
    return out.reshape(BATCH, SEQ, D_MODEL)
```

```python
import functools

import jax
import jax.numpy as jnp
import numpy as np
from jax import lax
from jax.experimental import pallas as pl
from jax.experimental.pallas import tpu as pltpu

F32 = jnp.float32
BF16 = jnp.bfloat16

D_MODEL = 2048
BATCH = 2
SEQ = 4096
N_TOK = BATCH * SEQ

MLA_HEADS = 8
MLA_Q_RANK = 512
MLA_KV_RANK = 256
MLA_NOPE = 128
MLA_ROPE = 64
MLA_V = 128
ROPE_THETA = 10000.0
MLA_HD = 256

GLA_HEADS = 4
GLA_DK = 128
GLA_DV = 256
GLA_GATE_RANK = 16
GLA_GATE_NORM = 16.0
GLA_CHUNK = 64

N_EXPERTS = 32
TOP_K = 4
D_EXPERT = D_MODEL
SWIGLU_LIMIT = 7.0
SWIGLU_ALPHA = 1.702
EPS = 1e-6
N_MOD = 6

LANES = 128
ROW_TILES = D_MODEL // LANES
VMEM_LIMIT = 56 * 1024 * 1024

P_QLAT = 0
P_CKV = 512
P_KPE = 768
P_GLR = 896
P_GQ = 1024
P_GK = 1536
P_GV = 2048
P_GOUT = 3072
P_GATEA = 4096
P_GATEB = 6144
P_TOTAL = 8192

MOE_BLK = 256
MOE_ITEM_BLKS = 5
MOE_ITEM_ROWS = MOE_BLK * MOE_ITEM_BLKS
MOE_TH = 256
MOE_NCHUNK = D_EXPERT // MOE_TH
N_ASSIGN = N_TOK * TOP_K
MOE_NBLK_MAX = N_ASSIGN // MOE_BLK + N_EXPERTS
MOE_SLOTS = MOE_NBLK_MAX * MOE_BLK
MOE_NITEMS = N_ASSIGN // MOE_ITEM_ROWS + N_EXPERTS


def _cparams(sem, vmem=VMEM_LIMIT):
    return pltpu.CompilerParams(dimension_semantics=sem, vmem_limit_bytes=vmem)


def _rms(x, w):
    return x * lax.rsqrt(jnp.mean(x * x, axis=-1, keepdims=True) + EPS) * w


def _adaln_kernel(ct_ref, w_ref, b_ref, o_ref):
    ct = ct_ref[...]
    cond = ct * jax.nn.sigmoid(ct)
    w = w_ref[...]
    for b in range(BATCH):
        o_ref[b:b + 1, :] = jnp.sum(w * cond[:, b:b + 1], axis=0, keepdims=True) + b_ref[...]


def _adaln(c_t, w, b, tn=1024):
    d, n = w.shape
    return pl.pallas_call(
        _adaln_kernel,
        grid=(n // tn,),
        in_specs=[pl.BlockSpec((d, BATCH), lambda j: (0, 0)),
                  pl.BlockSpec((d, tn), lambda j: (0, j)),
                  pl.BlockSpec((1, tn), lambda j: (0, j))],
        out_specs=pl.BlockSpec((BATCH, tn), lambda j: (0, j)),
        out_shape=jax.ShapeDtypeStruct((BATCH, n), F32),
        compiler_params=_cparams(("arbitrary",)),
        name="adaln",
    )(c_t, w, b.reshape(1, n))


IN_TM = 1024
IN_TN = 2048


def _inproj_kernel(x_ref, nw_ref, sh_ref, sc_ref, w_ref, o_ref, h_ref):
    @pl.when(pl.program_id(1) == 0)
    def _():
        for r in range(0, IN_TM, 256):
            x = x_ref[r:r + 256, :]
            h = _rms(x, nw_ref[...]) * (1.0 + sc_ref[0]) + sh_ref[0]
            h_ref[r:r + 256, :] = h.astype(BF16)

    o_ref[...] = jnp.dot(h_ref[...], w_ref[...], preferred_element_type=F32).astype(BF16)


def _inproj(x2d, nw, sh, sc, w_packed):
    per_b = SEQ // IN_TM
    return pl.pallas_call(
        _inproj_kernel,
        grid=(N_TOK // IN_TM, P_TOTAL // IN_TN),
        in_specs=[pl.BlockSpec((IN_TM, D_MODEL), lambda i, j: (i, 0)),
                  pl.BlockSpec((1, D_MODEL), lambda i, j: (0, 0)),
                  pl.BlockSpec((1, 1, D_MODEL), lambda i, j: (i // per_b, 0, 0)),
                  pl.BlockSpec((1, 1, D_MODEL), lambda i, j: (i // per_b, 0, 0)),
                  pl.BlockSpec((D_MODEL, IN_TN), lambda i, j: (0, j))],
        out_specs=pl.BlockSpec((IN_TM, IN_TN), lambda i, j: (i, j)),
        out_shape=jax.ShapeDtypeStruct((N_TOK, P_TOTAL), BF16),
        scratch_shapes=[pltpu.VMEM((IN_TM, D_MODEL), BF16)],
        compiler_params=_cparams(("arbitrary", "arbitrary")),
        name="inproj",
    )(x2d, nw, sh, sc, w_packed)


PREP_TM = 512


def _mla_prep_kernel(ql_ref, ckv_ref, kpe_ref, pos_ref, rc_ref, qnw_ref, kvnw_ref, wq_ref, wkv_ref,
                     q_ref, k_ref, v_ref):
    scale = (MLA_NOPE + MLA_ROPE) ** -0.5
    qn = _rms(ql_ref[...].astype(F32), qnw_ref[...]).astype(BF16)
    q = jnp.dot(qn, wq_ref[...], preferred_element_type=F32)
    cn = _rms(ckv_ref[...].astype(F32), kvnw_ref[...]).astype(BF16)
    kv = jnp.dot(cn, wkv_ref[...], preferred_element_type=F32)

    ang = pos_ref[...].astype(F32) * rc_ref[0:1, :]
    cos_t = jnp.cos(ang)
    sin_t = jnp.sin(ang)
    sin_a = sin_t * rc_ref[1:2, :]
    sin_b = sin_t * rc_ref[2:3, :]

    def rope(t):
        return (t * cos_t + pltpu.roll(t, LANES - MLA_ROPE // 2, 1) * sin_a
                + pltpu.roll(t, MLA_ROPE // 2, 1) * sin_b)

    kpe = rope(kpe_ref[...].astype(F32)).astype(BF16)
    for h in range(MLA_HEADS):
        c0 = h * MLA_HD
        q_ref[0, h, :, 0:LANES] = (q[:, c0:c0 + LANES] * scale).astype(BF16)
        q_ref[0, h, :, LANES:MLA_HD] = (rope(q[:, c0 + LANES:c0 + MLA_HD]) * scale).astype(BF16)
        k_ref[0, h, :, 0:LANES] = kv[:, c0:c0 + LANES].astype(BF16)
        k_ref[0, h, :, LANES:MLA_HD] = kpe
        v_ref[0, h, :, :] = kv[:, c0 + LANES:c0 + MLA_HD].astype(BF16)


def _mla_prep(proj, pos_col, rope_consts, qnw, kvnw, wq, wkv):
    tm = PREP_TM
    per_b = SEQ // tm
    row = lambda i: (i, 0)
    const = lambda i: (0, 0)
    out_map = lambda i: (i // per_b, 0, i % per_b, 0)
    return pl.pallas_call(
        _mla_prep_kernel,
        grid=(N_TOK // tm,),
        in_specs=[pl.BlockSpec((tm, MLA_Q_RANK), lambda i: (i, P_QLAT // MLA_Q_RANK)),
                  pl.BlockSpec((tm, MLA_KV_RANK), lambda i: (i, P_CKV // MLA_KV_RANK)),
                  pl.BlockSpec((tm, LANES), lambda i: (i, P_KPE // LANES)),
                  pl.BlockSpec((tm, 1), row),
                  pl.BlockSpec((8, LANES), const),
                  pl.BlockSpec((1, MLA_Q_RANK), const),
                  pl.BlockSpec((1, MLA_KV_RANK), const),
                  pl.BlockSpec((MLA_Q_RANK, MLA_HEADS * MLA_HD), const),
                  pl.BlockSpec((MLA_KV_RANK, MLA_HEADS * MLA_HD), const)],
        out_specs=[pl.BlockSpec((1, MLA_HEADS, tm, MLA_HD), out_map),
                   pl.BlockSpec((1, MLA_HEADS, tm, MLA_HD), out_map),
                   pl.BlockSpec((1, MLA_HEADS, tm, MLA_V), out_map)],
        out_shape=[jax.ShapeDtypeStruct((BATCH, MLA_HEADS, SEQ, MLA_HD), BF16),
                   jax.ShapeDtypeStruct((BATCH, MLA_HEADS, SEQ, MLA_HD), BF16),
                   jax.ShapeDtypeStruct((BATCH, MLA_HEADS, SEQ, MLA_V), BF16)],
        compiler_params=_cparams(("arbitrary",)),
        name="mla_prep",
    )(proj, proj, proj, pos_col, rope_consts, qnw, kvnw, wq, wkv)


ATT_T = 1024


def _attn_kernel(q_ref, k_ref, v_ref, o_ref, m_ref, l_ref, acc_ref, s0_ref, s1_ref):
    qi = pl.program_id(2)
    q = q_ref[0, 0]
    m_ref[...] = jnp.full(m_ref.shape, -jnp.inf, F32)
    l_ref[...] = jnp.zeros(l_ref.shape, F32)
    acc_ref[...] = jnp.zeros(acc_ref.shape, F32)

    def scores(j):
        r0 = pl.multiple_of(j * ATT_T, ATT_T)
        k = k_ref[0, 0, pl.ds(r0, ATT_T), :]
        return lax.dot_general(k, q, (((1,), (1,)), ((), ())), preferred_element_type=F32)

    def step(j, src_ref, dst_ref):
        masked = dst_ref is None
        r0 = pl.multiple_of(j * ATT_T, ATT_T)
        v = v_ref[0, 0, pl.ds(r0, ATT_T), :]
        s = src_ref[...]
        if not masked:
            dst_ref[...] = scores(j + 1)
        if masked:
            key = lax.broadcasted_iota(jnp.int32, s.shape, 0)
            qry = lax.broadcasted_iota(jnp.int32, s.shape, 1)
            s = jnp.where(key <= qry, s, -jnp.inf)
        m_old = m_ref[...]
        m_new = jnp.maximum(m_old, jnp.max(s, axis=0, keepdims=True))
        p = jnp.exp(s - m_new)
        alpha = jnp.exp(m_old - m_new)
        l_ref[...] = alpha * l_ref[...] + jnp.sum(p, axis=0, keepdims=True)
        pv = lax.dot_general(v, p.astype(BF16), (((0,), (0,)), ((), ())), preferred_element_type=F32)
        acc_ref[...] = alpha * acc_ref[...] + pv
        m_ref[...] = m_new

    def body(jj, carry):
        step(2 * jj, s0_ref, s1_ref)
        step(2 * jj + 1, s1_ref, s0_ref)
        return carry

    s0_ref[...] = scores(0)
    lax.fori_loop(0, qi // 2, body, 0)

    @pl.when(qi % 2 == 1)
    def _():
        step(qi - 1, s0_ref, s1_ref)
        step(qi, s1_ref, None)

    @pl.when(qi % 2 == 0)
    def _():
        step(qi, s0_ref, None)

    o_ref[...] = jnp.transpose(acc_ref[...] / l_ref[...]).astype(BF16)


def _attention(q, k, v):
    t = ATT_T
    nq = SEQ // t
    return pl.pallas_call(
        _attn_kernel,
        grid=(BATCH, MLA_HEADS, nq),
        in_specs=[pl.BlockSpec((1, 1, t, MLA_HD), lambda b, h, i: (b, h, i, 0)),
                  pl.BlockSpec((1, 1, SEQ, MLA_HD), lambda b, h, i: (b, h, 0, 0)),
                  pl.BlockSpec((1, 1, SEQ, MLA_V), lambda b, h, i: (b, h, 0, 0))],
        out_specs=pl.BlockSpec((t, MLA_V), lambda b, h, i: (b * nq + i, h)),
        out_shape=jax.ShapeDtypeStruct((N_TOK, MLA_HEADS * MLA_V), BF16),
        scratch_shapes=[pltpu.VMEM((1, t), F32), pltpu.VMEM((1, t), F32), pltpu.VMEM((MLA_V, t), F32),
                        pltpu.VMEM((t, t), F32), pltpu.VMEM((t, t), F32)],
        compiler_params=_cparams(("arbitrary", "arbitrary", "arbitrary")),
        name="mla_attn",
    )(q, k, v)


GLA_TG = 512
GLA_DIAG = 8
GLA_HPS = 4


def _gla_kernel(q_ref, k_ref, v_ref, glr_ref, gout_ref, wgk_ref, bgk_ref, nw_ref, o_ref, st_ref, g_ref):
    C = GLA_CHUNK

    @pl.when(pl.program_id(2) == 0)
    def _():
        st_ref[...] = jnp.zeros(st_ref.shape, F32)

    z = jnp.dot(glr_ref[...].astype(F32), wgk_ref[...], preferred_element_type=F32,
                precision=lax.Precision.HIGHEST) + bgk_ref[...]
    g_ref[...] = jax.nn.log_sigmoid(z) / GLA_GATE_NORM

    row = lax.broadcasted_iota(jnp.int32, (C, GLA_DK), 0)
    ii = lax.broadcasted_iota(jnp.int32, (C, C), 0)
    jj = lax.broadcasted_iota(jnp.int32, (C, C), 1)
    nt = (((1,), (1,)), ((), ()))

    def chunk(ci, carry):
        for hh in range(GLA_HPS):
            head_chunk(ci, hh)
        return carry

    def head_chunk(ci, hh):
        r0 = pl.multiple_of(ci * C, C)
        kcols = slice(hh * GLA_DK, (hh + 1) * GLA_DK)
        vcols = slice(hh * GLA_DV, (hh + 1) * GLA_DV)
        b = g_ref[pl.ds(r0, C), kcols]
        s = 1
        while s < C:
            b = b + jnp.where(row >= s, pltpu.roll(b, s, 0), 0.0)
            s *= 2
        q = q_ref[pl.ds(r0, C), kcols].astype(F32) * (GLA_DK ** -0.5)
        k = k_ref[pl.ds(r0, C), kcols].astype(F32)
        v = v_ref[pl.ds(r0, C), vcols]
        st = st_ref[hh]

        o = lax.dot_general((q * jnp.exp(b)).astype(BF16), st.astype(BF16), nt,
                            preferred_element_type=F32)

        a = jnp.zeros((C, C), F32)
        lvl = C // 2
        while lvl >= GLA_DIAG:
            pieces = [jnp.broadcast_to(b[t + lvl:t + lvl + 1, :], (2 * lvl, GLA_DK))
                      for t in range(0, C, 2 * lvl)]
            m = pieces[0] if len(pieces) == 1 else jnp.concatenate(pieces, axis=0)
            odd = (row // lvl) % 2 == 1
            ql = jnp.where(odd, q * jnp.exp(jnp.minimum(b - m, 0.0)), 0.0).astype(BF16)
            kl = jnp.where(odd, 0.0, k * jnp.exp(jnp.minimum(m - b, 0.0))).astype(BF16)
            al = lax.dot_general(ql, kl, nt, preferred_element_type=F32)
            a = a + jnp.where(ii // (2 * lvl) == jj // (2 * lvl), al, 0.0)
            lvl //= 2
        for d in range(GLA_DIAG):
            if d == 0:
                t = q * k
            else:
                t = q * pltpu.roll(k, d, 0) * jnp.exp(jnp.minimum(b - pltpu.roll(b, d, 0), 0.0))
            col = jnp.sum(t, axis=-1, keepdims=True)
            a = a + jnp.where((ii - jj == d) & (ii % GLA_DIAG >= d), col, 0.0)
        o = o + jnp.dot(a.astype(BF16), v, preferred_element_type=F32)

        b_last = b[C - 1:C, :]
        kdec = (k * jnp.exp(b_last - b)).astype(BF16)
        st_ref[hh] = st * jnp.exp(b_last) + lax.dot_general(
            v, kdec, (((0,), (0,)), ((), ())), preferred_element_type=F32)

        gate = gout_ref[pl.ds(r0, C), vcols].astype(F32)
        y = _rms(o, nw_ref[...]) * (gate * jax.nn.sigmoid(gate))
        o_ref[pl.ds(r0, C), vcols] = y.astype(BF16)

    lax.fori_loop(0, GLA_TG // C, chunk, 0)


def _gla(proj, wgk, bgk, nw):
    tg = GLA_TG
    per_b = SEQ // tg
    kw = GLA_HPS * GLA_DK
    vw = GLA_HPS * GLA_DV
    rb = lambda b, h, t: b * per_b + t
    return pl.pallas_call(
        _gla_kernel,
        grid=(BATCH, GLA_HEADS // GLA_HPS, per_b),
        in_specs=[pl.BlockSpec((tg, kw), lambda b, h, t: (rb(b, h, t), P_GQ // kw + h)),
                  pl.BlockSpec((tg, kw), lambda b, h, t: (rb(b, h, t), P_GK // kw + h)),
                  pl.BlockSpec((tg, vw), lambda b, h, t: (rb(b, h, t), P_GV // vw + h)),
                  pl.BlockSpec((tg, LANES), lambda b, h, t: (rb(b, h, t), P_GLR // LANES)),
                  pl.BlockSpec((tg, vw), lambda b, h, t: (rb(b, h, t), P_GOUT // vw + h)),
                  pl.BlockSpec((LANES, kw), lambda b, h, t: (0, h)),
                  pl.BlockSpec((1, kw), lambda b, h, t: (0, h)),
                  pl.BlockSpec((1, GLA_DV), lambda b, h, t: (0, 0))],
        out_specs=pl.BlockSpec((tg, vw), lambda b, h, t: (rb(b, h, t), h)),
        out_shape=jax.ShapeDtypeStruct((N_TOK, GLA_HEADS * GLA_DV), BF16),
        scratch_shapes=[pltpu.VMEM((GLA_HPS, GLA_DV, GLA_DK), F32), pltpu.VMEM((tg, kw), F32)],
        compiler_params=_cparams(("arbitrary", "arbitrary", "arbitrary")),
        name="gla",
    )(proj, proj, proj, proj, proj, wgk, bgk, nw)


MRG_TM = 256


def _merge_kernel(x_ref, oa_ref, ob_ref, ga_ref, gb_ref, gm_ref, shf_ref, scf_ref, nfw_ref,
                  woa_ref, wob_ref, wout_ref, wrh_ref, wrl_ref, br_ref,
                  x1_ref, h2_ref, idx_ref, tw_ref):
    ya = jnp.dot(oa_ref[...], woa_ref[...], preferred_element_type=F32)
    yb = jnp.dot(ob_ref[...], wob_ref[...], preferred_element_type=F32)
    merged = (jax.nn.sigmoid(ga_ref[...].astype(F32)) * ya
              + jax.nn.sigmoid(gb_ref[...].astype(F32)) * yb).astype(BF16)
    x1 = x_ref[...] + gm_ref[0] * jnp.dot(merged, wout_ref[...], preferred_element_type=F32)
    x1_ref[...] = x1
    h2 = _rms(x1, nfw_ref[...]) * (1.0 + scf_ref[0]) + shf_ref[0]
    for j in range(ROW_TILES):
        h2_ref[pl.ds(j, MRG_TM, stride=ROW_TILES), :] = h2[:, j * LANES:(j + 1) * LANES]

    hi = h2.astype(BF16)
    lo = (h2 - hi.astype(F32)).astype(BF16)
    logits = (jnp.dot(hi, wrh_ref[...], preferred_element_type=F32)
              + jnp.dot(hi, wrl_ref[...], preferred_element_type=F32)
              + jnp.dot(lo, wrh_ref[...], preferred_element_type=F32)) + br_ref[...]
    lane = lax.broadcasted_iota(jnp.int32, logits.shape, 1)
    vals = jnp.where(lane < N_EXPERTS, logits, -jnp.inf)
    idx_out = jnp.zeros(logits.shape, jnp.int32)
    w_out = jnp.zeros(logits.shape, F32)
    top = None
    denom = None
    for kk in range(TOP_K):
        m = jnp.max(vals, axis=-1, keepdims=True)
        sel = jnp.min(jnp.where(vals == m, lane, LANES), axis=-1, keepdims=True)
        if kk == 0:
            top = m
        e = jnp.exp(m - top)
        denom = e if kk == 0 else denom + e
        idx_out = jnp.where(lane == kk, sel, idx_out)
        w_out = jnp.where(lane == kk, e, w_out)
        vals = jnp.where(lane == sel, -jnp.inf, vals)
    idx_ref[...] = idx_out
    tw_ref[...] = w_out / denom


def _merge(x2d, o_mla, o_gla, proj, gm, shf, scf, nfw, woa, wob, wout, wrh, wrl, br):
    tm = MRG_TM
    per_b = SEQ // tm
    row = lambda i: (i, 0)
    const = lambda i: (0, 0)
    bvec = lambda i: (i // per_b, 0, 0)
    single = dict(pipeline_mode=pl.Buffered(1))
    return pl.pallas_call(
        _merge_kernel,
        grid=(N_TOK // tm,),
        in_specs=[pl.BlockSpec((tm, D_MODEL), row),
                  pl.BlockSpec((tm, MLA_HEADS * MLA_V), row),
                  pl.BlockSpec((tm, GLA_HEADS * GLA_DV), row),
                  pl.BlockSpec((tm, D_MODEL), lambda i: (i, P_GATEA // D_MODEL)),
                  pl.BlockSpec((tm, D_MODEL), lambda i: (i, P_GATEB // D_MODEL)),
                  pl.BlockSpec((1, 1, D_MODEL), bvec),
                  pl.BlockSpec((1, 1, D_MODEL), bvec),
                  pl.BlockSpec((1, 1, D_MODEL), bvec),
                  pl.BlockSpec((1, D_MODEL), const),
                  pl.BlockSpec((MLA_HEADS * MLA_V, D_MODEL), const, **single),
                  pl.BlockSpec((GLA_HEADS * GLA_DV, D_MODEL), const, **single),
                  pl.BlockSpec((D_MODEL, D_MODEL), const, **single),
                  pl.BlockSpec((D_MODEL, LANES), const, **single),
                  pl.BlockSpec((D_MODEL, LANES), const, **single),
                  pl.BlockSpec((1, LANES), const)],
        out_specs=[pl.BlockSpec((tm, D_MODEL), row),
                   pl.BlockSpec((tm * ROW_TILES, LANES), row),
                   pl.BlockSpec((tm, LANES), row),
                   pl.BlockSpec((tm, LANES), row)],
        out_shape=[jax.ShapeDtypeStruct((N_TOK, D_MODEL), F32),
                   jax.ShapeDtypeStruct((N_TOK * ROW_TILES, LANES), F32),
                   jax.ShapeDtypeStruct((N_TOK, LANES), jnp.int32),
                   jax.ShapeDtypeStruct((N_TOK, LANES), F32)],
        compiler_params=_cparams(("arbitrary",)),
        name="merge_router",
    )(x2d, o_mla, o_gla, proj, proj, gm, shf, scf, nfw, woa, wob, wout, wrh, wrl, br)


ROW_COPY_PRIORITY = 1
MOE_CHAINS = ((0, 5),)
MOE_QUOTA_ROWS = 4 * MOE_BLK
MOE_STEP_ROWS = MOE_QUOTA_ROWS // MOE_NCHUNK
MOE_QUOTAS = tuple(MOE_STEP_ROWS * w // MOE_ITEM_BLKS for _, w in MOE_CHAINS)
assert sum(w for _, w in MOE_CHAINS) == MOE_ITEM_BLKS and sum(MOE_QUOTAS) == MOE_STEP_ROWS


def _moe_kernel(it_e_ref, it_start_ref, it_rows_ref, src_ref, dst_ref,
                h2_hbm, w1_ref, b1_ref, w2_ref, b2_ref, sel_ref, y4_hbm,
                stage_ref, xb_ref, acc_ref, yout_ref, w1b_ref, w2b_ref, gsem, ssem):
    it = pl.program_id(0)
    c = pl.program_id(1)
    cnt = it_rows_ref[it]
    nblk = (cnt + MOE_BLK - 1) // MOE_BLK
    prev = jnp.maximum(it - 1, 0)
    rows_prev = jnp.where(it > 0, it_rows_ref[prev], 0)
    rows_next = it_rows_ref[it + 1]
    live = cnt > 0
    prev_live = rows_prev > 0
    base_prev = it_start_ref[prev]
    base_next = it_start_ref[it + 1]

    def token_rows(t):
        return pl.ds(pl.multiple_of(t * ROW_TILES, ROW_TILES), ROW_TILES)

    def gather_row(r, base, nrows, priority=ROW_COPY_PRIORITY):
        tok = jnp.where(r < nrows, src_ref[jnp.minimum(base + r, MOE_SLOTS - 1)], 0)
        pltpu.make_async_copy(h2_hbm.at[token_rows(tok), :], stage_ref.at[token_rows(r), :],
                              gsem).start(priority=priority)

    def scatter_row(r, priority=ROW_COPY_PRIORITY):
        d = dst_ref[jnp.minimum(base_prev + r, MOE_SLOTS - 1)]
        d = jnp.where(jnp.logical_and(r < rows_prev, d >= 0), d, N_ASSIGN + r)
        pltpu.make_async_copy(yout_ref.at[token_rows(r), :], y4_hbm.at[token_rows(d), :],
                              ssem).start(priority=priority)

    def issue_quota(ci, gather=True):
        r0 = c * MOE_STEP_ROWS + sum(MOE_QUOTAS[:ci])
        for u in range(MOE_QUOTAS[ci]):
            if gather:
                gather_row(r0 + u, base_next, rows_next)
            scatter_row(r0 + u)

    def tail(nrows):
        return MOE_QUOTA_ROWS, jnp.maximum(nrows, MOE_QUOTA_ROWS)

    def quota_rows(ref):
        return ref.at[pl.ds(0, MOE_QUOTA_ROWS * ROW_TILES), :]

    @pl.when(jnp.logical_and(it == 0, c == 0))
    def _():
        stage_ref[...] = jnp.zeros(stage_ref.shape, F32)
        yout_ref[...] = jnp.zeros(yout_ref.shape, F32)

        def first(r, carry):
            gather_row(r, it_start_ref[0], cnt)
            return carry
        lax.fori_loop(0, tail(cnt)[1], first, 0)

    @pl.when(jnp.logical_and(c == 0, jnp.logical_or(it == 0, prev_live)))
    def _():
        pltpu.make_async_copy(quota_rows(h2_hbm), quota_rows(stage_ref), gsem).wait()

        def drain(r, carry):
            pltpu.make_async_copy(h2_hbm.at[token_rows(0), :], stage_ref.at[token_rows(r), :], gsem).wait()
            return carry
        lax.fori_loop(*tail(cnt), drain, 0)

    @pl.when(jnp.logical_and(c == 0, live))
    def _():
        for s in range(MOE_ITEM_BLKS):
            @pl.when(s < nblk)
            def _():
                rows = slice(s * MOE_BLK, (s + 1) * MOE_BLK)
                for j in range(ROW_TILES):
                    piece = stage_ref[pl.ds(s * MOE_BLK * ROW_TILES + j, MOE_BLK, stride=ROW_TILES), :]
                    xb_ref[rows, j * LANES:(j + 1) * LANES] = piece.astype(BF16)
                acc_ref[rows, :] = jnp.broadcast_to(b2_ref[0], (MOE_BLK, D_MODEL))

    @pl.when(c == 0)
    def _():
        @pl.when(live)
        def _():
            def more(r, carry):
                gather_row(r, base_next, rows_next)
                return carry
            lax.fori_loop(*tail(rows_next), more, 0)

        def more(r, carry):
            scatter_row(r)
            return carry
        lax.fori_loop(*tail(rows_prev), more, 0)

    def compute(row0, nrows):
        rows = slice(row0, row0 + nrows)
        hm = jnp.dot(xb_ref[rows, :], w1b_ref[...], preferred_element_type=F32) + b1_ref[0]
        lin = pltpu.roll(hm, 2 * MOE_TH - 1, 1)
        glu = jnp.minimum(hm, SWIGLU_LIMIT)
        lin = jnp.clip(lin, -SWIGLU_LIMIT, SWIGLU_LIMIT)
        act = glu * jax.nn.sigmoid(SWIGLU_ALPHA * glu) * (lin + 1.0)
        lane = lax.broadcasted_iota(jnp.int32, act.shape, 1)
        act = jnp.where(lane % 2 == 0, act, 0.0).astype(BF16)
        act = jnp.dot(act, sel_ref[...], preferred_element_type=F32).astype(BF16)
        acc_ref[rows, :] += jnp.dot(act, w2b_ref[...], preferred_element_type=F32)

    @pl.when(live)
    def _():
        w1b_ref[...] = w1_ref[0].astype(BF16)
        w2b_ref[...] = w2_ref[0].astype(BF16)

    for ci, (first, width) in enumerate(MOE_CHAINS):
        in_chain = jnp.clip(nblk - first, 0, width)
        for nb in range(width, 0, -1):
            @pl.when(in_chain == nb)
            def _():
                compute(first * MOE_BLK, nb * MOE_BLK)
                issue_quota(ci)

        @pl.when(jnp.logical_and(in_chain == 0, live))
        def _():
            issue_quota(ci)

        @pl.when(jnp.logical_and(jnp.logical_not(live), prev_live))
        def _():
            issue_quota(ci, gather=False)

    @pl.when(jnp.logical_and(c == MOE_NCHUNK - 1, jnp.logical_or(live, prev_live)))
    def _():
        pltpu.make_async_copy(quota_rows(yout_ref), quota_rows(y4_hbm), ssem).wait()

        def drain(r, carry):
            pltpu.make_async_copy(yout_ref.at[token_rows(r), :], y4_hbm.at[token_rows(0), :], ssem).wait()
            return carry
        lax.fori_loop(*tail(rows_prev), drain, 0)

    @pl.when(jnp.logical_and(c == MOE_NCHUNK - 1, live))
    def _():
        for s in range(MOE_ITEM_BLKS):
            @pl.when(s < nblk)
            def _():
                rows = slice(s * MOE_BLK, (s + 1) * MOE_BLK)
                for j in range(ROW_TILES):
                    yout_ref[pl.ds(s * MOE_BLK * ROW_TILES + j, MOE_BLK, stride=ROW_TILES), :] = (
                        acc_ref[rows, j * LANES:(j + 1) * LANES])


def _moe(it_e, it_start, it_rows, slot_src, slot_dst, h2, w1, b1, w2, b2):
    def chunk_of(it, c, n_ref):
        return jnp.where(n_ref[it] > 0, c, MOE_NCHUNK - 1)

    def w1_map(it, c, e_ref, s_ref, n_ref, *_):
        return (e_ref[it], 0, chunk_of(it, c, n_ref))

    def w2_map(it, c, e_ref, s_ref, n_ref, *_):
        return (e_ref[it], chunk_of(it, c, n_ref), 0)

    def b2_map(it, c, e_ref, *_):
        return (e_ref[it], 0, 0)

    sel = (jnp.arange(2 * MOE_TH, dtype=jnp.int32)[:, None]
           == 2 * jnp.arange(MOE_TH, dtype=jnp.int32)[None, :]).astype(BF16)

    grid_spec = pltpu.PrefetchScalarGridSpec(
        num_scalar_prefetch=5,
        grid=(MOE_NITEMS + 1, MOE_NCHUNK),
        in_specs=[pl.BlockSpec(memory_space=pl.ANY),
                  pl.BlockSpec((1, D_MODEL, 2 * MOE_TH), w1_map),
                  pl.BlockSpec((1, 1, 2 * MOE_TH), w1_map),
                  pl.BlockSpec((1, MOE_TH, D_MODEL), w2_map),
                  pl.BlockSpec((1, 1, D_MODEL), b2_map),
                  pl.BlockSpec((2 * MOE_TH, MOE_TH), lambda it, c, *_: (0, 0))],
        out_specs=pl.BlockSpec(memory_space=pl.ANY),
        scratch_shapes=[pltpu.VMEM((MOE_ITEM_ROWS * ROW_TILES, LANES), F32),
                        pltpu.VMEM((MOE_ITEM_ROWS, D_MODEL), BF16),
                        pltpu.VMEM((MOE_ITEM_ROWS, D_MODEL), F32),
                        pltpu.VMEM((MOE_ITEM_ROWS * ROW_TILES, LANES), F32),
                        pltpu.VMEM((D_MODEL, 2 * MOE_TH), BF16),
                        pltpu.VMEM((MOE_TH, D_MODEL), BF16),
                        pltpu.SemaphoreType.DMA(()),
                        pltpu.SemaphoreType.DMA(())],
    )
    return pl.pallas_call(
        _moe_kernel,
        grid_spec=grid_spec,
        out_shape=jax.ShapeDtypeStruct(((N_ASSIGN + MOE_QUOTA_ROWS) * ROW_TILES, LANES), F32),
        compiler_params=_cparams(("arbitrary", "arbitrary")),
        name="moe_experts",
    )(it_e, it_start, it_rows, slot_src, slot_dst, h2, w1,
      b1.reshape(N_EXPERTS, 1, 2 * D_EXPERT), w2, b2.reshape(N_EXPERTS, 1, D_MODEL), sel)


def _route_tables(top_idx):
    e_flat = top_idx.reshape(N_ASSIGN)
    onehot = (e_flat[:, None] == jnp.arange(N_EXPERTS, dtype=jnp.int32)[None, :]).astype(jnp.int32)
    csum = jnp.cumsum(onehot, axis=0)
    rank = jnp.sum(csum * onehot, axis=-1) - 1
    counts = csum[-1]
    nblk = (counts + MOE_BLK - 1) // MOE_BLK
    blk_end = jnp.cumsum(nblk)
    blk_start = blk_end - nblk
    dest = blk_start[e_flat] * MOE_BLK + rank
    slot_a = jnp.full((MOE_SLOTS,), -1, jnp.int32).at[dest].set(jnp.arange(N_ASSIGN, dtype=jnp.int32))
    slot_src = jnp.where(slot_a >= 0, slot_a // TOP_K, 0)
    slot_dst = jnp.where(slot_a >= 0, (slot_a % TOP_K) * N_TOK + slot_a // TOP_K, -1)

    nit = (nblk + MOE_ITEM_BLKS - 1) // MOE_ITEM_BLKS
    it_end = jnp.cumsum(nit)
    it_begin = it_end - nit
    n_used = it_end[-1]
    ids = jnp.arange(MOE_NITEMS + 2, dtype=jnp.int32)
    valid = ids < n_used
    last = jnp.minimum(ids, n_used - 1)
    e_of = jnp.minimum(jnp.sum((it_end[None, :] <= last[:, None]).astype(jnp.int32), axis=1), N_EXPERTS - 1)
    local = ids - it_begin[e_of]
    it_start = jnp.where(valid, (blk_start[e_of] + local * MOE_ITEM_BLKS) * MOE_BLK, 0).astype(jnp.int32)
    it_rows = jnp.where(valid, jnp.clip(counts[e_of] - local * MOE_ITEM_ROWS, 0, MOE_ITEM_ROWS), 0)
    return e_of, it_start, it_rows.astype(jnp.int32), slot_src, slot_dst


FIN_TM = 512


def _final_kernel(x1_ref, ya_ref, yb_ref, yc_ref, yd_ref, tw_ref, gf_ref, sho_ref, sco_ref, nw_ref, o_ref):
    tw = tw_ref[...]
    pieces = []
    for j in range(ROW_TILES):
        acc = None
        for kk, y_ref in enumerate((ya_ref, yb_ref, yc_ref, yd_ref)):
            term = tw[:, kk:kk + 1] * y_ref[pl.ds(j, FIN_TM, stride=ROW_TILES), :]
            acc = term if acc is None else acc + term
        pieces.append(acc)
    moe = jnp.concatenate(pieces, axis=1)
    x2 = x1_ref[...] + gf_ref[0] * moe
    o_ref[...] = _rms(x2, nw_ref[...]) * (1.0 + sco_ref[0]) + sho_ref[0]


def _final(x1, y4, tw, gf, sho, sco, nw):
    tm = FIN_TM
    per_b = SEQ // tm
    row = lambda i: (i, 0)
    bvec = lambda i: (i // per_b, 0, 0)
    nb = N_TOK // tm
    kth = lambda kk: pl.BlockSpec((tm * ROW_TILES, LANES), lambda i: (kk * nb + i, 0))
    return pl.pallas_call(
        _final_kernel,
        grid=(nb,),
        in_specs=[pl.BlockSpec((tm, D_MODEL), row),
                  kth(0), kth(1), kth(2), kth(3),
                  pl.BlockSpec((tm, LANES), row),
                  pl.BlockSpec((1, 1, D_MODEL), bvec),
                  pl.BlockSpec((1, 1, D_MODEL), bvec),
                  pl.BlockSpec((1, 1, D_MODEL), bvec),
                  pl.BlockSpec((1, D_MODEL), lambda i: (0, 0))],
        out_specs=pl.BlockSpec((tm, D_MODEL), row),
        out_shape=jax.ShapeDtypeStruct((N_TOK, D_MODEL), F32),
        compiler_params=_cparams(("arbitrary",)),
        name="combine_final",
    )(x1, y4, y4, y4, y4, tw, gf, sho, sco, nw)


def _deinterleave(n):
    return np.concatenate([np.arange(0, n, 2), np.arange(1, n, 2)])


_IN_OFFS = np.cumsum([0, MLA_Q_RANK, MLA_KV_RANK + MLA_ROPE, GLA_HEADS * GLA_DK, GLA_HEADS * GLA_DK,
                      GLA_HEADS * GLA_DV, GLA_GATE_RANK, GLA_HEADS * GLA_DV, D_MODEL, D_MODEL])
_IN_MOVES = ((_IN_OFFS[0], MLA_Q_RANK, P_QLAT), (_IN_OFFS[1], MLA_KV_RANK, P_CKV),
             (_IN_OFFS[5], GLA_GATE_RANK, P_GLR), (_IN_OFFS[2], GLA_HEADS * GLA_DK, P_GQ),
             (_IN_OFFS[3], GLA_HEADS * GLA_DK, P_GK), (_IN_OFFS[4], GLA_HEADS * GLA_DV, P_GV),
             (_IN_OFFS[6], GLA_HEADS * GLA_DV, P_GOUT), (_IN_OFFS[7], D_MODEL, P_GATEA),
             (_IN_OFFS[8], D_MODEL, P_GATEB))
PACK_TM = 256


def _pack_kernel(w_ref, kpe_ref, o_ref):
    o_ref[:, P_KPE:P_KPE + LANES] = kpe_ref[...]
    o_ref[:, P_GLR:P_GLR + LANES] = jnp.zeros((PACK_TM, LANES), BF16)
    for src, width, dst in _IN_MOVES:
        o_ref[:, dst:dst + width] = w_ref[:, int(src):int(src) + width].astype(BF16)


def _pack_w_in(w_in):
    d_in = w_in.shape[1]
    kpe = w_in[:, _IN_OFFS[1] + MLA_KV_RANK:_IN_OFFS[2]][:, _deinterleave(MLA_ROPE)]
    kpe = jnp.concatenate([kpe, jnp.zeros((D_MODEL, LANES - MLA_ROPE), w_in.dtype)], axis=1).astype(BF16)
    return pl.pallas_call(
        _pack_kernel,
        grid=(D_MODEL // PACK_TM,),
        in_specs=[pl.BlockSpec((PACK_TM, d_in), lambda i: (i, 0)),
                  pl.BlockSpec((PACK_TM, LANES), lambda i: (i, 0))],
        out_specs=pl.BlockSpec((PACK_TM, P_TOTAL), lambda i: (i, 0)),
        out_shape=jax.ShapeDtypeStruct((D_MODEL, P_TOTAL), BF16),
        compiler_params=_cparams(("arbitrary",)),
        name="pack_w_in",
    )(w_in, kpe)


def _pack_w_q_b(w):
    w = w.reshape(MLA_Q_RANK, MLA_HEADS, MLA_NOPE + MLA_ROPE)
    nope = w[:, :, :MLA_NOPE]
    pe = w[:, :, MLA_NOPE:][:, :, _deinterleave(MLA_ROPE)]
    pad = jnp.zeros((MLA_Q_RANK, MLA_HEADS, MLA_HD - MLA_NOPE - MLA_ROPE), w.dtype)
    return jnp.concatenate([nope, pe, pad], axis=-1).reshape(MLA_Q_RANK, MLA_HEADS * MLA_HD).astype(BF16)


def _rope_consts():
    inv_freq = ROPE_THETA ** (-(jnp.arange(0, MLA_ROPE, 2, dtype=F32) / MLA_ROPE))
    half = MLA_ROPE // 2
    z = jnp.zeros((LANES - MLA_ROPE,), F32)
    rows = [jnp.concatenate([inv_freq, inv_freq, z]),
            jnp.concatenate([-jnp.ones((half,), F32), jnp.zeros((half,), F32), z]),
            jnp.concatenate([jnp.zeros((half,), F32), jnp.ones((half,), F32), z])]
    return jnp.concatenate([jnp.stack(rows), jnp.zeros((5, LANES), F32)], axis=0)


def kernel(x, c, positions, w_ada, b_ada, norm_mix_w, w_in, mla_q_norm_w, mla_w_q_b, mla_kv_norm_w,
           mla_w_kv_b, gla_w_gk_up, gla_b_gk_up, gla_norm_w, w_o_mla, w_o_gla, w_out, norm_ffn_w,
           w_router, b_router, w1, b1, w2, b2, w_ada_final, b_ada_final, norm_final_w):
    assert w_ada.shape[0] == 1, "single-layer stack"
    x2d = x.reshape(N_TOK, D_MODEL)
    c_t = c.T

    mod = _adaln(c_t, w_ada[0], b_ada[0])
    fmod = _adaln(c_t, w_ada_final, b_ada_final)
    sh_m, sc_m, g_m, sh_f, sc_f, g_f = [m.reshape(BATCH, 1, D_MODEL) for m in jnp.split(mod, N_MOD, axis=-1)]
    sh_o, sc_o = [m.reshape(BATCH, 1, D_MODEL) for m in jnp.split(fmod, 2, axis=-1)]

    proj = _inproj(x2d, norm_mix_w[0].reshape(1, D_MODEL), sh_m, sc_m, _pack_w_in(w_in[0]))

    q, k, v = _mla_prep(proj, positions.reshape(N_TOK, 1), _rope_consts(),
                        mla_q_norm_w[0].reshape(1, MLA_Q_RANK), mla_kv_norm_w[0].reshape(1, MLA_KV_RANK),
                        _pack_w_q_b(mla_w_q_b[0]), mla_w_kv_b[0].astype(BF16))
    o_mla = _attention(q, k, v)

    wgk = jnp.concatenate([gla_w_gk_up[0], jnp.zeros((LANES - GLA_GATE_RANK, GLA_HEADS * GLA_DK), F32)], axis=0)
    o_gla = _gla(proj, wgk, gla_b_gk_up[0].reshape(1, GLA_HEADS * GLA_DK), gla_norm_w[0].reshape(1, GLA_DV))

    wr = jnp.concatenate([w_router[0], jnp.zeros((D_MODEL, LANES - N_EXPERTS), F32)], axis=1)
    wr_hi = wr.astype(BF16)
    wr_lo = (wr - wr_hi.astype(F32)).astype(BF16)
    br = jnp.concatenate([b_router[0], jnp.zeros((LANES - N_EXPERTS,), F32)]).reshape(1, LANES)
    x1, h2, top_idx, top_w = _merge(
        x2d, o_mla, o_gla, proj, g_m, sh_f, sc_f, norm_ffn_w[0].reshape(1, D_MODEL),
        w_o_mla[0].astype(BF16), w_o_gla[0].astype(BF16), w_out[0].astype(BF16), wr_hi, wr_lo, br)

    tables = _route_tables(top_idx[:, :TOP_K])
    y4 = _moe(*tables, h2, w1[0], b1[0], w2[0], b2[0])

    out = _final(x1, y4, top_w, g_f, sh_o, sc_o, norm_final_w.reshape(1, D_MODEL))
    return out.reshape(BATCH, SEQ, D_MODEL)
```

```python
import functools

import jax
import jax.numpy as jnp
import numpy as np
from jax import lax
from jax.experimental import pallas as pl
from jax.experimental.pallas import tpu as pltpu

F32 = jnp.float32
BF16 = jnp.bfloat16

D_MODEL = 2048
BATCH = 2
SEQ = 4096
N_TOK = BATCH * SEQ

MLA_HEADS = 8
MLA_Q_RANK = 512
MLA_KV_RANK = 256
MLA_NOPE = 128
MLA_ROPE = 64
MLA_V = 128
ROPE_THETA = 10000.0
MLA_HD = 256

GLA_HEADS = 4
GLA_DK = 128
GLA_DV = 256
GLA_GATE_RANK = 16
GLA_GATE_NORM = 16.0
GLA_CHUNK = 64

N_EXPERTS = 32
TOP_K = 4
D_EXPERT = D_MODEL
SWIGLU_LIMIT = 7.0
SWIGLU_ALPHA = 1.702
EPS = 1e-6
N_MOD = 6

LANES = 128
ROW_TILES = D_MODEL // LANES
VMEM_LIMIT = 56 * 1024 * 1024

P_QLAT = 0
P_CKV = 512
P_KPE = 768
P_GLR = 896
P_GQ = 1024
P_GK = 1536
P_GV = 2048
P_GOUT = 3072
P_GATEA = 4096
P_GATEB = 6144
P_TOTAL = 8192

MOE_BLK = 256
MOE_ITEM_BLKS = 5
MOE_ITEM_ROWS = MOE_BLK * MOE_ITEM_BLKS
MOE_TH = 256
MOE_NCHUNK = D_EXPERT // MOE_TH
N_ASSIGN = N_TOK * TOP_K
MOE_NBLK_MAX = N_ASSIGN // MOE_BLK + N_EXPERTS
MOE_SLOTS = MOE_NBLK_MAX * MOE_BLK
MOE_NITEMS = N_ASSIGN // MOE_ITEM_ROWS + N_EXPERTS


def _cparams(sem, vmem=VMEM_LIMIT):
    return pltpu.CompilerParams(dimension_semantics=sem, vmem_limit_bytes=vmem)


def _rms(x, w):
    return x * lax.rsqrt(jnp.mean(x * x, axis=-1, keepdims=True) + EPS) * w


def _adaln_kernel(ct_ref, w_ref, b_ref, o_ref):
    ct = ct_ref[...]
    cond = ct * jax.nn.sigmoid(ct)
    w = w_ref[...]
    for b in range(BATCH):
        o_ref[b:b + 1, :] = jnp.sum(w * cond[:, b:b + 1], axis=0, keepdims=True) + b_ref[...]


def _adaln(c_t, w, b, tn=1024):
    d, n = w.shape
    return pl.pallas_call(
        _adaln_kernel,
        grid=(n // tn,),
        in_specs=[pl.BlockSpec((d, BATCH), lambda j: (0, 0)),
                  pl.BlockSpec((d, tn), lambda j: (0, j)),
                  pl.BlockSpec((1, tn), lambda j: (0, j))],
        out_specs=pl.BlockSpec((BATCH, tn), lambda j: (0, j)),
        out_shape=jax.ShapeDtypeStruct((BATCH, n), F32),
        compiler_params=_cparams(("arbitrary",)),
        name="adaln",
    )(c_t, w, b.reshape(1, n))


IN_TM = 1024
IN_TN = 2048


def _inproj_kernel(x_ref, nw_ref, sh_ref, sc_ref, w_ref, o_ref, h_ref):
    @pl.when(pl.program_id(1) == 0)
    def _():
        for r in range(0, IN_TM, 256):
            x = x_ref[r:r + 256, :]
            h = _rms(x, nw_ref[...]) * (1.0 + sc_ref[0]) + sh_ref[0]
            h_ref[r:r + 256, :] = h.astype(BF16)

    o_ref[...] = jnp.dot(h_ref[...], w_ref[...], preferred_element_type=F32).astype(BF16)


def _inproj(x2d, nw, sh, sc, w_packed):
    per_b = SEQ // IN_TM
    return pl.pallas_call(
        _inproj_kernel,
        grid=(N_TOK // IN_TM, P_TOTAL // IN_TN),
        in_specs=[pl.BlockSpec((IN_TM, D_MODEL), lambda i, j: (i, 0)),
                  pl.BlockSpec((1, D_MODEL), lambda i, j: (0, 0)),
                  pl.BlockSpec((1, 1, D_MODEL), lambda i, j: (i // per_b, 0, 0)),
                  pl.BlockSpec((1, 1, D_MODEL), lambda i, j: (i // per_b, 0, 0)),
                  pl.BlockSpec((D_MODEL, IN_TN), lambda i, j: (0, j))],
        out_specs=pl.BlockSpec((IN_TM, IN_TN), lambda i, j: (i, j)),
        out_shape=jax.ShapeDtypeStruct((N_TOK, P_TOTAL), BF16),
        scratch_shapes=[pltpu.VMEM((IN_TM, D_MODEL), BF16)],
        compiler_params=_cparams(("arbitrary", "arbitrary")),
        name="inproj",
    )(x2d, nw, sh, sc, w_packed)


PREP_TM = 512


def _mla_prep_kernel(ql_ref, ckv_ref, kpe_ref, pos_ref, rc_ref, qnw_ref, kvnw_ref, wq_ref, wkv_ref,
                     q_ref, k_ref, v_ref):
    scale = (MLA_NOPE + MLA_ROPE) ** -0.5
    qn = _rms(ql_ref[...].astype(F32), qnw_ref[...]).astype(BF16)
    q = jnp.dot(qn, wq_ref[...], preferred_element_type=F32)
    cn = _rms(ckv_ref[...].astype(F32), kvnw_ref[...]).astype(BF16)
    kv = jnp.dot(cn, wkv_ref[...], preferred_element_type=F32)

    ang = pos_ref[...].astype(F32) * rc_ref[0:1, :]
    cos_t = jnp.cos(ang)
    sin_t = jnp.sin(ang)
    sin_a = sin_t * rc_ref[1:2, :]
    sin_b = sin_t * rc_ref[2:3, :]

    def rope(t):
        return (t * cos_t + pltpu.roll(t, LANES - MLA_ROPE // 2, 1) * sin_a
                + pltpu.roll(t, MLA_ROPE // 2, 1) * sin_b)

    kpe = rope(kpe_ref[...].astype(F32)).astype(BF16)
    for h in range(MLA_HEADS):
        c0 = h * MLA_HD
        q_ref[0, h, :, 0:LANES] = (q[:, c0:c0 + LANES] * scale).astype(BF16)
        q_ref[0, h, :, LANES:MLA_HD] = (rope(q[:, c0 + LANES:c0 + MLA_HD]) * scale).astype(BF16)
        k_ref[0, h, :, 0:LANES] = kv[:, c0:c0 + LANES].astype(BF16)
        k_ref[0, h, :, LANES:MLA_HD] = kpe
        v_ref[0, h, :, :] = kv[:, c0 + LANES:c0 + MLA_HD].astype(BF16)


def _mla_prep(proj, pos_col, rope_consts, qnw, kvnw, wq, wkv):
    tm = PREP_TM
    per_b = SEQ // tm
    row = lambda i: (i, 0)
    const = lambda i: (0, 0)
    out_map = lambda i: (i // per_b, 0, i % per_b, 0)
    return pl.pallas_call(
        _mla_prep_kernel,
        grid=(N_TOK // tm,),
        in_specs=[pl.BlockSpec((tm, MLA_Q_RANK), lambda i: (i, P_QLAT // MLA_Q_RANK)),
                  pl.BlockSpec((tm, MLA_KV_RANK), lambda i: (i, P_CKV // MLA_KV_RANK)),
                  pl.BlockSpec((tm, LANES), lambda i: (i, P_KPE // LANES)),
                  pl.BlockSpec((tm, 1), row),
                  pl.BlockSpec((8, LANES), const),
                  pl.BlockSpec((1, MLA_Q_RANK), const),
                  pl.BlockSpec((1, MLA_KV_RANK), const),
                  pl.BlockSpec((MLA_Q_RANK, MLA_HEADS * MLA_HD), const),
                  pl.BlockSpec((MLA_KV_RANK, MLA_HEADS * MLA_HD), const)],
        out_specs=[pl.BlockSpec((1, MLA_HEADS, tm, MLA_HD), out_map),
                   pl.BlockSpec((1, MLA_HEADS, tm, MLA_HD), out_map),
                   pl.BlockSpec((1, MLA_HEADS, tm, MLA_V), out_map)],
        out_shape=[jax.ShapeDtypeStruct((BATCH, MLA_HEADS, SEQ, MLA_HD), BF16),
                   jax.ShapeDtypeStruct((BATCH, MLA_HEADS, SEQ, MLA_HD), BF16),
                   jax.ShapeDtypeStruct((BATCH, MLA_HEADS, SEQ, MLA_V), BF16)],
        compiler_params=_cparams(("arbitrary",)),
        name="mla_prep",
    )(proj, proj, proj, pos_col, rope_consts, qnw, kvnw, wq, wkv)


ATT_T = 1024


def _attn_kernel(q_ref, k_ref, v_ref, o_ref, m_ref, l_ref, acc_ref, s0_ref, s1_ref):
    qi = pl.program_id(2)
    q = q_ref[0, 0]
    m_ref[...] = jnp.full(m_ref.shape, -jnp.inf, F32)
    l_ref[...] = jnp.zeros(l_ref.shape, F32)
    acc_ref[...] = jnp.zeros(acc_ref.shape, F32)

    def scores(j):
        r0 = pl.multiple_of(j * ATT_T, ATT_T)
        k = k_ref[0, 0, pl.ds(r0, ATT_T), :]
        return lax.dot_general(k, q, (((1,), (1,)), ((), ())), preferred_element_type=F32)

    def step(j, src_ref, dst_ref):
        masked = dst_ref is None
        r0 = pl.multiple_of(j * ATT_T, ATT_T)
        v = v_ref[0, 0, pl.ds(r0, ATT_T), :]
        s = src_ref[...]
        if not masked:
            dst_ref[...] = scores(j + 1)
        if masked:
            key = lax.broadcasted_iota(jnp.int32, s.shape, 0)
            qry = lax.broadcasted_iota(jnp.int32, s.shape, 1)
            s = jnp.where(key <= qry, s, -jnp.inf)
        m_old = m_ref[...]
        m_new = jnp.maximum(m_old, jnp.max(s, axis=0, keepdims=True))
        p = jnp.exp(s - m_new)
        alpha = jnp.exp(m_old - m_new)
        l_ref[...] = alpha * l_ref[...] + jnp.sum(p, axis=0, keepdims=True)
        pv = lax.dot_general(v, p.astype(BF16), (((0,), (0,)), ((), ())), preferred_element_type=F32)
        acc_ref[...] = alpha * acc_ref[...] + pv
        m_ref[...] = m_new

    def body(jj, carry):
        step(2 * jj, s0_ref, s1_ref)
        step(2 * jj + 1, s1_ref, s0_ref)
        return carry

    s0_ref[...] = scores(0)
    lax.fori_loop(0, qi // 2, body, 0)

    @pl.when(qi % 2 == 1)
    def _():
        step(qi - 1, s0_ref, s1_ref)
        step(qi, s1_ref, None)

    @pl.when(qi % 2 == 0)
    def _():
        step(qi, s0_ref, None)

    o_ref[...] = jnp.transpose(acc_ref[...] / l_ref[...]).astype(BF16)


def _attention(q, k, v):
    t = ATT_T
    nq = SEQ // t
    return pl.pallas_call(
        _attn_kernel,
        grid=(BATCH, MLA_HEADS, nq),
        in_specs=[pl.BlockSpec((1, 1, t, MLA_HD), lambda b, h, i: (b, h, i, 0)),
                  pl.BlockSpec((1, 1, SEQ, MLA_HD), lambda b, h, i: (b, h, 0, 0)),
                  pl.BlockSpec((1, 1, SEQ, MLA_V), lambda b, h, i: (b, h, 0, 0))],
        out_specs=pl.BlockSpec((t, MLA_V), lambda b, h, i: (b * nq + i, h)),
        out_shape=jax.ShapeDtypeStruct((N_TOK, MLA_HEADS * MLA_V), BF16),
        scratch_shapes=[pltpu.VMEM((1, t), F32), pltpu.VMEM((1, t), F32), pltpu.VMEM((MLA_V, t), F32),
                        pltpu.VMEM((t, t), F32), pltpu.VMEM((t, t), F32)],
        compiler_params=_cparams(("arbitrary", "arbitrary", "arbitrary")),
        name="mla_attn",
    )(q, k, v)


GLA_TG = 512
GLA_DIAG = 8
GLA_HPS = 4


def _gla_kernel(q_ref, k_ref, v_ref, glr_ref, gout_ref, wgk_ref, bgk_ref, nw_ref, o_ref, st_ref, g_ref):
    C = GLA_CHUNK

    @pl.when(pl.program_id(2) == 0)
    def _():
        st_ref[...] = jnp.zeros(st_ref.shape, F32)

    z = jnp.dot(glr_ref[...].astype(F32), wgk_ref[...], preferred_element_type=F32,
                precision=lax.Precision.HIGHEST) + bgk_ref[...]
    g_ref[...] = jax.nn.log_sigmoid(z) / GLA_GATE_NORM

    row = lax.broadcasted_iota(jnp.int32, (C, GLA_DK), 0)
    ii = lax.broadcasted_iota(jnp.int32, (C, C), 0)
    jj = lax.broadcasted_iota(jnp.int32, (C, C), 1)
    nt = (((1,), (1,)), ((), ()))

    def chunk(ci, carry):
        for hh in range(GLA_HPS):
            head_chunk(ci, hh)
        return carry

    def head_chunk(ci, hh):
        r0 = pl.multiple_of(ci * C, C)
        kcols = slice(hh * GLA_DK, (hh + 1) * GLA_DK)
        vcols = slice(hh * GLA_DV, (hh + 1) * GLA_DV)
        b = g_ref[pl.ds(r0, C), kcols]
        s = 1
        while s < C:
            b = b + jnp.where(row >= s, pltpu.roll(b, s, 0), 0.0)
            s *= 2
        q = q_ref[pl.ds(r0, C), kcols].astype(F32) * (GLA_DK ** -0.5)
        k = k_ref[pl.ds(r0, C), kcols].astype(F32)
        v = v_ref[pl.ds(r0, C), vcols]
        st = st_ref[hh]

        o = lax.dot_general((q * jnp.exp(b)).astype(BF16), st.astype(BF16), nt,
                            preferred_element_type=F32)

        a = jnp.zeros((C, C), F32)
        lvl = C // 2
        while lvl >= GLA_DIAG:
            pieces = [jnp.broadcast_to(b[t + lvl:t + lvl + 1, :], (2 * lvl, GLA_DK))
                      for t in range(0, C, 2 * lvl)]
            m = pieces[0] if len(pieces) == 1 else jnp.concatenate(pieces, axis=0)
            odd = (row // lvl) % 2 == 1
            ql = jnp.where(odd, q * jnp.exp(jnp.minimum(b - m, 0.0)), 0.0).astype(BF16)
            kl = jnp.where(odd, 0.0, k * jnp.exp(jnp.minimum(m - b, 0.0))).astype(BF16)
            al = lax.dot_general(ql, kl, nt, preferred_element_type=F32)
            a = a + jnp.where(ii // (2 * lvl) == jj // (2 * lvl), al, 0.0)
            lvl //= 2
        for d in range(GLA_DIAG):
            if d == 0:
                t = q * k
            else:
                t = q * pltpu.roll(k, d, 0) * jnp.exp(jnp.minimum(b - pltpu.roll(b, d, 0), 0.0))
            col = jnp.sum(t, axis=-1, keepdims=True)
            a = a + jnp.where((ii - jj == d) & (ii % GLA_DIAG >= d), col, 0.0)
        o = o + jnp.dot(a.astype(BF16), v, preferred_element_type=F32)

        b_last = b[C - 1:C, :]
        kdec = (k * jnp.exp(b_last - b)).astype(BF16)
        st_ref[hh] = st * jnp.exp(b_last) + lax.dot_general(
            v, kdec, (((0,), (0,)), ((), ())), preferred_element_type=F32)

        gate = gout_ref[pl.ds(r0, C), vcols].astype(F32)
        y = _rms(o, nw_ref[...]) * (gate * jax.nn.sigmoid(gate))
        o_ref[pl.ds(r0, C), vcols] = y.astype(BF16)

    lax.fori_loop(0, GLA_TG // C, chunk, 0, unroll=4)


def _gla(proj, wgk, bgk, nw):
    tg = GLA_TG
    per_b = SEQ // tg
    kw = GLA_HPS * GLA_DK
    vw = GLA_HPS * GLA_DV
    rb = lambda b, h, t: b * per_b + t
    return pl.pallas_call(
        _gla_kernel,
        grid=(BATCH, GLA_HEADS // GLA_HPS, per_b),
        in_specs=[pl.BlockSpec((tg, kw), lambda b, h, t: (rb(b, h, t), P_GQ // kw + h)),
                  pl.BlockSpec((tg, kw), lambda b, h, t: (rb(b, h, t), P_GK // kw + h)),
                  pl.BlockSpec((tg, vw), lambda b, h, t: (rb(b, h, t), P_GV // vw + h)),
                  pl.BlockSpec((tg, LANES), lambda b, h, t: (rb(b, h, t), P_GLR // LANES)),
                  pl.BlockSpec((tg, vw), lambda b, h, t: (rb(b, h, t), P_GOUT // vw + h)),
                  pl.BlockSpec((LANES, kw), lambda b, h, t: (0, h)),
                  pl.BlockSpec((1, kw), lambda b, h, t: (0, h)),
                  pl.BlockSpec((1, GLA_DV), lambda b, h, t: (0, 0))],
        out_specs=pl.BlockSpec((tg, vw), lambda b, h, t: (rb(b, h, t), h)),
        out_shape=jax.ShapeDtypeStruct((N_TOK, GLA_HEADS * GLA_DV), BF16),
        scratch_shapes=[pltpu.VMEM((GLA_HPS, GLA_DV, GLA_DK), F32), pltpu.VMEM((tg, kw), F32)],
        compiler_params=_cparams(("arbitrary", "arbitrary", "arbitrary")),
        name="gla",
    )(proj, proj, proj, proj, proj, wgk, bgk, nw)


MRG_TM = 256


def _merge_kernel(x_ref, oa_ref, ob_ref, ga_ref, gb_ref, gm_ref, shf_ref, scf_ref, nfw_ref,
                  woa_ref, wob_ref, wout_ref, wr_ref, br_ref,
                  x1_ref, h2_ref, idx_ref, tw_ref):
    ya = jnp.dot(oa_ref[...], woa_ref[...], preferred_element_type=F32)
    yb = jnp.dot(ob_ref[...], wob_ref[...], preferred_element_type=F32)
    merged = (jax.nn.sigmoid(ga_ref[...].astype(F32)) * ya
              + jax.nn.sigmoid(gb_ref[...].astype(F32)) * yb).astype(BF16)
    x1 = x_ref[...] + gm_ref[0] * jnp.dot(merged, wout_ref[...], preferred_element_type=F32)
    x1_ref[...] = x1
    h2 = _rms(x1, nfw_ref[...]) * (1.0 + scf_ref[0]) + shf_ref[0]
    for j in range(ROW_TILES):
        h2_ref[pl.ds(j, MRG_TM, stride=ROW_TILES), :] = h2[:, j * LANES:(j + 1) * LANES]

    hi = h2.astype(BF16)
    lo = (h2 - hi.astype(F32)).astype(BF16)
    both = jnp.dot(hi, wr_ref[...], preferred_element_type=F32)
    logits = (both[:, :LANES] + both[:, LANES:]
              + jnp.dot(lo, wr_ref[:, :LANES], preferred_element_type=F32)) + br_ref[...]
    lane = lax.broadcasted_iota(jnp.int32, logits.shape, 1)
    vals = jnp.where(lane < N_EXPERTS, logits, -jnp.inf)
    idx_out = jnp.zeros(logits.shape, jnp.int32)
    w_out = jnp.zeros(logits.shape, F32)
    top = None
    denom = None
    for kk in range(TOP_K):
        m = jnp.max(vals, axis=-1, keepdims=True)
        sel = jnp.min(jnp.where(vals == m, lane, LANES), axis=-1, keepdims=True)
        if kk == 0:
            top = m
        e = jnp.exp(m - top)
        denom = e if kk == 0 else denom + e
        idx_out = jnp.where(lane == kk, sel, idx_out)
        w_out = jnp.where(lane == kk, e, w_out)
        vals = jnp.where(lane == sel, -jnp.inf, vals)
    idx_ref[...] = idx_out
    tw_ref[...] = w_out / denom


def _merge(x2d, o_mla, o_gla, proj, gm, shf, scf, nfw, woa, wob, wout, wr, br):
    tm = MRG_TM
    per_b = SEQ // tm
    row = lambda i: (i, 0)
    const = lambda i: (0, 0)
    bvec = lambda i: (i // per_b, 0, 0)
    single = dict(pipeline_mode=pl.Buffered(1))
    return pl.pallas_call(
        _merge_kernel,
        grid=(N_TOK // tm,),
        in_specs=[pl.BlockSpec((tm, D_MODEL), row),
                  pl.BlockSpec((tm, MLA_HEADS * MLA_V), row),
                  pl.BlockSpec((tm, GLA_HEADS * GLA_DV), row),
                  pl.BlockSpec((tm, D_MODEL), lambda i: (i, P_GATEA // D_MODEL)),
                  pl.BlockSpec((tm, D_MODEL), lambda i: (i, P_GATEB // D_MODEL)),
                  pl.BlockSpec((1, 1, D_MODEL), bvec),
                  pl.BlockSpec((1, 1, D_MODEL), bvec),
                  pl.BlockSpec((1, 1, D_MODEL), bvec),
                  pl.BlockSpec((1, D_MODEL), const),
                  pl.BlockSpec((MLA_HEADS * MLA_V, D_MODEL), const, **single),
                  pl.BlockSpec((GLA_HEADS * GLA_DV, D_MODEL), const, **single),
                  pl.BlockSpec((D_MODEL, D_MODEL), const, **single),
                  pl.BlockSpec((D_MODEL, 2 * LANES), const, **single),
                  pl.BlockSpec((1, LANES), const)],
        out_specs=[pl.BlockSpec((tm, D_MODEL), row),
                   pl.BlockSpec((tm * ROW_TILES, LANES), row),
                   pl.BlockSpec((tm, LANES), row),
                   pl.BlockSpec((tm, LANES), row)],
        out_shape=[jax.ShapeDtypeStruct((N_TOK, D_MODEL), F32),
                   jax.ShapeDtypeStruct((N_TOK * ROW_TILES, LANES), F32),
                   jax.ShapeDtypeStruct((N_TOK, LANES), jnp.int32),
                   jax.ShapeDtypeStruct((N_TOK, LANES), F32)],
        compiler_params=_cparams(("arbitrary",)),
        name="merge_router",
    )(x2d, o_mla, o_gla, proj, proj, gm, shf, scf, nfw, woa, wob, wout, wr, br)


ROW_COPY_PRIORITY = 1
MOE_CHAINS = ((0, 5),)
MOE_QUOTA_ROWS = 4 * MOE_BLK
MOE_STEP_ROWS = MOE_QUOTA_ROWS // MOE_NCHUNK
MOE_QUOTAS = tuple(MOE_STEP_ROWS * w // MOE_ITEM_BLKS for _, w in MOE_CHAINS)
assert sum(w for _, w in MOE_CHAINS) == MOE_ITEM_BLKS and sum(MOE_QUOTAS) == MOE_STEP_ROWS


def _moe_kernel(it_e_ref, it_start_ref, it_rows_ref, src_ref, dst_ref,
                h2_hbm, w1_ref, b1_ref, w2_ref, b2_ref, sel_ref, y4_hbm,
                stage_ref, xb_ref, acc_ref, yout_ref, w1b_ref, w2b_ref, gsem, ssem):
    it = pl.program_id(0)
    c = pl.program_id(1)
    cnt = it_rows_ref[it]
    nblk = (cnt + MOE_BLK - 1) // MOE_BLK
    prev = jnp.maximum(it - 1, 0)
    rows_prev = jnp.where(it > 0, it_rows_ref[prev], 0)
    rows_next = it_rows_ref[it + 1]
    live = cnt > 0
    prev_live = rows_prev > 0
    base_prev = it_start_ref[prev]
    base_next = it_start_ref[it + 1]

    def token_rows(t):
        return pl.ds(pl.multiple_of(t * ROW_TILES, ROW_TILES), ROW_TILES)

    def gather_row(r, base, nrows, priority=ROW_COPY_PRIORITY):
        tok = jnp.where(r < nrows, src_ref[jnp.minimum(base + r, MOE_SLOTS - 1)], 0)
        pltpu.make_async_copy(h2_hbm.at[token_rows(tok), :], stage_ref.at[token_rows(r), :],
                              gsem).start(priority=priority)

    def scatter_row(r, priority=ROW_COPY_PRIORITY):
        d = dst_ref[jnp.minimum(base_prev + r, MOE_SLOTS - 1)]
        d = jnp.where(jnp.logical_and(r < rows_prev, d >= 0), d, N_ASSIGN + r)
        pltpu.make_async_copy(yout_ref.at[token_rows(r), :], y4_hbm.at[token_rows(d), :],
                              ssem).start(priority=priority)

    def issue_quota(ci, gather=True):
        r0 = c * MOE_STEP_ROWS + sum(MOE_QUOTAS[:ci])
        for u in range(MOE_QUOTAS[ci]):
            if gather:
                gather_row(r0 + u, base_next, rows_next)
            scatter_row(r0 + u)

    def tail(nrows):
        return MOE_QUOTA_ROWS, jnp.maximum(nrows, MOE_QUOTA_ROWS)

    def quota_rows(ref):
        return ref.at[pl.ds(0, MOE_QUOTA_ROWS * ROW_TILES), :]

    @pl.when(jnp.logical_and(it == 0, c == 0))
    def _():
        stage_ref[...] = jnp.zeros(stage_ref.shape, F32)
        yout_ref[...] = jnp.zeros(yout_ref.shape, F32)

        def first(r, carry):
            gather_row(r, it_start_ref[0], cnt)
            return carry
        lax.fori_loop(0, tail(cnt)[1], first, 0)

    @pl.when(jnp.logical_and(c == 0, jnp.logical_or(it == 0, prev_live)))
    def _():
        pltpu.make_async_copy(quota_rows(h2_hbm), quota_rows(stage_ref), gsem).wait()

        def drain(r, carry):
            pltpu.make_async_copy(h2_hbm.at[token_rows(0), :], stage_ref.at[token_rows(r), :], gsem).wait()
            return carry
        lax.fori_loop(*tail(cnt), drain, 0)

    @pl.when(jnp.logical_and(c == 0, live))
    def _():
        for s in range(MOE_ITEM_BLKS):
            @pl.when(s < nblk)
            def _():
                rows = slice(s * MOE_BLK, (s + 1) * MOE_BLK)
                for j in range(ROW_TILES):
                    piece = stage_ref[pl.ds(s * MOE_BLK * ROW_TILES + j, MOE_BLK, stride=ROW_TILES), :]
                    xb_ref[rows, j * LANES:(j + 1) * LANES] = piece.astype(BF16)
                acc_ref[rows, :] = jnp.broadcast_to(b2_ref[0], (MOE_BLK, D_MODEL))

    @pl.when(c == 0)
    def _():
        @pl.when(live)
        def _():
            def more(r, carry):
                gather_row(r, base_next, rows_next)
                return carry
            lax.fori_loop(*tail(rows_next), more, 0)

        def more(r, carry):
            scatter_row(r)
            return carry
        lax.fori_loop(*tail(rows_prev), more, 0)

    def compute(row0, nrows):
        rows = slice(row0, row0 + nrows)
        hm = jnp.dot(xb_ref[rows, :], w1b_ref[...], preferred_element_type=F32) + b1_ref[0]
        lin = pltpu.roll(hm, 2 * MOE_TH - 1, 1)
        glu = jnp.minimum(hm, SWIGLU_LIMIT)
        lin = jnp.clip(lin, -SWIGLU_LIMIT, SWIGLU_LIMIT)
        act = glu * jax.nn.sigmoid(SWIGLU_ALPHA * glu) * (lin + 1.0)
        lane = lax.broadcasted_iota(jnp.int32, act.shape, 1)
        act = jnp.where(lane % 2 == 0, act, 0.0).astype(BF16)
        act = jnp.dot(act, sel_ref[...], preferred_element_type=F32).astype(BF16)
        acc_ref[rows, :] += jnp.dot(act, w2b_ref[...], preferred_element_type=F32)

    @pl.when(live)
    def _():
        w1b_ref[...] = w1_ref[0].astype(BF16)
        w2b_ref[...] = w2_ref[0].astype(BF16)

    for ci, (first, width) in enumerate(MOE_CHAINS):
        in_chain = jnp.clip(nblk - first, 0, width)
        for nb in range(width, 0, -1):
            @pl.when(in_chain == nb)
            def _():
                compute(first * MOE_BLK, nb * MOE_BLK)
                issue_quota(ci)

        @pl.when(jnp.logical_and(in_chain == 0, live))
        def _():
            issue_quota(ci)

        @pl.when(jnp.logical_and(jnp.logical_not(live), prev_live))
        def _():
            issue_quota(ci, gather=False)

    @pl.when(jnp.logical_and(c == MOE_NCHUNK - 1, jnp.logical_or(live, prev_live)))
    def _():
        pltpu.make_async_copy(quota_rows(yout_ref), quota_rows(y4_hbm), ssem).wait()

        def drain(r, carry):
            pltpu.make_async_copy(yout_ref.at[token_rows(r), :], y4_hbm.at[token_rows(0), :], ssem).wait()
            return carry
        lax.fori_loop(*tail(rows_prev), drain, 0)

    @pl.when(jnp.logical_and(c == MOE_NCHUNK - 1, live))
    def _():
        for s in range(MOE_ITEM_BLKS):
            @pl.when(s < nblk)
            def _():
                rows = slice(s * MOE_BLK, (s + 1) * MOE_BLK)
                for j in range(ROW_TILES):
                    yout_ref[pl.ds(s * MOE_BLK * ROW_TILES + j, MOE_BLK, stride=ROW_TILES), :] = (
                        acc_ref[rows, j * LANES:(j + 1) * LANES])


def _moe(it_e, it_start, it_rows, slot_src, slot_dst, h2, w1, b1, w2, b2):
    def chunk_of(it, c, n_ref):
        return jnp.where(n_ref[it] > 0, c, MOE_NCHUNK - 1)

    def w1_map(it, c, e_ref, s_ref, n_ref, *_):
        return (e_ref[it], 0, chunk_of(it, c, n_ref))

    def w2_map(it, c, e_ref, s_ref, n_ref, *_):
        return (e_ref[it], chunk_of(it, c, n_ref), 0)

    def b2_map(it, c, e_ref, *_):
        return (e_ref[it], 0, 0)

    sel = (jnp.arange(2 * MOE_TH, dtype=jnp.int32)[:, None]
           == 2 * jnp.arange(MOE_TH, dtype=jnp.int32)[None, :]).astype(BF16)

    grid_spec = pltpu.PrefetchScalarGridSpec(
        num_scalar_prefetch=5,
        grid=(MOE_NITEMS + 1, MOE_NCHUNK),
        in_specs=[pl.BlockSpec(memory_space=pl.ANY),
                  pl.BlockSpec((1, D_MODEL, 2 * MOE_TH), w1_map),
                  pl.BlockSpec((1, 1, 2 * MOE_TH), w1_map),
                  pl.BlockSpec((1, MOE_TH, D_MODEL), w2_map),
                  pl.BlockSpec((1, 1, D_MODEL), b2_map),
                  pl.BlockSpec((2 * MOE_TH, MOE_TH), lambda it, c, *_: (0, 0))],
        out_specs=pl.BlockSpec(memory_space=pl.ANY),
        scratch_shapes=[pltpu.VMEM((MOE_ITEM_ROWS * ROW_TILES, LANES), F32),
                        pltpu.VMEM((MOE_ITEM_ROWS, D_MODEL), BF16),
                        pltpu.VMEM((MOE_ITEM_ROWS, D_MODEL), F32),
                        pltpu.VMEM((MOE_ITEM_ROWS * ROW_TILES, LANES), F32),
                        pltpu.VMEM((D_MODEL, 2 * MOE_TH), BF16),
                        pltpu.VMEM((MOE_TH, D_MODEL), BF16),
                        pltpu.SemaphoreType.DMA(()),
                        pltpu.SemaphoreType.DMA(())],
    )
    return pl.pallas_call(
        _moe_kernel,
        grid_spec=grid_spec,
        out_shape=jax.ShapeDtypeStruct(((N_ASSIGN + MOE_QUOTA_ROWS) * ROW_TILES, LANES), F32),
        compiler_params=_cparams(("arbitrary", "arbitrary")),
        name="moe_experts",
    )(it_e, it_start, it_rows, slot_src, slot_dst, h2, w1,
      b1.reshape(N_EXPERTS, 1, 2 * D_EXPERT), w2, b2.reshape(N_EXPERTS, 1, D_MODEL), sel)


def _route_tables(top_idx):
    e_flat = top_idx.reshape(N_ASSIGN)
    onehot = (e_flat[:, None] == jnp.arange(N_EXPERTS, dtype=jnp.int32)[None, :]).astype(jnp.int32)
    csum = jnp.cumsum(onehot, axis=0)
    rank = jnp.sum(csum * onehot, axis=-1) - 1
    counts = csum[-1]
    nblk = (counts + MOE_BLK - 1) // MOE_BLK
    blk_end = jnp.cumsum(nblk)
    blk_start = blk_end - nblk
    dest = blk_start[e_flat] * MOE_BLK + rank
    slot_a = jnp.full((MOE_SLOTS,), -1, jnp.int32).at[dest].set(jnp.arange(N_ASSIGN, dtype=jnp.int32))
    slot_src = jnp.where(slot_a >= 0, slot_a // TOP_K, 0)
    slot_dst = jnp.where(slot_a >= 0, (slot_a % TOP_K) * N_TOK + slot_a // TOP_K, -1)

    nit = (nblk + MOE_ITEM_BLKS - 1) // MOE_ITEM_BLKS
    it_end = jnp.cumsum(nit)
    it_begin = it_end - nit
    n_used = it_end[-1]
    ids = jnp.arange(MOE_NITEMS + 2, dtype=jnp.int32)
    valid = ids < n_used
    last = jnp.minimum(ids, n_used - 1)
    e_of = jnp.minimum(jnp.sum((it_end[None, :] <= last[:, None]).astype(jnp.int32), axis=1), N_EXPERTS - 1)
    local = ids - it_begin[e_of]
    it_start = jnp.where(valid, (blk_start[e_of] + local * MOE_ITEM_BLKS) * MOE_BLK, 0).astype(jnp.int32)
    it_rows = jnp.where(valid, jnp.clip(counts[e_of] - local * MOE_ITEM_ROWS, 0, MOE_ITEM_ROWS), 0)
    return e_of, it_start, it_rows.astype(jnp.int32), slot_src, slot_dst


FIN_TM = 512


def _final_kernel(x1_ref, ya_ref, yb_ref, yc_ref, yd_ref, tw_ref, gf_ref, sho_ref, sco_ref, nw_ref, o_ref):
    tw = tw_ref[...]
    pieces = []
    for j in range(ROW_TILES):
        acc = None
        for kk, y_ref in enumerate((ya_ref, yb_ref, yc_ref, yd_ref)):
            term = tw[:, kk:kk + 1] * y_ref[pl.ds(j, FIN_TM, stride=ROW_TILES), :]
            acc = term if acc is None else acc + term
        pieces.append(acc)
    moe = jnp.concatenate(pieces, axis=1)
    x2 = x1_ref[...] + gf_ref[0] * moe
    o_ref[...] = _rms(x2, nw_ref[...]) * (1.0 + sco_ref[0]) + sho_ref[0]


def _final(x1, y4, tw, gf, sho, sco, nw):
    tm = FIN_TM
    per_b = SEQ // tm
    row = lambda i: (i, 0)
    bvec = lambda i: (i // per_b, 0, 0)
    nb = N_TOK // tm
    kth = lambda kk: pl.BlockSpec((tm * ROW_TILES, LANES), lambda i: (kk * nb + i, 0))
    return pl.pallas_call(
        _final_kernel,
        grid=(nb,),
        in_specs=[pl.BlockSpec((tm, D_MODEL), row),
                  kth(0), kth(1), kth(2), kth(3),
                  pl.BlockSpec((tm, LANES), row),
                  pl.BlockSpec((1, 1, D_MODEL), bvec),
                  pl.BlockSpec((1, 1, D_MODEL), bvec),
                  pl.BlockSpec((1, 1, D_MODEL), bvec),
                  pl.BlockSpec((1, D_MODEL), lambda i: (0, 0))],
        out_specs=pl.BlockSpec((tm, D_MODEL), row),
        out_shape=jax.ShapeDtypeStruct((N_TOK, D_MODEL), F32),
        compiler_params=_cparams(("arbitrary",)),
        name="combine_final",
    )(x1, y4, y4, y4, y4, tw, gf, sho, sco, nw)


def _deinterleave(n):
    return np.concatenate([np.arange(0, n, 2), np.arange(1, n, 2)])


_IN_OFFS = np.cumsum([0, MLA_Q_RANK, MLA_KV_RANK + MLA_ROPE, GLA_HEADS * GLA_DK, GLA_HEADS * GLA_DK,
                      GLA_HEADS * GLA_DV, GLA_GATE_RANK, GLA_HEADS * GLA_DV, D_MODEL, D_MODEL])
_IN_MOVES = ((_IN_OFFS[0], MLA_Q_RANK, P_QLAT), (_IN_OFFS[1], MLA_KV_RANK, P_CKV),
             (_IN_OFFS[5], GLA_GATE_RANK, P_GLR), (_IN_OFFS[2], GLA_HEADS * GLA_DK, P_GQ),
             (_IN_OFFS[3], GLA_HEADS * GLA_DK, P_GK), (_IN_OFFS[4], GLA_HEADS * GLA_DV, P_GV),
             (_IN_OFFS[6], GLA_HEADS * GLA_DV, P_GOUT), (_IN_OFFS[7], D_MODEL, P_GATEA),
             (_IN_OFFS[8], D_MODEL, P_GATEB))
PACK_TM = 256


def _pack_kernel(w_ref, kpe_ref, o_ref):
    o_ref[:, P_KPE:P_KPE + LANES] = kpe_ref[...]
    o_ref[:, P_GLR:P_GLR + LANES] = jnp.zeros((PACK_TM, LANES), BF16)
    for src, width, dst in _IN_MOVES:
        o_ref[:, dst:dst + width] = w_ref[:, int(src):int(src) + width].astype(BF16)


def _pack_w_in(w_in):
    d_in = w_in.shape[1]
    kpe = w_in[:, _IN_OFFS[1] + MLA_KV_RANK:_IN_OFFS[2]][:, _deinterleave(MLA_ROPE)]
    kpe = jnp.concatenate([kpe, jnp.zeros((D_MODEL, LANES - MLA_ROPE), w_in.dtype)], axis=1).astype(BF16)
    return pl.pallas_call(
        _pack_kernel,
        grid=(D_MODEL // PACK_TM,),
        in_specs=[pl.BlockSpec((PACK_TM, d_in), lambda i: (i, 0)),
                  pl.BlockSpec((PACK_TM, LANES), lambda i: (i, 0))],
        out_specs=pl.BlockSpec((PACK_TM, P_TOTAL), lambda i: (i, 0)),
        out_shape=jax.ShapeDtypeStruct((D_MODEL, P_TOTAL), BF16),
        compiler_params=_cparams(("arbitrary",)),
        name="pack_w_in",
    )(w_in, kpe)


def _pack_w_q_b(w):
    w = w.reshape(MLA_Q_RANK, MLA_HEADS, MLA_NOPE + MLA_ROPE)
    nope = w[:, :, :MLA_NOPE]
    pe = w[:, :, MLA_NOPE:][:, :, _deinterleave(MLA_ROPE)]
    pad = jnp.zeros((MLA_Q_RANK, MLA_HEADS, MLA_HD - MLA_NOPE - MLA_ROPE), w.dtype)
    return jnp.concatenate([nope, pe, pad], axis=-1).reshape(MLA_Q_RANK, MLA_HEADS * MLA_HD).astype(BF16)


def _rope_consts():
    inv_freq = ROPE_THETA ** (-(jnp.arange(0, MLA_ROPE, 2, dtype=F32) / MLA_ROPE))
    half = MLA_ROPE // 2
    z = jnp.zeros((LANES - MLA_ROPE,), F32)
    rows = [jnp.concatenate([inv_freq, inv_freq, z]),
            jnp.concatenate([-jnp.ones((half,), F32), jnp.zeros((half,), F32), z]),
            jnp.concatenate([jnp.zeros((half,), F32), jnp.ones((half,), F32), z])]
    return jnp.concatenate([jnp.stack(rows), jnp.zeros((5, LANES), F32)], axis=0)


def kernel(x, c, positions, w_ada, b_ada, norm_mix_w, w_in, mla_q_norm_w, mla_w_q_b, mla_kv_norm_w,
           mla_w_kv_b, gla_w_gk_up, gla_b_gk_up, gla_norm_w, w_o_mla, w_o_gla, w_out, norm_ffn_w,
           w_router, b_router, w1, b1, w2, b2, w_ada_final, b_ada_final, norm_final_w):
    assert w_ada.shape[0] == 1, "single-layer stack"
    x2d = x.reshape(N_TOK, D_MODEL)
    c_t = c.T

    mod = _adaln(c_t, w_ada[0], b_ada[0])
    fmod = _adaln(c_t, w_ada_final, b_ada_final)
    sh_m, sc_m, g_m, sh_f, sc_f, g_f = [m.reshape(BATCH, 1, D_MODEL) for m in jnp.split(mod, N_MOD, axis=-1)]
    sh_o, sc_o = [m.reshape(BATCH, 1, D_MODEL) for m in jnp.split(fmod, 2, axis=-1)]

    proj = _inproj(x2d, norm_mix_w[0].reshape(1, D_MODEL), sh_m, sc_m, _pack_w_in(w_in[0]))

    q, k, v = _mla_prep(proj, positions.reshape(N_TOK, 1), _rope_consts(),
                        mla_q_norm_w[0].reshape(1, MLA_Q_RANK), mla_kv_norm_w[0].reshape(1, MLA_KV_RANK),
                        _pack_w_q_b(mla_w_q_b[0]), mla_w_kv_b[0].astype(BF16))
    o_mla = _attention(q, k, v)

    wgk = jnp.concatenate([gla_w_gk_up[0], jnp.zeros((LANES - GLA_GATE_RANK, GLA_HEADS * GLA_DK), F32)], axis=0)
    o_gla = _gla(proj, wgk, gla_b_gk_up[0].reshape(1, GLA_HEADS * GLA_DK), gla_norm_w[0].reshape(1, GLA_DV))

    wr = jnp.concatenate([w_router[0], jnp.zeros((D_MODEL, LANES - N_EXPERTS), F32)], axis=1)
    wr_hi = wr.astype(BF16)
    wr_lo = (wr - wr_hi.astype(F32)).astype(BF16)
    br = jnp.concatenate([b_router[0], jnp.zeros((LANES - N_EXPERTS,), F32)]).reshape(1, LANES)
    x1, h2, top_idx, top_w = _merge(
        x2d, o_mla, o_gla, proj, g_m, sh_f, sc_f, norm_ffn_w[0].reshape(1, D_MODEL),
        w_o_mla[0].astype(BF16), w_o_gla[0].astype(BF16), w_out[0].astype(BF16),
        jnp.concatenate([wr_hi, wr_lo], axis=1), br)

    tables = _route_tables(top_idx[:, :TOP_K])
    y4 = _moe(*tables, h2, w1[0], b1[0], w2[0], b2[0])

    out = _final(x1, y4, top_w, g_f, sh_o, sc_o, norm_final_w.reshape(1, D_MODEL))
    return out.reshape(BATCH, SEQ, D_MODEL)
```

```python
import functools

import jax
import jax.numpy as jnp
import numpy as np
from jax import lax
from jax.experimental import pallas as pl
from jax.experimental.pallas import tpu as pltpu

F32 = jnp.float32
BF16 = jnp.bfloat16

D_MODEL = 2048
BATCH = 2
SEQ = 4096
N_TOK = BATCH * SEQ

MLA_HEADS = 8
MLA_Q_RANK = 512
MLA_KV_RANK = 256
MLA_NOPE = 128
MLA_ROPE = 64
MLA_V = 128
ROPE_THETA = 10000.0
MLA_HD = 256

GLA_HEADS = 4
GLA_DK = 128
GLA_DV = 256
GLA_GATE_RANK = 16
GLA_GATE_NORM = 16.0
GLA_CHUNK = 64

N_EXPERTS = 32
TOP_K = 4
D_EXPERT = D_MODEL
SWIGLU_LIMIT = 7.0
SWIGLU_ALPHA = 1.702
EPS = 1e-6
N_MOD = 6

LANES = 128
ROW_TILES = D_MODEL // LANES
VMEM_LIMIT = 56 * 1024 * 1024

P_QLAT = 0
P_CKV = 512
P_KPE = 768
P_GLR = 896
P_GQ = 1024
P_GK = 1536
P_GV = 2048
P_GOUT = 3072
P_GATEA = 4096
P_GATEB = 6144
P_TOTAL = 8192

MOE_BLK = 256
MOE_ITEM_BLKS = 5
MOE_ITEM_ROWS = MOE_BLK * MOE_ITEM_BLKS
MOE_TH = 256
MOE_NCHUNK = D_EXPERT // MOE_TH
N_ASSIGN = N_TOK * TOP_K
MOE_NBLK_MAX = N_ASSIGN // MOE_BLK + N_EXPERTS
MOE_SLOTS = MOE_NBLK_MAX * MOE_BLK
MOE_NITEMS = N_ASSIGN // MOE_ITEM_ROWS + N_EXPERTS


def _cparams(sem, vmem=VMEM_LIMIT):
    return pltpu.CompilerParams(dimension_semantics=sem, vmem_limit_bytes=vmem)


def _from_token_major(t):
    rows = t.shape[0] // ROW_TILES
    return pltpu.einshape("mjl->jml", t.reshape(rows, ROW_TILES, LANES))


def _rms(x, w):
    return x * lax.rsqrt(jnp.mean(x * x, axis=-1, keepdims=True) + EPS) * w


def _adaln_kernel(ct_ref, w_ref, b_ref, o_ref):
    ct = ct_ref[...]
    cond = ct * jax.nn.sigmoid(ct)
    w = w_ref[...]
    for b in range(BATCH):
        o_ref[b:b + 1, :] = jnp.sum(w * cond[:, b:b + 1], axis=0, keepdims=True) + b_ref[...]


def _adaln(c_t, w, b, tn=1024):
    d, n = w.shape
    return pl.pallas_call(
        _adaln_kernel,
        grid=(n // tn,),
        in_specs=[pl.BlockSpec((d, BATCH), lambda j: (0, 0)),
                  pl.BlockSpec((d, tn), lambda j: (0, j)),
                  pl.BlockSpec((1, tn), lambda j: (0, j))],
        out_specs=pl.BlockSpec((BATCH, tn), lambda j: (0, j)),
        out_shape=jax.ShapeDtypeStruct((BATCH, n), F32),
        compiler_params=_cparams(("arbitrary",)),
        name="adaln",
    )(c_t, w, b.reshape(1, n))


IN_TM = 1024
IN_TN = 2048


def _inproj_kernel(x_ref, nw_ref, sh_ref, sc_ref, w_ref, o_ref, h_ref):
    @pl.when(pl.program_id(1) == 0)
    def _():
        for r in range(0, IN_TM, 256):
            x = x_ref[r:r + 256, :]
            h = _rms(x, nw_ref[...]) * (1.0 + sc_ref[0]) + sh_ref[0]
            h_ref[r:r + 256, :] = h.astype(BF16)

    o_ref[...] = jnp.dot(h_ref[...], w_ref[...], preferred_element_type=F32).astype(BF16)


def _inproj(x2d, nw, sh, sc, w_packed):
    per_b = SEQ // IN_TM
    return pl.pallas_call(
        _inproj_kernel,
        grid=(N_TOK // IN_TM, P_TOTAL // IN_TN),
        in_specs=[pl.BlockSpec((IN_TM, D_MODEL), lambda i, j: (i, 0)),
                  pl.BlockSpec((1, D_MODEL), lambda i, j: (0, 0)),
                  pl.BlockSpec((1, 1, D_MODEL), lambda i, j: (i // per_b, 0, 0)),
                  pl.BlockSpec((1, 1, D_MODEL), lambda i, j: (i // per_b, 0, 0)),
                  pl.BlockSpec((D_MODEL, IN_TN), lambda i, j: (0, j))],
        out_specs=pl.BlockSpec((IN_TM, IN_TN), lambda i, j: (i, j)),
        out_shape=jax.ShapeDtypeStruct((N_TOK, P_TOTAL), BF16),
        scratch_shapes=[pltpu.VMEM((IN_TM, D_MODEL), BF16)],
        compiler_params=_cparams(("arbitrary", "arbitrary")),
        name="inproj",
    )(x2d, nw, sh, sc, w_packed)


PREP_TM = 512


def _mla_prep_kernel(ql_ref, ckv_ref, kpe_ref, pos_ref, rc_ref, qnw_ref, kvnw_ref, wq_ref, wkv_ref,
                     q_ref, k_ref, v_ref):
    scale = (MLA_NOPE + MLA_ROPE) ** -0.5
    qn = _rms(ql_ref[...].astype(F32), qnw_ref[...]).astype(BF16)
    q = jnp.dot(qn, wq_ref[...], preferred_element_type=F32)
    cn = _rms(ckv_ref[...].astype(F32), kvnw_ref[...]).astype(BF16)
    kv = jnp.dot(cn, wkv_ref[...], preferred_element_type=F32)

    ang = pos_ref[...].astype(F32) * rc_ref[0:1, :]
    cos_t = jnp.cos(ang)
    sin_t = jnp.sin(ang)
    sin_a = sin_t * rc_ref[1:2, :]
    sin_b = sin_t * rc_ref[2:3, :]

    def rope(t):
        return (t * cos_t + pltpu.roll(t, LANES - MLA_ROPE // 2, 1) * sin_a
                + pltpu.roll(t, MLA_ROPE // 2, 1) * sin_b)

    kpe = rope(kpe_ref[...].astype(F32)).astype(BF16)
    for h in range(MLA_HEADS):
        c0 = h * MLA_HD
        q_ref[0, h, :, 0:LANES] = (q[:, c0:c0 + LANES] * scale).astype(BF16)
        q_ref[0, h, :, LANES:MLA_HD] = (rope(q[:, c0 + LANES:c0 + MLA_HD]) * scale).astype(BF16)
        k_ref[0, h, :, 0:LANES] = kv[:, c0:c0 + LANES].astype(BF16)
        k_ref[0, h, :, LANES:MLA_HD] = kpe
        v_ref[0, h, :, :] = kv[:, c0 + LANES:c0 + MLA_HD].astype(BF16)


def _mla_prep(proj, pos_col, rope_consts, qnw, kvnw, wq, wkv):
    tm = PREP_TM
    per_b = SEQ // tm
    row = lambda i: (i, 0)
    const = lambda i: (0, 0)
    out_map = lambda i: (i // per_b, 0, i % per_b, 0)
    return pl.pallas_call(
        _mla_prep_kernel,
        grid=(N_TOK // tm,),
        in_specs=[pl.BlockSpec((tm, MLA_Q_RANK), lambda i: (i, P_QLAT // MLA_Q_RANK)),
                  pl.BlockSpec((tm, MLA_KV_RANK), lambda i: (i, P_CKV // MLA_KV_RANK)),
                  pl.BlockSpec((tm, LANES), lambda i: (i, P_KPE // LANES)),
                  pl.BlockSpec((tm, 1), row),
                  pl.BlockSpec((8, LANES), const),
                  pl.BlockSpec((1, MLA_Q_RANK), const),
                  pl.BlockSpec((1, MLA_KV_RANK), const),
                  pl.BlockSpec((MLA_Q_RANK, MLA_HEADS * MLA_HD), const),
                  pl.BlockSpec((MLA_KV_RANK, MLA_HEADS * MLA_HD), const)],
        out_specs=[pl.BlockSpec((1, MLA_HEADS, tm, MLA_HD), out_map),
                   pl.BlockSpec((1, MLA_HEADS, tm, MLA_HD), out_map),
                   pl.BlockSpec((1, MLA_HEADS, tm, MLA_V), out_map)],
        out_shape=[jax.ShapeDtypeStruct((BATCH, MLA_HEADS, SEQ, MLA_HD), BF16),
                   jax.ShapeDtypeStruct((BATCH, MLA_HEADS, SEQ, MLA_HD), BF16),
                   jax.ShapeDtypeStruct((BATCH, MLA_HEADS, SEQ, MLA_V), BF16)],
        compiler_params=_cparams(("arbitrary",)),
        name="mla_prep",
    )(proj, proj, proj, pos_col, rope_consts, qnw, kvnw, wq, wkv)


ATT_T = 1024


def _attn_kernel(q_ref, k_ref, v_ref, o_ref, m_ref, l_ref, acc_ref, s0_ref, s1_ref):
    qi = pl.program_id(2)
    q = q_ref[0, 0]
    m_ref[...] = jnp.full(m_ref.shape, -jnp.inf, F32)
    l_ref[...] = jnp.zeros(l_ref.shape, F32)
    acc_ref[...] = jnp.zeros(acc_ref.shape, F32)

    def scores(j):
        r0 = pl.multiple_of(j * ATT_T, ATT_T)
        k = k_ref[0, 0, pl.ds(r0, ATT_T), :]
        return lax.dot_general(k, q, (((1,), (1,)), ((), ())), preferred_element_type=F32)

    def step(j, src_ref, dst_ref):
        masked = dst_ref is None
        r0 = pl.multiple_of(j * ATT_T, ATT_T)
        v = v_ref[0, 0, pl.ds(r0, ATT_T), :]
        s = src_ref[...]
        if not masked:
            dst_ref[...] = scores(j + 1)
        if masked:
            key = lax.broadcasted_iota(jnp.int32, s.shape, 0)
            qry = lax.broadcasted_iota(jnp.int32, s.shape, 1)
            s = jnp.where(key <= qry, s, -jnp.inf)
        m_old = m_ref[...]
        m_new = jnp.maximum(m_old, jnp.max(s, axis=0, keepdims=True))
        p = jnp.exp(s - m_new)
        alpha = jnp.exp(m_old - m_new)
        l_ref[...] = alpha * l_ref[...] + jnp.sum(p, axis=0, keepdims=True)
        pv = lax.dot_general(v, p.astype(BF16), (((0,), (0,)), ((), ())), preferred_element_type=F32)
        acc_ref[...] = alpha * acc_ref[...] + pv
        m_ref[...] = m_new

    def body(jj, carry):
        step(2 * jj, s0_ref, s1_ref)
        step(2 * jj + 1, s1_ref, s0_ref)
        return carry

    s0_ref[...] = scores(0)
    lax.fori_loop(0, qi // 2, body, 0)

    @pl.when(qi % 2 == 1)
    def _():
        step(qi - 1, s0_ref, s1_ref)
        step(qi, s1_ref, None)

    @pl.when(qi % 2 == 0)
    def _():
        step(qi, s0_ref, None)

    o_ref[...] = jnp.transpose(acc_ref[...] / l_ref[...]).astype(BF16)


def _attention(q, k, v):
    t = ATT_T
    nq = SEQ // t
    return pl.pallas_call(
        _attn_kernel,
        grid=(BATCH, MLA_HEADS, nq),
        in_specs=[pl.BlockSpec((1, 1, t, MLA_HD), lambda b, h, i: (b, h, i, 0)),
                  pl.BlockSpec((1, 1, SEQ, MLA_HD), lambda b, h, i: (b, h, 0, 0)),
                  pl.BlockSpec((1, 1, SEQ, MLA_V), lambda b, h, i: (b, h, 0, 0))],
        out_specs=pl.BlockSpec((t, MLA_V), lambda b, h, i: (b * nq + i, h)),
        out_shape=jax.ShapeDtypeStruct((N_TOK, MLA_HEADS * MLA_V), BF16),
        scratch_shapes=[pltpu.VMEM((1, t), F32), pltpu.VMEM((1, t), F32), pltpu.VMEM((MLA_V, t), F32),
                        pltpu.VMEM((t, t), F32), pltpu.VMEM((t, t), F32)],
        compiler_params=_cparams(("arbitrary", "arbitrary", "arbitrary")),
        name="mla_attn",
    )(q, k, v)


GLA_TG = 512
GLA_DIAG = 8
GLA_HPS = 4


def _gla_kernel(q_ref, k_ref, v_ref, glr_ref, gout_ref, wgk_ref, bgk_ref, nw_ref, o_ref, st_ref, g_ref):
    C = GLA_CHUNK

    @pl.when(pl.program_id(2) == 0)
    def _():
        st_ref[...] = jnp.zeros(st_ref.shape, F32)

    z = jnp.dot(glr_ref[...].astype(F32), wgk_ref[...], preferred_element_type=F32,
                precision=lax.Precision.HIGHEST) + bgk_ref[...]
    g_ref[...] = jax.nn.log_sigmoid(z) / GLA_GATE_NORM

    row = lax.broadcasted_iota(jnp.int32, (C, GLA_DK), 0)
    ii = lax.broadcasted_iota(jnp.int32, (C, C), 0)
    jj = lax.broadcasted_iota(jnp.int32, (C, C), 1)
    nt = (((1,), (1,)), ((), ()))

    def chunk(ci, carry):
        for hh in range(GLA_HPS):
            head_chunk(ci, hh)
        return carry

    def head_chunk(ci, hh):
        r0 = pl.multiple_of(ci * C, C)
        kcols = slice(hh * GLA_DK, (hh + 1) * GLA_DK)
        vcols = slice(hh * GLA_DV, (hh + 1) * GLA_DV)
        b = g_ref[pl.ds(r0, C), kcols]
        s = 1
        while s < C:
            b = b + jnp.where(row >= s, pltpu.roll(b, s, 0), 0.0)
            s *= 2
        q = q_ref[pl.ds(r0, C), kcols].astype(F32) * (GLA_DK ** -0.5)
        k = k_ref[pl.ds(r0, C), kcols].astype(F32)
        v = v_ref[pl.ds(r0, C), vcols]
        st = st_ref[hh]

        o = lax.dot_general((q * jnp.exp(b)).astype(BF16), st.astype(BF16), nt,
                            preferred_element_type=F32)

        a = jnp.zeros((C, C), F32)
        lvl = C // 2
        while lvl >= GLA_DIAG:
            pieces = [jnp.broadcast_to(b[t + lvl:t + lvl + 1, :], (2 * lvl, GLA_DK))
                      for t in range(0, C, 2 * lvl)]
            m = pieces[0] if len(pieces) == 1 else jnp.concatenate(pieces, axis=0)
            odd = (row // lvl) % 2 == 1
            ql = jnp.where(odd, q * jnp.exp(jnp.minimum(b - m, 0.0)), 0.0).astype(BF16)
            kl = jnp.where(odd, 0.0, k * jnp.exp(jnp.minimum(m - b, 0.0))).astype(BF16)
            al = lax.dot_general(ql, kl, nt, preferred_element_type=F32)
            a = a + jnp.where(ii // (2 * lvl) == jj // (2 * lvl), al, 0.0)
            lvl //= 2
        for d in range(GLA_DIAG):
            if d == 0:
                t = q * k
            else:
                t = q * pltpu.roll(k, d, 0) * jnp.exp(jnp.minimum(b - pltpu.roll(b, d, 0), 0.0))
            col = jnp.sum(t, axis=-1, keepdims=True)
            a = a + jnp.where((ii - jj == d) & (ii % GLA_DIAG >= d), col, 0.0)
        o = o + jnp.dot(a.astype(BF16), v, preferred_element_type=F32)

        b_last = b[C - 1:C, :]
        kdec = (k * jnp.exp(b_last - b)).astype(BF16)
        st_ref[hh] = st * jnp.exp(b_last) + lax.dot_general(
            v, kdec, (((0,), (0,)), ((), ())), preferred_element_type=F32)

        gate = gout_ref[pl.ds(r0, C), vcols].astype(F32)
        y = _rms(o, nw_ref[...]) * (gate * jax.nn.sigmoid(gate))
        o_ref[pl.ds(r0, C), vcols] = y.astype(BF16)

    lax.fori_loop(0, GLA_TG // C, chunk, 0, unroll=4)


def _gla(proj, wgk, bgk, nw):
    tg = GLA_TG
    per_b = SEQ // tg
    kw = GLA_HPS * GLA_DK
    vw = GLA_HPS * GLA_DV
    rb = lambda b, h, t: b * per_b + t
    return pl.pallas_call(
        _gla_kernel,
        grid=(BATCH, GLA_HEADS // GLA_HPS, per_b),
        in_specs=[pl.BlockSpec((tg, kw), lambda b, h, t: (rb(b, h, t), P_GQ // kw + h)),
                  pl.BlockSpec((tg, kw), lambda b, h, t: (rb(b, h, t), P_GK // kw + h)),
                  pl.BlockSpec((tg, vw), lambda b, h, t: (rb(b, h, t), P_GV // vw + h)),
                  pl.BlockSpec((tg, LANES), lambda b, h, t: (rb(b, h, t), P_GLR // LANES)),
                  pl.BlockSpec((tg, vw), lambda b, h, t: (rb(b, h, t), P_GOUT // vw + h)),
                  pl.BlockSpec((LANES, kw), lambda b, h, t: (0, h)),
                  pl.BlockSpec((1, kw), lambda b, h, t: (0, h)),
                  pl.BlockSpec((1, GLA_DV), lambda b, h, t: (0, 0))],
        out_specs=pl.BlockSpec((tg, vw), lambda b, h, t: (rb(b, h, t), h)),
        out_shape=jax.ShapeDtypeStruct((N_TOK, GLA_HEADS * GLA_DV), BF16),
        scratch_shapes=[pltpu.VMEM((GLA_HPS, GLA_DV, GLA_DK), F32), pltpu.VMEM((tg, kw), F32)],
        compiler_params=_cparams(("arbitrary", "arbitrary", "arbitrary")),
        name="gla",
    )(proj, proj, proj, proj, proj, wgk, bgk, nw)


MRG_TM = 256


def _merge_kernel(x_ref, oa_ref, ob_ref, ga_ref, gb_ref, gm_ref, shf_ref, scf_ref, nfw_ref,
                  woa_ref, wob_ref, wout_ref, wr_ref, br_ref,
                  x1_ref, h2_ref, idx_ref, tw_ref):
    ya = jnp.dot(oa_ref[...], woa_ref[...], preferred_element_type=F32)
    yb = jnp.dot(ob_ref[...], wob_ref[...], preferred_element_type=F32)
    merged = (jax.nn.sigmoid(ga_ref[...].astype(F32)) * ya
              + jax.nn.sigmoid(gb_ref[...].astype(F32)) * yb).astype(BF16)
    x1 = x_ref[...] + gm_ref[0] * jnp.dot(merged, wout_ref[...], preferred_element_type=F32)
    x1_ref[...] = x1
    h2 = _rms(x1, nfw_ref[...]) * (1.0 + scf_ref[0]) + shf_ref[0]
    for j in range(ROW_TILES):
        h2_ref[pl.ds(j, MRG_TM, stride=ROW_TILES), :] = h2[:, j * LANES:(j + 1) * LANES]

    hi = h2.astype(BF16)
    lo = (h2 - hi.astype(F32)).astype(BF16)
    both = jnp.dot(hi, wr_ref[...], preferred_element_type=F32)
    logits = (both[:, :LANES] + both[:, LANES:]
              + jnp.dot(lo, wr_ref[:, :LANES], preferred_element_type=F32)) + br_ref[...]
    lane = lax.broadcasted_iota(jnp.int32, logits.shape, 1)
    vals = jnp.where(lane < N_EXPERTS, logits, -jnp.inf)
    idx_out = jnp.zeros(logits.shape, jnp.int32)
    w_out = jnp.zeros(logits.shape, F32)
    top = None
    denom = None
    for kk in range(TOP_K):
        m = jnp.max(vals, axis=-1, keepdims=True)
        sel = jnp.min(jnp.where(vals == m, lane, LANES), axis=-1, keepdims=True)
        if kk == 0:
            top = m
        e = jnp.exp(m - top)
        denom = e if kk == 0 else denom + e
        idx_out = jnp.where(lane == kk, sel, idx_out)
        w_out = jnp.where(lane == kk, e, w_out)
        vals = jnp.where(lane == sel, -jnp.inf, vals)
    idx_ref[...] = idx_out
    tw_ref[...] = w_out / denom


def _merge(x2d, o_mla, o_gla, proj, gm, shf, scf, nfw, woa, wob, wout, wr, br):
    tm = MRG_TM
    per_b = SEQ // tm
    row = lambda i: (i, 0)
    const = lambda i: (0, 0)
    bvec = lambda i: (i // per_b, 0, 0)
    single = dict(pipeline_mode=pl.Buffered(1))
    return pl.pallas_call(
        _merge_kernel,
        grid=(N_TOK // tm,),
        in_specs=[pl.BlockSpec((tm, D_MODEL), row),
                  pl.BlockSpec((tm, MLA_HEADS * MLA_V), row),
                  pl.BlockSpec((tm, GLA_HEADS * GLA_DV), row),
                  pl.BlockSpec((tm, D_MODEL), lambda i: (i, P_GATEA // D_MODEL)),
                  pl.BlockSpec((tm, D_MODEL), lambda i: (i, P_GATEB // D_MODEL)),
                  pl.BlockSpec((1, 1, D_MODEL), bvec),
                  pl.BlockSpec((1, 1, D_MODEL), bvec),
                  pl.BlockSpec((1, 1, D_MODEL), bvec),
                  pl.BlockSpec((1, D_MODEL), const),
                  pl.BlockSpec((MLA_HEADS * MLA_V, D_MODEL), const, **single),
                  pl.BlockSpec((GLA_HEADS * GLA_DV, D_MODEL), const, **single),
                  pl.BlockSpec((D_MODEL, D_MODEL), const, **single),
                  pl.BlockSpec((D_MODEL, 2 * LANES), const, **single),
                  pl.BlockSpec((1, LANES), const)],
        out_specs=[pl.BlockSpec((tm, D_MODEL), row),
                   pl.BlockSpec((tm * ROW_TILES, LANES), row),
                   pl.BlockSpec((tm, LANES), row),
                   pl.BlockSpec((tm, LANES), row)],
        out_shape=[jax.ShapeDtypeStruct((N_TOK, D_MODEL), F32),
                   jax.ShapeDtypeStruct((N_TOK * ROW_TILES, LANES), F32),
                   jax.ShapeDtypeStruct((N_TOK, LANES), jnp.int32),
                   jax.ShapeDtypeStruct((N_TOK, LANES), F32)],
        compiler_params=_cparams(("arbitrary",)),
        name="merge_router",
    )(x2d, o_mla, o_gla, proj, proj, gm, shf, scf, nfw, woa, wob, wout, wr, br)


ROW_COPY_PRIORITY = 1
MOE_CHAINS = ((0, 5),)
MOE_QUOTA_ROWS = 4 * MOE_BLK
MOE_STEP_ROWS = MOE_QUOTA_ROWS // MOE_NCHUNK
MOE_QUOTAS = tuple(MOE_STEP_ROWS * w // MOE_ITEM_BLKS for _, w in MOE_CHAINS)
assert sum(w for _, w in MOE_CHAINS) == MOE_ITEM_BLKS and sum(MOE_QUOTAS) == MOE_STEP_ROWS


def _moe_kernel(it_e_ref, it_start_ref, it_rows_ref, src_ref, dst_ref,
                h2_hbm, w1_ref, b1_ref, w2_ref, b2_ref, sel_ref, y4_hbm,
                stage_ref, xb_ref, acc_ref, yout_ref, w1b_ref, w2b_ref, gsem, ssem):
    it = pl.program_id(0)
    c = pl.program_id(1)
    cnt = it_rows_ref[it]
    nblk = (cnt + MOE_BLK - 1) // MOE_BLK
    prev = jnp.maximum(it - 1, 0)
    rows_prev = jnp.where(it > 0, it_rows_ref[prev], 0)
    rows_next = it_rows_ref[it + 1]
    live = cnt > 0
    prev_live = rows_prev > 0
    base_prev = it_start_ref[prev]
    base_next = it_start_ref[it + 1]

    def token_rows(t):
        return pl.ds(pl.multiple_of(t * ROW_TILES, ROW_TILES), ROW_TILES)

    def gather_row(r, base, nrows, priority=ROW_COPY_PRIORITY):
        tok = jnp.where(r < nrows, src_ref[jnp.minimum(base + r, MOE_SLOTS - 1)], 0)
        pltpu.make_async_copy(h2_hbm.at[token_rows(tok), :], stage_ref.at[token_rows(r), :],
                              gsem).start(priority=priority)

    def scatter_row(r, priority=ROW_COPY_PRIORITY):
        d = dst_ref[jnp.minimum(base_prev + r, MOE_SLOTS - 1)]
        d = jnp.where(jnp.logical_and(r < rows_prev, d >= 0), d, N_ASSIGN + r)
        pltpu.make_async_copy(yout_ref.at[token_rows(r), :], y4_hbm.at[token_rows(d), :],
                              ssem).start(priority=priority)

    def issue_quota(ci, gather=True):
        r0 = c * MOE_STEP_ROWS + sum(MOE_QUOTAS[:ci])
        for u in range(MOE_QUOTAS[ci]):
            if gather:
                gather_row(r0 + u, base_next, rows_next)
            scatter_row(r0 + u)

    def tail(nrows):
        return MOE_QUOTA_ROWS, jnp.maximum(nrows, MOE_QUOTA_ROWS)

    def quota_rows(ref):
        return ref.at[pl.ds(0, MOE_QUOTA_ROWS * ROW_TILES), :]

    @pl.when(jnp.logical_and(it == 0, c == 0))
    def _():
        stage_ref[...] = jnp.zeros(stage_ref.shape, F32)
        yout_ref[...] = jnp.zeros(yout_ref.shape, F32)

        def first(r, carry):
            gather_row(r, it_start_ref[0], cnt)
            return carry
        lax.fori_loop(0, tail(cnt)[1], first, 0)

    @pl.when(jnp.logical_and(c == 0, jnp.logical_or(it == 0, prev_live)))
    def _():
        pltpu.make_async_copy(quota_rows(h2_hbm), quota_rows(stage_ref), gsem).wait()

        def drain(r, carry):
            pltpu.make_async_copy(h2_hbm.at[token_rows(0), :], stage_ref.at[token_rows(r), :], gsem).wait()
            return carry
        lax.fori_loop(*tail(cnt), drain, 0)

    @pl.when(jnp.logical_and(c == 0, live))
    def _():
        for s in range(MOE_ITEM_BLKS):
            @pl.when(s < nblk)
            def _():
                rows = slice(s * MOE_BLK, (s + 1) * MOE_BLK)
                blk = _from_token_major(stage_ref[s * MOE_BLK * ROW_TILES:(s + 1) * MOE_BLK * ROW_TILES, :])
                for j in range(ROW_TILES):
                    xb_ref[rows, j * LANES:(j + 1) * LANES] = blk[j].astype(BF16)
                acc_ref[rows, :] = jnp.broadcast_to(b2_ref[0], (MOE_BLK, D_MODEL))

    @pl.when(c == 0)
    def _():
        @pl.when(live)
        def _():
            def more(r, carry):
                gather_row(r, base_next, rows_next)
                return carry
            lax.fori_loop(*tail(rows_next), more, 0)

        def more(r, carry):
            scatter_row(r)
            return carry
        lax.fori_loop(*tail(rows_prev), more, 0)

    def compute(row0, nrows):
        rows = slice(row0, row0 + nrows)
        hm = jnp.dot(xb_ref[rows, :], w1b_ref[...], preferred_element_type=F32) + b1_ref[0]
        lin = pltpu.roll(hm, 2 * MOE_TH - 1, 1)
        glu = jnp.minimum(hm, SWIGLU_LIMIT)
        lin = jnp.clip(lin, -SWIGLU_LIMIT, SWIGLU_LIMIT)
        act = glu * jax.nn.sigmoid(SWIGLU_ALPHA * glu) * (lin + 1.0)
        lane = lax.broadcasted_iota(jnp.int32, act.shape, 1)
        act = jnp.where(lane % 2 == 0, act, 0.0).astype(BF16)
        act = jnp.dot(act, sel_ref[...], preferred_element_type=F32).astype(BF16)
        acc_ref[rows, :] += jnp.dot(act, w2b_ref[...], preferred_element_type=F32)

    @pl.when(live)
    def _():
        w1b_ref[...] = w1_ref[0].astype(BF16)
        w2b_ref[...] = w2_ref[0].astype(BF16)

    for ci, (first, width) in enumerate(MOE_CHAINS):
        in_chain = jnp.clip(nblk - first, 0, width)
        for nb in range(width, 0, -1):
            @pl.when(in_chain == nb)
            def _():
                compute(first * MOE_BLK, nb * MOE_BLK)
                issue_quota(ci)

        @pl.when(jnp.logical_and(in_chain == 0, live))
        def _():
            issue_quota(ci)

        @pl.when(jnp.logical_and(jnp.logical_not(live), prev_live))
        def _():
            issue_quota(ci, gather=False)

    @pl.when(jnp.logical_and(c == MOE_NCHUNK - 1, jnp.logical_or(live, prev_live)))
    def _():
        pltpu.make_async_copy(quota_rows(yout_ref), quota_rows(y4_hbm), ssem).wait()

        def drain(r, carry):
            pltpu.make_async_copy(yout_ref.at[token_rows(r), :], y4_hbm.at[token_rows(0), :], ssem).wait()
            return carry
        lax.fori_loop(*tail(rows_prev), drain, 0)

    @pl.when(jnp.logical_and(c == MOE_NCHUNK - 1, live))
    def _():
        for s in range(MOE_ITEM_BLKS):
            @pl.when(s < nblk)
            def _():
                rows = slice(s * MOE_BLK, (s + 1) * MOE_BLK)
                for j in range(ROW_TILES):
                    yout_ref[pl.ds(s * MOE_BLK * ROW_TILES + j, MOE_BLK, stride=ROW_TILES), :] = (
                        acc_ref[rows, j * LANES:(j + 1) * LANES])


def _moe(it_e, it_start, it_rows, slot_src, slot_dst, h2, w1, b1, w2, b2):
    def chunk_of(it, c, n_ref):
        return jnp.where(n_ref[it] > 0, c, MOE_NCHUNK - 1)

    def w1_map(it, c, e_ref, s_ref, n_ref, *_):
        return (e_ref[it], 0, chunk_of(it, c, n_ref))

    def w2_map(it, c, e_ref, s_ref, n_ref, *_):
        return (e_ref[it], chunk_of(it, c, n_ref), 0)

    def b2_map(it, c, e_ref, *_):
        return (e_ref[it], 0, 0)

    sel = (jnp.arange(2 * MOE_TH, dtype=jnp.int32)[:, None]
           == 2 * jnp.arange(MOE_TH, dtype=jnp.int32)[None, :]).astype(BF16)

    grid_spec = pltpu.PrefetchScalarGridSpec(
        num_scalar_prefetch=5,
        grid=(MOE_NITEMS + 1, MOE_NCHUNK),
        in_specs=[pl.BlockSpec(memory_space=pl.ANY),
                  pl.BlockSpec((1, D_MODEL, 2 * MOE_TH), w1_map),
                  pl.BlockSpec((1, 1, 2 * MOE_TH), w1_map),
                  pl.BlockSpec((1, MOE_TH, D_MODEL), w2_map),
                  pl.BlockSpec((1, 1, D_MODEL), b2_map),
                  pl.BlockSpec((2 * MOE_TH, MOE_TH), lambda it, c, *_: (0, 0))],
        out_specs=pl.BlockSpec(memory_space=pl.ANY),
        scratch_shapes=[pltpu.VMEM((MOE_ITEM_ROWS * ROW_TILES, LANES), F32),
                        pltpu.VMEM((MOE_ITEM_ROWS, D_MODEL), BF16),
                        pltpu.VMEM((MOE_ITEM_ROWS, D_MODEL), F32),
                        pltpu.VMEM((MOE_ITEM_ROWS * ROW_TILES, LANES), F32),
                        pltpu.VMEM((D_MODEL, 2 * MOE_TH), BF16),
                        pltpu.VMEM((MOE_TH, D_MODEL), BF16),
                        pltpu.SemaphoreType.DMA(()),
                        pltpu.SemaphoreType.DMA(())],
    )
    return pl.pallas_call(
        _moe_kernel,
        grid_spec=grid_spec,
        out_shape=jax.ShapeDtypeStruct(((N_ASSIGN + MOE_QUOTA_ROWS) * ROW_TILES, LANES), F32),
        compiler_params=_cparams(("arbitrary", "arbitrary")),
        name="moe_experts",
    )(it_e, it_start, it_rows, slot_src, slot_dst, h2, w1,
      b1.reshape(N_EXPERTS, 1, 2 * D_EXPERT), w2, b2.reshape(N_EXPERTS, 1, D_MODEL), sel)


def _route_tables(top_idx):
    e_flat = top_idx.reshape(N_ASSIGN)
    onehot = (e_flat[:, None] == jnp.arange(N_EXPERTS, dtype=jnp.int32)[None, :]).astype(jnp.int32)
    csum = jnp.cumsum(onehot, axis=0)
    rank = jnp.sum(csum * onehot, axis=-1) - 1
    counts = csum[-1]
    nblk = (counts + MOE_BLK - 1) // MOE_BLK
    blk_end = jnp.cumsum(nblk)
    blk_start = blk_end - nblk
    dest = blk_start[e_flat] * MOE_BLK + rank
    slot_a = jnp.full((MOE_SLOTS,), -1, jnp.int32).at[dest].set(jnp.arange(N_ASSIGN, dtype=jnp.int32))
    slot_src = jnp.where(slot_a >= 0, slot_a // TOP_K, 0)
    slot_dst = jnp.where(slot_a >= 0, (slot_a % TOP_K) * N_TOK + slot_a // TOP_K, -1)

    nit = (nblk + MOE_ITEM_BLKS - 1) // MOE_ITEM_BLKS
    it_end = jnp.cumsum(nit)
    it_begin = it_end - nit
    n_used = it_end[-1]
    ids = jnp.arange(MOE_NITEMS + 2, dtype=jnp.int32)
    valid = ids < n_used
    last = jnp.minimum(ids, n_used - 1)
    e_of = jnp.minimum(jnp.sum((it_end[None, :] <= last[:, None]).astype(jnp.int32), axis=1), N_EXPERTS - 1)
    local = ids - it_begin[e_of]
    it_start = jnp.where(valid, (blk_start[e_of] + local * MOE_ITEM_BLKS) * MOE_BLK, 0).astype(jnp.int32)
    it_rows = jnp.where(valid, jnp.clip(counts[e_of] - local * MOE_ITEM_ROWS, 0, MOE_ITEM_ROWS), 0)
    return e_of, it_start, it_rows.astype(jnp.int32), slot_src, slot_dst


FIN_TM = 512


def _final_kernel(x1_ref, ya_ref, yb_ref, yc_ref, yd_ref, tw_ref, gf_ref, sho_ref, sco_ref, nw_ref, o_ref):
    tw = tw_ref[...]
    pieces = []
    for j in range(ROW_TILES):
        acc = None
        for kk, y_ref in enumerate((ya_ref, yb_ref, yc_ref, yd_ref)):
            term = tw[:, kk:kk + 1] * y_ref[pl.ds(j, FIN_TM, stride=ROW_TILES), :]
            acc = term if acc is None else acc + term
        pieces.append(acc)
    moe = jnp.concatenate(pieces, axis=1)
    x2 = x1_ref[...] + gf_ref[0] * moe
    o_ref[...] = _rms(x2, nw_ref[...]) * (1.0 + sco_ref[0]) + sho_ref[0]


def _final(x1, y4, tw, gf, sho, sco, nw):
    tm = FIN_TM
    per_b = SEQ // tm
    row = lambda i: (i, 0)
    bvec = lambda i: (i // per_b, 0, 0)
    nb = N_TOK // tm
    kth = lambda kk: pl.BlockSpec((tm * ROW_TILES, LANES), lambda i: (kk * nb + i, 0))
    return pl.pallas_call(
        _final_kernel,
        grid=(nb,),
        in_specs=[pl.BlockSpec((tm, D_MODEL), row),
                  kth(0), kth(1), kth(2), kth(3),
                  pl.BlockSpec((tm, LANES), row),
                  pl.BlockSpec((1, 1, D_MODEL), bvec),
                  pl.BlockSpec((1, 1, D_MODEL), bvec),
                  pl.BlockSpec((1, 1, D_MODEL), bvec),
                  pl.BlockSpec((1, D_MODEL), lambda i: (0, 0))],
        out_specs=pl.BlockSpec((tm, D_MODEL), row),
        out_shape=jax.ShapeDtypeStruct((N_TOK, D_MODEL), F32),
        compiler_params=_cparams(("arbitrary",)),
        name="combine_final",
    )(x1, y4, y4, y4, y4, tw, gf, sho, sco, nw)


def _deinterleave(n):
    return np.concatenate([np.arange(0, n, 2), np.arange(1, n, 2)])


_IN_OFFS = np.cumsum([0, MLA_Q_RANK, MLA_KV_RANK + MLA_ROPE, GLA_HEADS * GLA_DK, GLA_HEADS * GLA_DK,
                      GLA_HEADS * GLA_DV, GLA_GATE_RANK, GLA_HEADS * GLA_DV, D_MODEL, D_MODEL])
_IN_MOVES = ((_IN_OFFS[0], MLA_Q_RANK, P_QLAT), (_IN_OFFS[1], MLA_KV_RANK, P_CKV),
             (_IN_OFFS[5], GLA_GATE_RANK, P_GLR), (_IN_OFFS[2], GLA_HEADS * GLA_DK, P_GQ),
             (_IN_OFFS[3], GLA_HEADS * GLA_DK, P_GK), (_IN_OFFS[4], GLA_HEADS * GLA_DV, P_GV),
             (_IN_OFFS[6], GLA_HEADS * GLA_DV, P_GOUT), (_IN_OFFS[7], D_MODEL, P_GATEA),
             (_IN_OFFS[8], D_MODEL, P_GATEB))
PACK_TM = 256


def _pack_kernel(w_ref, kpe_ref, o_ref):
    o_ref[:, P_KPE:P_KPE + LANES] = kpe_ref[...]
    o_ref[:, P_GLR:P_GLR + LANES] = jnp.zeros((PACK_TM, LANES), BF16)
    for src, width, dst in _IN_MOVES:
        o_ref[:, dst:dst + width] = w_ref[:, int(src):int(src) + width].astype(BF16)


def _pack_w_in(w_in):
    d_in = w_in.shape[1]
    kpe = w_in[:, _IN_OFFS[1] + MLA_KV_RANK:_IN_OFFS[2]][:, _deinterleave(MLA_ROPE)]
    kpe = jnp.concatenate([kpe, jnp.zeros((D_MODEL, LANES - MLA_ROPE), w_in.dtype)], axis=1).astype(BF16)
    return pl.pallas_call(
        _pack_kernel,
        grid=(D_MODEL // PACK_TM,),
        in_specs=[pl.BlockSpec((PACK_TM, d_in), lambda i: (i, 0)),
                  pl.BlockSpec((PACK_TM, LANES), lambda i: (i, 0))],
        out_specs=pl.BlockSpec((PACK_TM, P_TOTAL), lambda i: (i, 0)),
        out_shape=jax.ShapeDtypeStruct((D_MODEL, P_TOTAL), BF16),
        compiler_params=_cparams(("arbitrary",)),
        name="pack_w_in",
    )(w_in, kpe)


def _pack_w_q_b(w):
    w = w.reshape(MLA_Q_RANK, MLA_HEADS, MLA_NOPE + MLA_ROPE)
    nope = w[:, :, :MLA_NOPE]
    pe = w[:, :, MLA_NOPE:][:, :, _deinterleave(MLA_ROPE)]
    pad = jnp.zeros((MLA_Q_RANK, MLA_HEADS, MLA_HD - MLA_NOPE - MLA_ROPE), w.dtype)
    return jnp.concatenate([nope, pe, pad], axis=-1).reshape(MLA_Q_RANK, MLA_HEADS * MLA_HD).astype(BF16)


def _rope_consts():
    inv_freq = ROPE_THETA ** (-(jnp.arange(0, MLA_ROPE, 2, dtype=F32) / MLA_ROPE))
    half = MLA_ROPE // 2
    z = jnp.zeros((LANES - MLA_ROPE,), F32)
    rows = [jnp.concatenate([inv_freq, inv_freq, z]),
            jnp.concatenate([-jnp.ones((half,), F32), jnp.zeros((half,), F32), z]),
            jnp.concatenate([jnp.zeros((half,), F32), jnp.ones((half,), F32), z])]
    return jnp.concatenate([jnp.stack(rows), jnp.zeros((5, LANES), F32)], axis=0)


def kernel(x, c, positions, w_ada, b_ada, norm_mix_w, w_in, mla_q_norm_w, mla_w_q_b, mla_kv_norm_w,
           mla_w_kv_b, gla_w_gk_up, gla_b_gk_up, gla_norm_w, w_o_mla, w_o_gla, w_out, norm_ffn_w,
           w_router, b_router, w1, b1, w2, b2, w_ada_final, b_ada_final, norm_final_w):
    assert w_ada.shape[0] == 1, "single-layer stack"
    x2d = x.reshape(N_TOK, D_MODEL)
    c_t = c.T

    mod = _adaln(c_t, w_ada[0], b_ada[0])
    fmod = _adaln(c_t, w_ada_final, b_ada_final)
    sh_m, sc_m, g_m, sh_f, sc_f, g_f = [m.reshape(BATCH, 1, D_MODEL) for m in jnp.split(mod, N_MOD, axis=-1)]
    sh_o, sc_o = [m.reshape(BATCH, 1, D_MODEL) for m in jnp.split(fmod, 2, axis=-1)]

    proj = _inproj(x2d, norm_mix_w[0].reshape(1, D_MODEL), sh_m, sc_m, _pack_w_in(w_in[0]))

    q, k, v = _mla_prep(proj, positions.reshape(N_TOK, 1), _rope_consts(),
                        mla_q_norm_w[0].reshape(1, MLA_Q_RANK), mla_kv_norm_w[0].reshape(1, MLA_KV_RANK),
                        _pack_w_q_b(mla_w_q_b[0]), mla_w_kv_b[0].astype(BF16))
    o_mla = _attention(q, k, v)

    wgk = jnp.concatenate([gla_w_gk_up[0], jnp.zeros((LANES - GLA_GATE_RANK, GLA_HEADS * GLA_DK), F32)], axis=0)
    o_gla = _gla(proj, wgk, gla_b_gk_up[0].reshape(1, GLA_HEADS * GLA_DK), gla_norm_w[0].reshape(1, GLA_DV))

    wr = jnp.concatenate([w_router[0], jnp.zeros((D_MODEL, LANES - N_EXPERTS), F32)], axis=1)
    wr_hi = wr.astype(BF16)
    wr_lo = (wr - wr_hi.astype(F32)).astype(BF16)
    br = jnp.concatenate([b_router[0], jnp.zeros((LANES - N_EXPERTS,), F32)]).reshape(1, LANES)
    x1, h2, top_idx, top_w = _merge(
        x2d, o_mla, o_gla, proj, g_m, sh_f, sc_f, norm_ffn_w[0].reshape(1, D_MODEL),
        w_o_mla[0].astype(BF16), w_o_gla[0].astype(BF16), w_out[0].astype(BF16),
        jnp.concatenate([wr_hi, wr_lo], axis=1), br)

    tables = _route_tables(top_idx[:, :TOP_K])
    y4 = _moe(*tables, h2, w1[0], b1[0], w2[0], b2[0])

    out = _final(x1, y4, top_w, g_f, sh_o, sc_o, norm_final_w.reshape(1, D_MODEL))
    return out.reshape(BATCH, SEQ, D_MODEL)
```

```python
import functools

import jax
import jax.numpy as jnp
import numpy as np
from jax import lax
from jax.experimental import pallas as pl
from jax.experimental.pallas import tpu as pltpu

F32 = jnp.float32
BF16 = jnp.bfloat16

D_MODEL = 2048
BATCH = 2
SEQ = 4096
N_TOK = BATCH * SEQ

MLA_HEADS = 8
MLA_Q_RANK = 512
MLA_KV_RANK = 256
MLA_NOPE = 128
MLA_ROPE = 64
MLA_V = 128
ROPE_THETA = 10000.0
MLA_HD = 256

GLA_HEADS = 4
GLA_DK = 128
GLA_DV = 256
GLA_GATE_RANK = 16
GLA_GATE_NORM = 16.0
GLA_CHUNK = 64

N_EXPERTS = 32
TOP_K = 4
D_EXPERT = D_MODEL
SWIGLU_LIMIT = 7.0
SWIGLU_ALPHA = 1.702
EPS = 1e-6
N_MOD = 6

LANES = 128
ROW_TILES = D_MODEL // LANES
VMEM_LIMIT = 56 * 1024 * 1024

P_QLAT = 0
P_CKV = 512
P_KPE = 768
P_GLR = 896
P_GQ = 1024
P_GK = 1536
P_GV = 2048
P_GOUT = 3072
P_GATEA = 4096
P_GATEB = 6144
P_TOTAL = 8192

MOE_BLK = 256
MOE_ITEM_BLKS = 5
MOE_ITEM_ROWS = MOE_BLK * MOE_ITEM_BLKS
MOE_TH = 256
MOE_NCHUNK = D_EXPERT // MOE_TH
N_ASSIGN = N_TOK * TOP_K
MOE_NBLK_MAX = N_ASSIGN // MOE_BLK + N_EXPERTS
MOE_SLOTS = MOE_NBLK_MAX * MOE_BLK
MOE_NITEMS = N_ASSIGN // MOE_ITEM_ROWS + N_EXPERTS


def _cparams(sem, vmem=VMEM_LIMIT):
    return pltpu.CompilerParams(dimension_semantics=sem, vmem_limit_bytes=vmem)


def _from_token_major(t):
    rows = t.shape[0] // ROW_TILES
    return pltpu.einshape("mjl->jml", t.reshape(rows, ROW_TILES, LANES))


def _rms(x, w):
    return x * lax.rsqrt(jnp.mean(x * x, axis=-1, keepdims=True) + EPS) * w


def _adaln_kernel(ct_ref, w_ref, b_ref, o_ref):
    ct = ct_ref[...]
    cond = ct * jax.nn.sigmoid(ct)
    w = w_ref[...]
    for b in range(BATCH):
        o_ref[b:b + 1, :] = jnp.sum(w * cond[:, b:b + 1], axis=0, keepdims=True) + b_ref[...]


def _adaln(c_t, w, b, tn=1024):
    d, n = w.shape
    return pl.pallas_call(
        _adaln_kernel,
        grid=(n // tn,),
        in_specs=[pl.BlockSpec((d, BATCH), lambda j: (0, 0)),
                  pl.BlockSpec((d, tn), lambda j: (0, j)),
                  pl.BlockSpec((1, tn), lambda j: (0, j))],
        out_specs=pl.BlockSpec((BATCH, tn), lambda j: (0, j)),
        out_shape=jax.ShapeDtypeStruct((BATCH, n), F32),
        compiler_params=_cparams(("arbitrary",)),
        name="adaln",
    )(c_t, w, b.reshape(1, n))


IN_TM = 1024
IN_TN = 2048


def _inproj_kernel(x_ref, nw_ref, sh_ref, sc_ref, w_ref, o_ref, h_ref):
    @pl.when(pl.program_id(1) == 0)
    def _():
        for r in range(0, IN_TM, 256):
            x = x_ref[r:r + 256, :]
            h = _rms(x, nw_ref[...]) * (1.0 + sc_ref[0]) + sh_ref[0]
            h_ref[r:r + 256, :] = h.astype(BF16)

    o_ref[...] = jnp.dot(h_ref[...], w_ref[...], preferred_element_type=F32).astype(BF16)


def _inproj(x2d, nw, sh, sc, w_packed):
    per_b = SEQ // IN_TM
    return pl.pallas_call(
        _inproj_kernel,
        grid=(N_TOK // IN_TM, P_TOTAL // IN_TN),
        in_specs=[pl.BlockSpec((IN_TM, D_MODEL), lambda i, j: (i, 0)),
                  pl.BlockSpec((1, D_MODEL), lambda i, j: (0, 0)),
                  pl.BlockSpec((1, 1, D_MODEL), lambda i, j: (i // per_b, 0, 0)),
                  pl.BlockSpec((1, 1, D_MODEL), lambda i, j: (i // per_b, 0, 0)),
                  pl.BlockSpec((D_MODEL, IN_TN), lambda i, j: (0, j))],
        out_specs=pl.BlockSpec((IN_TM, IN_TN), lambda i, j: (i, j)),
        out_shape=jax.ShapeDtypeStruct((N_TOK, P_TOTAL), BF16),
        scratch_shapes=[pltpu.VMEM((IN_TM, D_MODEL), BF16)],
        compiler_params=_cparams(("arbitrary", "arbitrary")),
        name="inproj",
    )(x2d, nw, sh, sc, w_packed)


PREP_TM = 512


def _mla_prep_kernel(ql_ref, ckv_ref, kpe_ref, pos_ref, rc_ref, qnw_ref, kvnw_ref, wq_ref, wkv_ref,
                     q_ref, k_ref, v_ref):
    scale = (MLA_NOPE + MLA_ROPE) ** -0.5
    qn = _rms(ql_ref[...].astype(F32), qnw_ref[...]).astype(BF16)
    q = jnp.dot(qn, wq_ref[...], preferred_element_type=F32)
    cn = _rms(ckv_ref[...].astype(F32), kvnw_ref[...]).astype(BF16)
    kv = jnp.dot(cn, wkv_ref[...], preferred_element_type=F32)

    ang = pos_ref[...].astype(F32) * rc_ref[0:1, :]
    cos_t = jnp.cos(ang)
    sin_t = jnp.sin(ang)
    sin_a = sin_t * rc_ref[1:2, :]
    sin_b = sin_t * rc_ref[2:3, :]

    def rope(t):
        return (t * cos_t + pltpu.roll(t, LANES - MLA_ROPE // 2, 1) * sin_a
                + pltpu.roll(t, MLA_ROPE // 2, 1) * sin_b)

    kpe = rope(kpe_ref[...].astype(F32)).astype(BF16)
    for h in range(MLA_HEADS):
        c0 = h * MLA_HD
        q_ref[0, h, :, 0:LANES] = (q[:, c0:c0 + LANES] * scale).astype(BF16)
        q_ref[0, h, :, LANES:MLA_HD] = (rope(q[:, c0 + LANES:c0 + MLA_HD]) * scale).astype(BF16)
        k_ref[0, h, :, 0:LANES] = kv[:, c0:c0 + LANES].astype(BF16)
        k_ref[0, h, :, LANES:MLA_HD] = kpe
        v_ref[0, h, :, :] = kv[:, c0 + LANES:c0 + MLA_HD].astype(BF16)


def _mla_prep(proj, pos_col, rope_consts, qnw, kvnw, wq, wkv):
    tm = PREP_TM
    per_b = SEQ // tm
    row = lambda i: (i, 0)
    const = lambda i: (0, 0)
    out_map = lambda i: (i // per_b, 0, i % per_b, 0)
    return pl.pallas_call(
        _mla_prep_kernel,
        grid=(N_TOK // tm,),
        in_specs=[pl.BlockSpec((tm, MLA_Q_RANK), lambda i: (i, P_QLAT // MLA_Q_RANK)),
                  pl.BlockSpec((tm, MLA_KV_RANK), lambda i: (i, P_CKV // MLA_KV_RANK)),
                  pl.BlockSpec((tm, LANES), lambda i: (i, P_KPE // LANES)),
                  pl.BlockSpec((tm, 1), row),
                  pl.BlockSpec((8, LANES), const),
                  pl.BlockSpec((1, MLA_Q_RANK), const),
                  pl.BlockSpec((1, MLA_KV_RANK), const),
                  pl.BlockSpec((MLA_Q_RANK, MLA_HEADS * MLA_HD), const),
                  pl.BlockSpec((MLA_KV_RANK, MLA_HEADS * MLA_HD), const)],
        out_specs=[pl.BlockSpec((1, MLA_HEADS, tm, MLA_HD), out_map),
                   pl.BlockSpec((1, MLA_HEADS, tm, MLA_HD), out_map),
                   pl.BlockSpec((1, MLA_HEADS, tm, MLA_V), out_map)],
        out_shape=[jax.ShapeDtypeStruct((BATCH, MLA_HEADS, SEQ, MLA_HD), BF16),
                   jax.ShapeDtypeStruct((BATCH, MLA_HEADS, SEQ, MLA_HD), BF16),
                   jax.ShapeDtypeStruct((BATCH, MLA_HEADS, SEQ, MLA_V), BF16)],
        compiler_params=_cparams(("arbitrary",)),
        name="mla_prep",
    )(proj, proj, proj, pos_col, rope_consts, qnw, kvnw, wq, wkv)


ATT_T = 1024


ATT_HPS = 2


def _attn_kernel(q_ref, k_ref, v_ref, o_ref, m_ref, l_ref, acc_ref, s0_ref, s1_ref):
    qi = pl.program_id(2)
    m_ref[...] = jnp.full(m_ref.shape, -jnp.inf, F32)
    l_ref[...] = jnp.zeros(l_ref.shape, F32)
    acc_ref[...] = jnp.zeros(acc_ref.shape, F32)

    def scores(hh, j):
        r0 = pl.multiple_of(j * ATT_T, ATT_T)
        k = k_ref[0, hh, pl.ds(r0, ATT_T), :]
        return lax.dot_general(k, q_ref[0, hh], (((1,), (1,)), ((), ())), preferred_element_type=F32)

    def head_step(hh, j, src_ref, dst_ref):
        masked = dst_ref is None
        r0 = pl.multiple_of(j * ATT_T, ATT_T)
        v = v_ref[0, hh, pl.ds(r0, ATT_T), :]
        s = src_ref[hh]
        if not masked:
            dst_ref[hh] = scores(hh, j + 1)
        if masked:
            key = lax.broadcasted_iota(jnp.int32, s.shape, 0)
            qry = lax.broadcasted_iota(jnp.int32, s.shape, 1)
            s = jnp.where(key <= qry, s, -jnp.inf)
        m_old = m_ref[hh]
        m_new = jnp.maximum(m_old, jnp.max(s, axis=0, keepdims=True))
        p = jnp.exp(s - m_new)
        alpha = jnp.exp(m_old - m_new)
        l_ref[hh] = alpha * l_ref[hh] + jnp.sum(p, axis=0, keepdims=True)
        pv = lax.dot_general(v, p.astype(BF16), (((0,), (0,)), ((), ())), preferred_element_type=F32)
        acc_ref[hh] = alpha * acc_ref[hh] + pv
        m_ref[hh] = m_new

    def step(j, src_ref, dst_ref):
        for hh in range(ATT_HPS):
            head_step(hh, j, src_ref, dst_ref)

    def body(jj, carry):
        step(2 * jj, s0_ref, s1_ref)
        step(2 * jj + 1, s1_ref, s0_ref)
        return carry

    for hh in range(ATT_HPS):
        s0_ref[hh] = scores(hh, 0)
    lax.fori_loop(0, qi // 2, body, 0)

    @pl.when(qi % 2 == 1)
    def _():
        step(qi - 1, s0_ref, s1_ref)
        step(qi, s1_ref, None)

    @pl.when(qi % 2 == 0)
    def _():
        step(qi, s0_ref, None)

    for hh in range(ATT_HPS):
        o_ref[:, hh * MLA_V:(hh + 1) * MLA_V] = jnp.transpose(acc_ref[hh] / l_ref[hh]).astype(BF16)


def _attention(q, k, v):
    t = ATT_T
    nq = SEQ // t
    hps = ATT_HPS
    return pl.pallas_call(
        _attn_kernel,
        grid=(BATCH, MLA_HEADS // hps, nq),
        in_specs=[pl.BlockSpec((1, hps, t, MLA_HD), lambda b, h, i: (b, h, i, 0)),
                  pl.BlockSpec((1, hps, SEQ, MLA_HD), lambda b, h, i: (b, h, 0, 0)),
                  pl.BlockSpec((1, hps, SEQ, MLA_V), lambda b, h, i: (b, h, 0, 0))],
        out_specs=pl.BlockSpec((t, hps * MLA_V), lambda b, h, i: (b * nq + i, h)),
        out_shape=jax.ShapeDtypeStruct((N_TOK, MLA_HEADS * MLA_V), BF16),
        scratch_shapes=[pltpu.VMEM((hps, 1, t), F32), pltpu.VMEM((hps, 1, t), F32),
                        pltpu.VMEM((hps, MLA_V, t), F32),
                        pltpu.VMEM((hps, t, t), F32), pltpu.VMEM((hps, t, t), F32)],
        compiler_params=_cparams(("arbitrary", "arbitrary", "arbitrary")),
        name="mla_attn",
    )(q, k, v)


GLA_TG = 512
GLA_DIAG = 8
GLA_HPS = 4


def _gla_kernel(q_ref, k_ref, v_ref, glr_ref, gout_ref, wgk_ref, bgk_ref, nw_ref, o_ref, st_ref, g_ref):
    C = GLA_CHUNK

    @pl.when(pl.program_id(2) == 0)
    def _():
        st_ref[...] = jnp.zeros(st_ref.shape, F32)

    z = jnp.dot(glr_ref[...].astype(F32), wgk_ref[...], preferred_element_type=F32,
                precision=lax.Precision.HIGHEST) + bgk_ref[...]
    g_ref[...] = jax.nn.log_sigmoid(z) / GLA_GATE_NORM

    row = lax.broadcasted_iota(jnp.int32, (C, GLA_DK), 0)
    ii = lax.broadcasted_iota(jnp.int32, (C, C), 0)
    jj = lax.broadcasted_iota(jnp.int32, (C, C), 1)
    nt = (((1,), (1,)), ((), ()))

    def chunk(ci, carry):
        for hh in range(GLA_HPS):
            head_chunk(ci, hh)
        return carry

    def head_chunk(ci, hh):
        r0 = pl.multiple_of(ci * C, C)
        kcols = slice(hh * GLA_DK, (hh + 1) * GLA_DK)
        vcols = slice(hh * GLA_DV, (hh + 1) * GLA_DV)
        b = g_ref[pl.ds(r0, C), kcols]
        s = 1
        while s < C:
            b = b + jnp.where(row >= s, pltpu.roll(b, s, 0), 0.0)
            s *= 2
        q = q_ref[pl.ds(r0, C), kcols].astype(F32) * (GLA_DK ** -0.5)
        k = k_ref[pl.ds(r0, C), kcols].astype(F32)
        v = v_ref[pl.ds(r0, C), vcols]
        st = st_ref[hh]

        o = lax.dot_general((q * jnp.exp(b)).astype(BF16), st.astype(BF16), nt,
                            preferred_element_type=F32)

        a = jnp.zeros((C, C), F32)
        lvl = C // 2
        while lvl >= GLA_DIAG:
            pieces = [jnp.broadcast_to(b[t + lvl:t + lvl + 1, :], (2 * lvl, GLA_DK))
                      for t in range(0, C, 2 * lvl)]
            m = pieces[0] if len(pieces) == 1 else jnp.concatenate(pieces, axis=0)
            odd = (row // lvl) % 2 == 1
            ql = jnp.where(odd, q * jnp.exp(jnp.minimum(b - m, 0.0)), 0.0).astype(BF16)
            kl = jnp.where(odd, 0.0, k * jnp.exp(jnp.minimum(m - b, 0.0))).astype(BF16)
            al = lax.dot_general(ql, kl, nt, preferred_element_type=F32)
            a = a + jnp.where(ii // (2 * lvl) == jj // (2 * lvl), al, 0.0)
            lvl //= 2
        for d in range(GLA_DIAG):
            if d == 0:
                t = q * k
            else:
                t = q * pltpu.roll(k, d, 0) * jnp.exp(jnp.minimum(b - pltpu.roll(b, d, 0), 0.0))
            col = jnp.sum(t, axis=-1, keepdims=True)
            a = a + jnp.where((ii - jj == d) & (ii % GLA_DIAG >= d), col, 0.0)
        o = o + jnp.dot(a.astype(BF16), v, preferred_element_type=F32)

        b_last = b[C - 1:C, :]
        kdec = (k * jnp.exp(b_last - b)).astype(BF16)
        st_ref[hh] = st * jnp.exp(b_last) + lax.dot_general(
            v, kdec, (((0,), (0,)), ((), ())), preferred_element_type=F32)

        gate = gout_ref[pl.ds(r0, C), vcols].astype(F32)
        y = _rms(o, nw_ref[...]) * (gate * jax.nn.sigmoid(gate))
        o_ref[pl.ds(r0, C), vcols] = y.astype(BF16)

    lax.fori_loop(0, GLA_TG // C, chunk, 0, unroll=4)


def _gla(proj, wgk, bgk, nw):
    tg = GLA_TG
    per_b = SEQ // tg
    kw = GLA_HPS * GLA_DK
    vw = GLA_HPS * GLA_DV
    rb = lambda b, h, t: b * per_b + t
    return pl.pallas_call(
        _gla_kernel,
        grid=(BATCH, GLA_HEADS // GLA_HPS, per_b),
        in_specs=[pl.BlockSpec((tg, kw), lambda b, h, t: (rb(b, h, t), P_GQ // kw + h)),
                  pl.BlockSpec((tg, kw), lambda b, h, t: (rb(b, h, t), P_GK // kw + h)),
                  pl.BlockSpec((tg, vw), lambda b, h, t: (rb(b, h, t), P_GV // vw + h)),
                  pl.BlockSpec((tg, LANES), lambda b, h, t: (rb(b, h, t), P_GLR // LANES)),
                  pl.BlockSpec((tg, vw), lambda b, h, t: (rb(b, h, t), P_GOUT // vw + h)),
                  pl.BlockSpec((LANES, kw), lambda b, h, t: (0, h)),
                  pl.BlockSpec((1, kw), lambda b, h, t: (0, h)),
                  pl.BlockSpec((1, GLA_DV), lambda b, h, t: (0, 0))],
        out_specs=pl.BlockSpec((tg, vw), lambda b, h, t: (rb(b, h, t), h)),
        out_shape=jax.ShapeDtypeStruct((N_TOK, GLA_HEADS * GLA_DV), BF16),
        scratch_shapes=[pltpu.VMEM((GLA_HPS, GLA_DV, GLA_DK), F32), pltpu.VMEM((tg, kw), F32)],
        compiler_params=_cparams(("arbitrary", "arbitrary", "arbitrary")),
        name="gla",
    )(proj, proj, proj, proj, proj, wgk, bgk, nw)


MRG_TM = 256


def _merge_kernel(x_ref, oa_ref, ob_ref, ga_ref, gb_ref, gm_ref, shf_ref, scf_ref, nfw_ref,
                  woa_ref, wob_ref, wout_ref, wr_ref, br_ref,
                  x1_ref, h2_ref, idx_ref, tw_ref):
    ya = jnp.dot(oa_ref[...], woa_ref[...], preferred_element_type=F32)
    yb = jnp.dot(ob_ref[...], wob_ref[...], preferred_element_type=F32)
    merged = (jax.nn.sigmoid(ga_ref[...].astype(F32)) * ya
              + jax.nn.sigmoid(gb_ref[...].astype(F32)) * yb).astype(BF16)
    x1 = x_ref[...] + gm_ref[0] * jnp.dot(merged, wout_ref[...], preferred_element_type=F32)
    x1_ref[...] = x1
    h2 = _rms(x1, nfw_ref[...]) * (1.0 + scf_ref[0]) + shf_ref[0]
    for j in range(ROW_TILES):
        h2_ref[pl.ds(j, MRG_TM, stride=ROW_TILES), :] = h2[:, j * LANES:(j + 1) * LANES]

    hi = h2.astype(BF16)
    lo = (h2 - hi.astype(F32)).astype(BF16)
    both = jnp.dot(hi, wr_ref[...], preferred_element_type=F32)
    logits = (both[:, :LANES] + both[:, LANES:]
              + jnp.dot(lo, wr_ref[:, :LANES], preferred_element_type=F32)) + br_ref[...]
    lane = lax.broadcasted_iota(jnp.int32, logits.shape, 1)
    vals = jnp.where(lane < N_EXPERTS, logits, -jnp.inf)
    idx_out = jnp.zeros(logits.shape, jnp.int32)
    w_out = jnp.zeros(logits.shape, F32)
    top = None
    denom = None
    for kk in range(TOP_K):
        m = jnp.max(vals, axis=-1, keepdims=True)
        sel = jnp.min(jnp.where(vals == m, lane, LANES), axis=-1, keepdims=True)
        if kk == 0:
            top = m
        e = jnp.exp(m - top)
        denom = e if kk == 0 else denom + e
        idx_out = jnp.where(lane == kk, sel, idx_out)
        w_out = jnp.where(lane == kk, e, w_out)
        vals = jnp.where(lane == sel, -jnp.inf, vals)
    idx_ref[...] = idx_out
    tw_ref[...] = w_out / denom


def _merge(x2d, o_mla, o_gla, proj, gm, shf, scf, nfw, woa, wob, wout, wr, br):
    tm = MRG_TM
    per_b = SEQ // tm
    row = lambda i: (i, 0)
    const = lambda i: (0, 0)
    bvec = lambda i: (i // per_b, 0, 0)
    single = dict(pipeline_mode=pl.Buffered(1))
    return pl.pallas_call(
        _merge_kernel,
        grid=(N_TOK // tm,),
        in_specs=[pl.BlockSpec((tm, D_MODEL), row),
                  pl.BlockSpec((tm, MLA_HEADS * MLA_V), row),
                  pl.BlockSpec((tm, GLA_HEADS * GLA_DV), row),
                  pl.BlockSpec((tm, D_MODEL), lambda i: (i, P_GATEA // D_MODEL)),
                  pl.BlockSpec((tm, D_MODEL), lambda i: (i, P_GATEB // D_MODEL)),
                  pl.BlockSpec((1, 1, D_MODEL), bvec),
                  pl.BlockSpec((1, 1, D_MODEL), bvec),
                  pl.BlockSpec((1, 1, D_MODEL), bvec),
                  pl.BlockSpec((1, D_MODEL), const),
                  pl.BlockSpec((MLA_HEADS * MLA_V, D_MODEL), const, **single),
                  pl.BlockSpec((GLA_HEADS * GLA_DV, D_MODEL), const, **single),
                  pl.BlockSpec((D_MODEL, D_MODEL), const, **single),
                  pl.BlockSpec((D_MODEL, 2 * LANES), const, **single),
                  pl.BlockSpec((1, LANES), const)],
        out_specs=[pl.BlockSpec((tm, D_MODEL), row),
                   pl.BlockSpec((tm * ROW_TILES, LANES), row),
                   pl.BlockSpec((tm, LANES), row),
                   pl.BlockSpec((tm, LANES), row)],
        out_shape=[jax.ShapeDtypeStruct((N_TOK, D_MODEL), F32),
                   jax.ShapeDtypeStruct((N_TOK * ROW_TILES, LANES), F32),
                   jax.ShapeDtypeStruct((N_TOK, LANES), jnp.int32),
                   jax.ShapeDtypeStruct((N_TOK, LANES), F32)],
        compiler_params=_cparams(("arbitrary",)),
        name="merge_router",
    )(x2d, o_mla, o_gla, proj, proj, gm, shf, scf, nfw, woa, wob, wout, wr, br)


ROW_COPY_PRIORITY = 1
MOE_CHAINS = ((0, 5),)
MOE_QUOTA_ROWS = 4 * MOE_BLK
MOE_STEP_ROWS = MOE_QUOTA_ROWS // MOE_NCHUNK
MOE_QUOTAS = tuple(MOE_STEP_ROWS * w // MOE_ITEM_BLKS for _, w in MOE_CHAINS)
assert sum(w for _, w in MOE_CHAINS) == MOE_ITEM_BLKS and sum(MOE_QUOTAS) == MOE_STEP_ROWS


def _moe_kernel(it_e_ref, it_start_ref, it_rows_ref, src_ref, dst_ref,
                h2_hbm, w1_ref, b1_ref, w2_ref, b2_ref, sel_ref, y4_hbm,
                stage_ref, xb_ref, acc_ref, yout_ref, w1b_ref, w2b_ref, gsem, ssem):
    it = pl.program_id(0)
    c = pl.program_id(1)
    cnt = it_rows_ref[it]
    nblk = (cnt + MOE_BLK - 1) // MOE_BLK
    prev = jnp.maximum(it - 1, 0)
    rows_prev = jnp.where(it > 0, it_rows_ref[prev], 0)
    rows_next = it_rows_ref[it + 1]
    live = cnt > 0
    prev_live = rows_prev > 0
    base_prev = it_start_ref[prev]
    base_next = it_start_ref[it + 1]

    def token_rows(t):
        return pl.ds(pl.multiple_of(t * ROW_TILES, ROW_TILES), ROW_TILES)

    def gather_row(r, base, nrows, priority=ROW_COPY_PRIORITY):
        tok = jnp.where(r < nrows, src_ref[jnp.minimum(base + r, MOE_SLOTS - 1)], 0)
        pltpu.make_async_copy(h2_hbm.at[token_rows(tok), :], stage_ref.at[token_rows(r), :],
                              gsem).start(priority=priority)

    def scatter_row(r, priority=ROW_COPY_PRIORITY):
        d = dst_ref[jnp.minimum(base_prev + r, MOE_SLOTS - 1)]
        d = jnp.where(jnp.logical_and(r < rows_prev, d >= 0), d, N_ASSIGN + r)
        pltpu.make_async_copy(yout_ref.at[token_rows(r), :], y4_hbm.at[token_rows(d), :],
                              ssem).start(priority=priority)

    def issue_quota(ci, gather=True):
        r0 = c * MOE_STEP_ROWS + sum(MOE_QUOTAS[:ci])
        for u in range(MOE_QUOTAS[ci]):
            if gather:
                gather_row(r0 + u, base_next, rows_next)
            scatter_row(r0 + u)

    def tail(nrows):
        return MOE_QUOTA_ROWS, jnp.maximum(nrows, MOE_QUOTA_ROWS)

    def quota_rows(ref):
        return ref.at[pl.ds(0, MOE_QUOTA_ROWS * ROW_TILES), :]

    @pl.when(jnp.logical_and(it == 0, c == 0))
    def _():
        stage_ref[...] = jnp.zeros(stage_ref.shape, F32)
        yout_ref[...] = jnp.zeros(yout_ref.shape, F32)

        def first(r, carry):
            gather_row(r, it_start_ref[0], cnt)
            return carry
        lax.fori_loop(0, tail(cnt)[1], first, 0)

    @pl.when(jnp.logical_and(c == 0, jnp.logical_or(it == 0, prev_live)))
    def _():
        pltpu.make_async_copy(quota_rows(h2_hbm), quota_rows(stage_ref), gsem).wait()

        def drain(r, carry):
            pltpu.make_async_copy(h2_hbm.at[token_rows(0), :], stage_ref.at[token_rows(r), :], gsem).wait()
            return carry
        lax.fori_loop(*tail(cnt), drain, 0)

    @pl.when(jnp.logical_and(c == 0, live))
    def _():
        for s in range(MOE_ITEM_BLKS):
            @pl.when(s < nblk)
            def _():
                rows = slice(s * MOE_BLK, (s + 1) * MOE_BLK)
                blk = _from_token_major(stage_ref[s * MOE_BLK * ROW_TILES:(s + 1) * MOE_BLK * ROW_TILES, :])
                for j in range(ROW_TILES):
                    xb_ref[rows, j * LANES:(j + 1) * LANES] = blk[j].astype(BF16)
                acc_ref[rows, :] = jnp.broadcast_to(b2_ref[0], (MOE_BLK, D_MODEL))

    @pl.when(c == 0)
    def _():
        @pl.when(live)
        def _():
            def more(r, carry):
                gather_row(r, base_next, rows_next)
                return carry
            lax.fori_loop(*tail(rows_next), more, 0)

        def more(r, carry):
            scatter_row(r)
            return carry
        lax.fori_loop(*tail(rows_prev), more, 0)

    def compute(row0, nrows):
        rows = slice(row0, row0 + nrows)
        hm = jnp.dot(xb_ref[rows, :], w1b_ref[...], preferred_element_type=F32) + b1_ref[0]
        lin = pltpu.roll(hm, 2 * MOE_TH - 1, 1)
        glu = jnp.minimum(hm, SWIGLU_LIMIT)
        lin = jnp.clip(lin, -SWIGLU_LIMIT, SWIGLU_LIMIT)
        act = glu * jax.nn.sigmoid(SWIGLU_ALPHA * glu) * (lin + 1.0)
        lane = lax.broadcasted_iota(jnp.int32, act.shape, 1)
        act = jnp.where(lane % 2 == 0, act, 0.0).astype(BF16)
        act = jnp.dot(act, sel_ref[...], preferred_element_type=F32).astype(BF16)
        acc_ref[rows, :] += jnp.dot(act, w2b_ref[...], preferred_element_type=F32)

    @pl.when(live)
    def _():
        w1b_ref[...] = w1_ref[0].astype(BF16)
        w2b_ref[...] = w2_ref[0].astype(BF16)

    for ci, (first, width) in enumerate(MOE_CHAINS):
        in_chain = jnp.clip(nblk - first, 0, width)
        for nb in range(width, 0, -1):
            @pl.when(in_chain == nb)
            def _():
                compute(first * MOE_BLK, nb * MOE_BLK)
                issue_quota(ci)

        @pl.when(jnp.logical_and(in_chain == 0, live))
        def _():
            issue_quota(ci)

        @pl.when(jnp.logical_and(jnp.logical_not(live), prev_live))
        def _():
            issue_quota(ci, gather=False)

    @pl.when(jnp.logical_and(c == MOE_NCHUNK - 1, jnp.logical_or(live, prev_live)))
    def _():
        pltpu.make_async_copy(quota_rows(yout_ref), quota_rows(y4_hbm), ssem).wait()

        def drain(r, carry):
            pltpu.make_async_copy(yout_ref.at[token_rows(r), :], y4_hbm.at[token_rows(0), :], ssem).wait()
            return carry
        lax.fori_loop(*tail(rows_prev), drain, 0)

    @pl.when(jnp.logical_and(c == MOE_NCHUNK - 1, live))
    def _():
        for s in range(MOE_ITEM_BLKS):
            @pl.when(s < nblk)
            def _():
                rows = slice(s * MOE_BLK, (s + 1) * MOE_BLK)
                for j in range(ROW_TILES):
                    yout_ref[pl.ds(s * MOE_BLK * ROW_TILES + j, MOE_BLK, stride=ROW_TILES), :] = (
                        acc_ref[rows, j * LANES:(j + 1) * LANES])


def _moe(it_e, it_start, it_rows, slot_src, slot_dst, h2, w1, b1, w2, b2):
    def chunk_of(it, c, n_ref):
        return jnp.where(n_ref[it] > 0, c, MOE_NCHUNK - 1)

    def w1_map(it, c, e_ref, s_ref, n_ref, *_):
        return (e_ref[it], 0, chunk_of(it, c, n_ref))

    def w2_map(it, c, e_ref, s_ref, n_ref, *_):
        return (e_ref[it], chunk_of(it, c, n_ref), 0)

    def b2_map(it, c, e_ref, *_):
        return (e_ref[it], 0, 0)

    sel = (jnp.arange(2 * MOE_TH, dtype=jnp.int32)[:, None]
           == 2 * jnp.arange(MOE_TH, dtype=jnp.int32)[None, :]).astype(BF16)

    grid_spec = pltpu.PrefetchScalarGridSpec(
        num_scalar_prefetch=5,
        grid=(MOE_NITEMS + 1, MOE_NCHUNK),
        in_specs=[pl.BlockSpec(memory_space=pl.ANY),
                  pl.BlockSpec((1, D_MODEL, 2 * MOE_TH), w1_map),
                  pl.BlockSpec((1, 1, 2 * MOE_TH), w1_map),
                  pl.BlockSpec((1, MOE_TH, D_MODEL), w2_map),
                  pl.BlockSpec((1, 1, D_MODEL), b2_map),
                  pl.BlockSpec((2 * MOE_TH, MOE_TH), lambda it, c, *_: (0, 0))],
        out_specs=pl.BlockSpec(memory_space=pl.ANY),
        scratch_shapes=[pltpu.VMEM((MOE_ITEM_ROWS * ROW_TILES, LANES), F32),
                        pltpu.VMEM((MOE_ITEM_ROWS, D_MODEL), BF16),
                        pltpu.VMEM((MOE_ITEM_ROWS, D_MODEL), F32),
                        pltpu.VMEM((MOE_ITEM_ROWS * ROW_TILES, LANES), F32),
                        pltpu.VMEM((D_MODEL, 2 * MOE_TH), BF16),
                        pltpu.VMEM((MOE_TH, D_MODEL), BF16),
                        pltpu.SemaphoreType.DMA(()),
                        pltpu.SemaphoreType.DMA(())],
    )
    return pl.pallas_call(
        _moe_kernel,
        grid_spec=grid_spec,
        out_shape=jax.ShapeDtypeStruct(((N_ASSIGN + MOE_QUOTA_ROWS) * ROW_TILES, LANES), F32),
        compiler_params=_cparams(("arbitrary", "arbitrary")),
        name="moe_experts",
    )(it_e, it_start, it_rows, slot_src, slot_dst, h2, w1,
      b1.reshape(N_EXPERTS, 1, 2 * D_EXPERT), w2, b2.reshape(N_EXPERTS, 1, D_MODEL), sel)


def _route_tables(top_idx):
    e_flat = top_idx.reshape(N_ASSIGN)
    onehot = (e_flat[:, None] == jnp.arange(N_EXPERTS, dtype=jnp.int32)[None, :]).astype(jnp.int32)
    csum = jnp.cumsum(onehot, axis=0)
    rank = jnp.sum(csum * onehot, axis=-1) - 1
    counts = csum[-1]
    nblk = (counts + MOE_BLK - 1) // MOE_BLK
    blk_end = jnp.cumsum(nblk)
    blk_start = blk_end - nblk
    dest = blk_start[e_flat] * MOE_BLK + rank
    slot_a = jnp.full((MOE_SLOTS,), -1, jnp.int32).at[dest].set(jnp.arange(N_ASSIGN, dtype=jnp.int32))
    slot_src = jnp.where(slot_a >= 0, slot_a // TOP_K, 0)
    slot_dst = jnp.where(slot_a >= 0, (slot_a % TOP_K) * N_TOK + slot_a // TOP_K, -1)

    nit = (nblk + MOE_ITEM_BLKS - 1) // MOE_ITEM_BLKS
    it_end = jnp.cumsum(nit)
    it_begin = it_end - nit
    n_used = it_end[-1]
    ids = jnp.arange(MOE_NITEMS + 2, dtype=jnp.int32)
    valid = ids < n_used
    last = jnp.minimum(ids, n_used - 1)
    e_of = jnp.minimum(jnp.sum((it_end[None, :] <= last[:, None]).astype(jnp.int32), axis=1), N_EXPERTS - 1)
    local = ids - it_begin[e_of]
    it_start = jnp.where(valid, (blk_start[e_of] + local * MOE_ITEM_BLKS) * MOE_BLK, 0).astype(jnp.int32)
    it_rows = jnp.where(valid, jnp.clip(counts[e_of] - local * MOE_ITEM_ROWS, 0, MOE_ITEM_ROWS), 0)
    return e_of, it_start, it_rows.astype(jnp.int32), slot_src, slot_dst


FIN_TM = 512


def _final_kernel(x1_ref, ya_ref, yb_ref, yc_ref, yd_ref, tw_ref, gf_ref, sho_ref, sco_ref, nw_ref, o_ref):
    tw = tw_ref[...]
    pieces = []
    for j in range(ROW_TILES):
        acc = None
        for kk, y_ref in enumerate((ya_ref, yb_ref, yc_ref, yd_ref)):
            term = tw[:, kk:kk + 1] * y_ref[pl.ds(j, FIN_TM, stride=ROW_TILES), :]
            acc = term if acc is None else acc + term
        pieces.append(acc)
    moe = jnp.concatenate(pieces, axis=1)
    x2 = x1_ref[...] + gf_ref[0] * moe
    o_ref[...] = _rms(x2, nw_ref[...]) * (1.0 + sco_ref[0]) + sho_ref[0]


def _final(x1, y4, tw, gf, sho, sco, nw):
    tm = FIN_TM
    per_b = SEQ // tm
    row = lambda i: (i, 0)
    bvec = lambda i: (i // per_b, 0, 0)
    nb = N_TOK // tm
    kth = lambda kk: pl.BlockSpec((tm * ROW_TILES, LANES), lambda i: (kk * nb + i, 0))
    return pl.pallas_call(
        _final_kernel,
        grid=(nb,),
        in_specs=[pl.BlockSpec((tm, D_MODEL), row),
                  kth(0), kth(1), kth(2), kth(3),
                  pl.BlockSpec((tm, LANES), row),
                  pl.BlockSpec((1, 1, D_MODEL), bvec),
                  pl.BlockSpec((1, 1, D_MODEL), bvec),
                  pl.BlockSpec((1, 1, D_MODEL), bvec),
                  pl.BlockSpec((1, D_MODEL), lambda i: (0, 0))],
        out_specs=pl.BlockSpec((tm, D_MODEL), row),
        out_shape=jax.ShapeDtypeStruct((N_TOK, D_MODEL), F32),
        compiler_params=_cparams(("arbitrary",)),
        name="combine_final",
    )(x1, y4, y4, y4, y4, tw, gf, sho, sco, nw)


def _deinterleave(n):
    return np.concatenate([np.arange(0, n, 2), np.arange(1, n, 2)])


_IN_OFFS = np.cumsum([0, MLA_Q_RANK, MLA_KV_RANK + MLA_ROPE, GLA_HEADS * GLA_DK, GLA_HEADS * GLA_DK,
                      GLA_HEADS * GLA_DV, GLA_GATE_RANK, GLA_HEADS * GLA_DV, D_MODEL, D_MODEL])
_IN_MOVES = ((_IN_OFFS[0], MLA_Q_RANK, P_QLAT), (_IN_OFFS[1], MLA_KV_RANK, P_CKV),
             (_IN_OFFS[5], GLA_GATE_RANK, P_GLR), (_IN_OFFS[2], GLA_HEADS * GLA_DK, P_GQ),
             (_IN_OFFS[3], GLA_HEADS * GLA_DK, P_GK), (_IN_OFFS[4], GLA_HEADS * GLA_DV, P_GV),
             (_IN_OFFS[6], GLA_HEADS * GLA_DV, P_GOUT), (_IN_OFFS[7], D_MODEL, P_GATEA),
             (_IN_OFFS[8], D_MODEL, P_GATEB))
PACK_TM = 256


def _pack_kernel(w_ref, kpe_ref, o_ref):
    o_ref[:, P_KPE:P_KPE + LANES] = kpe_ref[...]
    o_ref[:, P_GLR:P_GLR + LANES] = jnp.zeros((PACK_TM, LANES), BF16)
    for src, width, dst in _IN_MOVES:
        o_ref[:, dst:dst + width] = w_ref[:, int(src):int(src) + width].astype(BF16)


def _pack_w_in(w_in):
    d_in = w_in.shape[1]
    kpe = w_in[:, _IN_OFFS[1] + MLA_KV_RANK:_IN_OFFS[2]][:, _deinterleave(MLA_ROPE)]
    kpe = jnp.concatenate([kpe, jnp.zeros((D_MODEL, LANES - MLA_ROPE), w_in.dtype)], axis=1).astype(BF16)
    return pl.pallas_call(
        _pack_kernel,
        grid=(D_MODEL // PACK_TM,),
        in_specs=[pl.BlockSpec((PACK_TM, d_in), lambda i: (i, 0)),
                  pl.BlockSpec((PACK_TM, LANES), lambda i: (i, 0))],
        out_specs=pl.BlockSpec((PACK_TM, P_TOTAL), lambda i: (i, 0)),
        out_shape=jax.ShapeDtypeStruct((D_MODEL, P_TOTAL), BF16),
        compiler_params=_cparams(("arbitrary",)),
        name="pack_w_in",
    )(w_in, kpe)


def _pack_w_q_b(w):
    w = w.reshape(MLA_Q_RANK, MLA_HEADS, MLA_NOPE + MLA_ROPE)
    nope = w[:, :, :MLA_NOPE]
    pe = w[:, :, MLA_NOPE:][:, :, _deinterleave(MLA_ROPE)]
    pad = jnp.zeros((MLA_Q_RANK, MLA_HEADS, MLA_HD - MLA_NOPE - MLA_ROPE), w.dtype)
    return jnp.concatenate([nope, pe, pad], axis=-1).reshape(MLA_Q_RANK, MLA_HEADS * MLA_HD).astype(BF16)


def _rope_consts():
    inv_freq = ROPE_THETA ** (-(jnp.arange(0, MLA_ROPE, 2, dtype=F32) / MLA_ROPE))
    half = MLA_ROPE // 2
    z = jnp.zeros((LANES - MLA_ROPE,), F32)
    rows = [jnp.concatenate([inv_freq, inv_freq, z]),
            jnp.concatenate([-jnp.ones((half,), F32), jnp.zeros((half,), F32), z]),
            jnp.concatenate([jnp.zeros((half,), F32), jnp.ones((half,), F32), z])]
    return jnp.concatenate([jnp.stack(rows), jnp.zeros((5, LANES), F32)], axis=0)


def kernel(x, c, positions, w_ada, b_ada, norm_mix_w, w_in, mla_q_norm_w, mla_w_q_b, mla_kv_norm_w,
           mla_w_kv_b, gla_w_gk_up, gla_b_gk_up, gla_norm_w, w_o_mla, w_o_gla, w_out, norm_ffn_w,
           w_router, b_router, w1, b1, w2, b2, w_ada_final, b_ada_final, norm_final_w):
    assert w_ada.shape[0] == 1, "single-layer stack"
    x2d = x.reshape(N_TOK, D_MODEL)
    c_t = c.T

    mod = _adaln(c_t, w_ada[0], b_ada[0])
    fmod = _adaln(c_t, w_ada_final, b_ada_final)
    sh_m, sc_m, g_m, sh_f, sc_f, g_f = [m.reshape(BATCH, 1, D_MODEL) for m in jnp.split(mod, N_MOD, axis=-1)]
    sh_o, sc_o = [m.reshape(BATCH, 1, D_MODEL) for m in jnp.split(fmod, 2, axis=-1)]

    proj = _inproj(x2d, norm_mix_w[0].reshape(1, D_MODEL), sh_m, sc_m, _pack_w_in(w_in[0]))

    q, k, v = _mla_prep(proj, positions.reshape(N_TOK, 1), _rope_consts(),
                        mla_q_norm_w[0].reshape(1, MLA_Q_RANK), mla_kv_norm_w[0].reshape(1, MLA_KV_RANK),
                        _pack_w_q_b(mla_w_q_b[0]), mla_w_kv_b[0].astype(BF16))
    o_mla = _attention(q, k, v)

    wgk = jnp.concatenate([gla_w_gk_up[0], jnp.zeros((LANES - GLA_GATE_RANK, GLA_HEADS * GLA_DK), F32)], axis=0)
    o_gla = _gla(proj, wgk, gla_b_gk_up[0].reshape(1, GLA_HEADS * GLA_DK), gla_norm_w[0].reshape(1, GLA_DV))

    wr = jnp.concatenate([w_router[0], jnp.zeros((D_MODEL, LANES - N_EXPERTS), F32)], axis=1)
    wr_hi = wr.astype(BF16)
    wr_lo = (wr - wr_hi.astype(F32)).astype(BF16)
    br = jnp.concatenate([b_router[0], jnp.zeros((LANES - N_EXPERTS,), F32)]).reshape(1, LANES)
    x1, h2, top_idx, top_w = _merge(
        x2d, o_mla, o_gla, proj, g_m, sh_f, sc_f, norm_ffn_w[0].reshape(1, D_MODEL),
        w_o_mla[0].astype(BF16), w_o_gla[0].astype(BF16), w_out[0].astype(BF16),
        jnp.concatenate([wr_hi, wr_lo], axis=1), br)

    tables = _route_tables(top_idx[:, :TOP_K])
    y4 = _moe(*tables, h2, w1[0], b1[0], w2[0], b2[0])

    out = _final(x1, y4, top_w, g_f, sh_o, sc_o, norm_final_w.reshape(1, D_MODEL))
    return out.reshape(BATCH, SEQ, D_MODEL)
```
